```python
import math
import jax, jax.numpy as jnp
from jax import lax
import numpy as np

D_MODEL = 1024
BATCH = 4
SEQ = 4096
DEPTH = 1
DEC_BATCH = 128
DEC_SEQ = 4
PAST_LEN = 2048
PAGE_SIZE = 128

D_MIX = D_MODEL
D_LIN = D_MIX // 2
D_ATT = D_MIX - D_LIN
LIN_HEAD_DIM = 128
LIN_HEADS = D_LIN // LIN_HEAD_DIM
ATT_HEAD_DIM = 64
ATT_HEADS = D_ATT // ATT_HEAD_DIM
MOBA_BLOCK = 256
MOBA_TOPK = 3
MOBA_Q_ROWS = 128
HGRN_CHUNK = 64
N_BUCKETS = 32
MAX_DISTANCE = 1024
EPS = 1e-6
D_IN = 4 * D_LIN + 4 * D_ATT
SPLITS = [D_LIN, 2 * D_LIN, 3 * D_LIN, 4 * D_LIN,
          4 * D_LIN + D_ATT, 4 * D_LIN + 2 * D_ATT, 4 * D_LIN + 3 * D_ATT]

kernel_name = "hymba_hgrn2_moba_decode_step"


def rmsnorm(x, g):
    xf = x.astype(jnp.float32)
    xf = xf * lax.rsqrt(jnp.mean(xf * xf, axis=-1, keepdims=True) + EPS)
    return (xf * g.astype(jnp.float32)).astype(x.dtype)


def hgrn2_scan(q, log_f, k, v, s0):
    B, T, H, DK = q.shape
    DV = v.shape[-1]
    C = math.gcd(T, HGRN_CHUNK)
    N = T // C

    def to_chunks(a):
        return a.reshape(B, N, C, H, a.shape[-1]).transpose(1, 0, 3, 2, 4)

    causal = jnp.tril(jnp.ones((C, C), dtype=bool))[:, :, None]

    def step(S, inp):
        qi, gi, ki, vi = inp
        b = jnp.cumsum(gi, axis=2)
        inter = jnp.einsum('bhtk,bhkv->bhtv', qi * jnp.exp(b), S)
        rel = b[:, :, :, None, :] - b[:, :, None, :, :]
        decay = jnp.exp(jnp.where(causal, rel, -jnp.inf))
        att = jnp.einsum('bhtk,bhsk,bhtsk->bhts', qi, ki, decay)
        intra = jnp.einsum('bhts,bhsv->bhtv', att, vi)
        b_last = b[:, :, -1:, :]
        S_new = jnp.exp(b_last[:, :, 0, :])[..., None] * S + jnp.einsum(
            'bhsk,bhsv->bhkv', ki * jnp.exp(b_last - b), vi)
        return S_new, inter + intra

    S_final, o = lax.scan(step, s0, (to_chunks(q), to_chunks(log_f), to_chunks(k), to_chunks(v)))
    o = o.transpose(1, 0, 3, 2, 4).reshape(B, T, H, DV)
    return o, S_final


def t5_bucket(dist):
    max_exact = N_BUCKETS // 2
    n = jnp.maximum(dist, 0)
    large = max_exact + (jnp.log(jnp.maximum(n, 1).astype(jnp.float32) / max_exact)
                         / math.log(MAX_DISTANCE / max_exact) * (N_BUCKETS - max_exact)).astype(jnp.int32)
    large = jnp.minimum(large, N_BUCKETS - 1)
    return jnp.where(n < max_exact, n, large)


def moba_attention(q, k, v, q_pos, rel_bias):
    B, T, H, hd = q.shape
    L = k.shape[1]
    nb = -(-L // MOBA_BLOCK)
    pad = nb * MOBA_BLOCK - L
    k = jnp.pad(k, ((0, 0), (0, pad), (0, 0), (0, 0)))
    v = jnp.pad(v, ((0, 0), (0, pad), (0, 0), (0, 0)))
    kbt = k.reshape(B, nb, MOBA_BLOCK, H, hd).transpose(0, 3, 1, 2, 4)
    vbt = v.reshape(B, nb, MOBA_BLOCK, H, hd).transpose(0, 3, 1, 2, 4)
    k_mean = jnp.mean(kbt.astype(jnp.float32), axis=3)
    n_sel = min(MOBA_TOPK, nb)
    qb = math.gcd(T, max(1, MOBA_Q_ROWS // B))
    n_qb = T // qb
    q_blocks = q.reshape(B, n_qb, qb, H, hd).transpose(1, 0, 3, 2, 4)
    pos_blocks = q_pos.reshape(n_qb, qb)
    bi = jnp.arange(B)[:, None, None, None]
    hi = jnp.arange(H)[None, :, None, None]
    blk_ids = jnp.arange(nb)
    offs = jnp.arange(MOBA_BLOCK)
    scale = hd ** -0.5

    def attend_block(args):
        qi, pos = args
        own = pos // MOBA_BLOCK
        gate = jnp.einsum('bhqd,bhnd->bhqn', qi.astype(jnp.float32), k_mean)
        fully_past = blk_ids[None, :] < own[:, None]
        gate = jnp.where(fully_past, gate, -jnp.inf)
        top_s, top_i = lax.top_k(gate, n_sel)
        sel = jnp.concatenate([top_i, jnp.broadcast_to(own[:, None], (B, H, qb, 1))], axis=-1)
        valid = jnp.concatenate([jnp.isfinite(top_s), jnp.ones((B, H, qb, 1), dtype=bool)], axis=-1)
        kg = kbt[bi, hi, sel]
        vg = vbt[bi, hi, sel]
        dist = pos[:, None, None] - (sel[..., None] * MOBA_BLOCK + offs)
        logits = (jnp.einsum('bhqd,bhqjpd->bhqjp', qi, kg).astype(jnp.float32) * scale
                  + rel_bias[t5_bucket(dist), hi[..., None]].astype(jnp.float32))
        logits = jnp.where(valid[..., None] & (dist >= 0), logits, -jnp.inf)
        p = jax.nn.softmax(logits.reshape(B, H, qb, -1), axis=-1).reshape(logits.shape)
        return jnp.einsum('bhqjp,bhqjpd->bqhd', p.astype(vg.dtype), vg)

    out = lax.map(attend_block, (q_blocks, pos_blocks))
    return out.transpose(1, 0, 2, 3, 4).reshape(B, T, H, hd)


def hybrid_layer(x, k_past, v_past, s0, q_pos, w_in, w_out, g_pre, g_post, g_lin, lb, rel_bias):
    B, T, _ = x.shape
    h = rmsnorm(x, g_pre)
    z = jnp.einsum('btd,de->bte', h, w_in)
    lq, lf, li, lg, aq, ak, av, ag = jnp.split(z, SPLITS, axis=-1)
    f = lb + (1.0 - lb) * jax.nn.sigmoid(lf.astype(jnp.float32))
    shp = (B, T, LIN_HEADS, LIN_HEAD_DIM)
    o_lin, s_new = hgrn2_scan(jax.nn.silu(lq.astype(jnp.float32)).reshape(shp),
                              jnp.log(f).reshape(shp), (1.0 - f).reshape(shp),
                              li.astype(jnp.float32).reshape(shp), s0.astype(jnp.float32))
    o_lin = rmsnorm(o_lin, g_lin).reshape(B, T, D_LIN).astype(x.dtype) * jax.nn.silu(lg)
    ashp = (B, T, ATT_HEADS, ATT_HEAD_DIM)
    qa, ka, va = aq.reshape(ashp), ak.reshape(ashp), av.reshape(ashp)
    k_all = ka if k_past is None else jnp.concatenate([k_past.astype(ka.dtype), ka], axis=1)
    v_all = va if v_past is None else jnp.concatenate([v_past.astype(va.dtype), va], axis=1)
    o_att = moba_attention(qa, k_all, v_all, q_pos, rel_bias).reshape(B, T, D_ATT) * jax.nn.silu(ag)
    o = jnp.einsum('bte,ed->btd', jnp.concatenate([o_lin, o_att], axis=-1), w_out)
    y = x + rmsnorm(o, g_post)
    return y, ka, va, s_new.astype(s0.dtype)


def setup_inputs(seed: int = 0) -> dict:
    key = jax.random.key(seed)
    ks = jax.random.split(key, 14)
    n_pages = PAST_LEN // PAGE_SIZE
    n_phys = (DEC_BATCH * n_pages * 5) // 4
    f32 = jnp.float32
    perm = jax.random.permutation(ks[0], n_phys)[:DEC_BATCH * n_pages]
    return {
        "x_prompt": jax.random.normal(ks[1], (BATCH, SEQ, D_MODEL), f32),
        "x_sample": jax.random.normal(ks[2], (DEC_BATCH, DEC_SEQ, D_MODEL), f32),
        "cache_k": jax.random.normal(ks[3], (DEPTH, n_phys, PAGE_SIZE, ATT_HEADS, ATT_HEAD_DIM), f32),
        "cache_v": jax.random.normal(ks[4], (DEPTH, n_phys, PAGE_SIZE, ATT_HEADS, ATT_HEAD_DIM), f32),
        "state_hgrn": 0.5 * jax.random.normal(ks[5], (DEPTH, DEC_BATCH, LIN_HEADS, LIN_HEAD_DIM, LIN_HEAD_DIM), f32),
        "page_table": perm.reshape(DEC_BATCH, n_pages).astype(jnp.int32),
        "w_in": jax.random.normal(ks[6], (DEPTH, D_MODEL, D_IN), f32) * D_MODEL ** -0.5,
        "w_out": jax.random.normal(ks[7], (DEPTH, D_MIX, D_MODEL), f32) * D_MIX ** -0.5,
        "norm_pre": 1.0 + 0.05 * jax.random.normal(ks[8], (DEPTH, D_MODEL), f32),
        "norm_post": 1.0 + 0.05 * jax.random.normal(ks[9], (DEPTH, D_MODEL), f32),
        "norm_lin_out": 1.0 + 0.05 * jax.random.normal(ks[10], (DEPTH, LIN_HEAD_DIM), f32),
        "lin_lower_bound": jax.random.normal(ks[11], (DEPTH + 1, D_LIN), f32),
        "rel_bias": 0.5 * jax.random.normal(ks[12], (N_BUCKETS, ATT_HEADS), f32),
    }


def reference(x_prompt, x_sample, cache_k, cache_v, state_hgrn, page_table, w_in, w_out,
              norm_pre, norm_post, norm_lin_out, lin_lower_bound, rel_bias):
    n_pages = page_table.shape[1]
    past_len = n_pages * PAGE_SIZE
    pos_prompt = jnp.arange(x_prompt.shape[1], dtype=jnp.int32)
    pos_sample = past_len + jnp.arange(x_sample.shape[1], dtype=jnp.int32)
    lb_all = jnp.cumsum(jax.nn.softmax(lin_lower_bound.astype(jnp.float32), axis=0), axis=0)
    yp, ys = x_prompt, x_sample
    kp_l, vp_l, sp_l, ks_l, vs_l, ss_l = [], [], [], [], [], []
    for l in range(DEPTH):
        lb = lb_all[l]
        Bp = yp.shape[0]
        s0_p = jnp.zeros((Bp, LIN_HEADS, LIN_HEAD_DIM, LIN_HEAD_DIM), state_hgrn.dtype)
        yp, kp, vp, sp = hybrid_layer(yp, None, None, s0_p, pos_prompt, w_in[l], w_out[l],
                                      norm_pre[l], norm_post[l], norm_lin_out[l], lb, rel_bias)
        Bs = ys.shape[0]
        k_past = cache_k[l][page_table].reshape(Bs, past_len, ATT_HEADS, ATT_HEAD_DIM)
        v_past = cache_v[l][page_table].reshape(Bs, past_len, ATT_HEADS, ATT_HEAD_DIM)
        ys, ksn, vsn, ssn = hybrid_layer(ys, k_past, v_past, state_hgrn[l], pos_sample, w_in[l], w_out[l],
                                         norm_pre[l], norm_post[l], norm_lin_out[l], lb, rel_bias)
        kp_l.append(kp); vp_l.append(vp); sp_l.append(sp)
        ks_l.append(ksn); vs_l.append(vsn); ss_l.append(ssn)
    k_prompt_new = jnp.stack(kp_l)
    v_prompt_new = jnp.stack(vp_l)
    state_prompt_new = jnp.stack(sp_l)
    k_sample_new = jnp.stack(ks_l)
    v_sample_new = jnp.stack(vs_l)
    state_sample_new = jnp.stack(ss_l)
    return (yp, ys, k_prompt_new, v_prompt_new, state_prompt_new, k_sample_new, v_sample_new, state_sample_new)
```

```python
import functools
import math

import jax
import jax.numpy as jnp
from jax import lax
from jax.experimental import pallas as pl
from jax.experimental.pallas import tpu as pltpu

F32 = jnp.float32
BF16 = jnp.bfloat16

EPS = 1e-6
LIN_HEAD_DIM = 128
ATT_HEAD_DIM = 64
MOBA_BLOCK = 256
MOBA_TOPK = 3
N_BUCKETS = 32
MAX_DISTANCE = 1024
NEG = -2e30
M_INIT = -1e30
LANES = 128
SUBLANES = 8
VMEM_LIMIT = 56 * 1024 * 1024

NT_DIMS = (((1,), (1,)), ((), ()))
TN_DIMS = (((0,), (0,)), ((), ()))


def _sigmoid(x):
    return 1.0 / (1.0 + jnp.exp(-x))


def _silu(x):
    return x * _sigmoid(x)


def _rms(x, g):
    return x * lax.rsqrt(jnp.mean(x * x, axis=-1, keepdims=True) + EPS) * g


def _in_head(lane, head):
    return (lane >= head * ATT_HEAD_DIM) & (lane < (head + 1) * ATT_HEAD_DIM)


def _in_proj_body(x_ref, gpre_ref, w_ref, llb_ref,
                  q_o, g_o, kk_o, v_o, gl_o, aq_o, ak_o, av_o, ga_o, *, d_lin, d_att):
    h = _rms(x_ref[...], gpre_ref[...]).astype(BF16)

    def proj(col, width):
        return jnp.dot(h, w_ref[:, col:col + width], preferred_element_type=F32)

    llb = llb_ref[...]
    e = jnp.exp(llb - jnp.max(llb, axis=0, keepdims=True))
    lb = e[0:1, :] / jnp.sum(e, axis=0, keepdims=True)

    q_o[...] = _silu(proj(0, d_lin))
    f = lb + (1.0 - lb) * _sigmoid(proj(d_lin, d_lin))
    g_o[...] = jnp.log(f)
    kk_o[...] = 1.0 - f
    v_o[...] = proj(2 * d_lin, d_lin)
    gl_o[...] = _silu(proj(3 * d_lin, d_lin))
    a0 = 4 * d_lin
    aq_o[...] = proj(a0, d_att)
    ak_o[...] = proj(a0 + d_att, d_att)
    av_o[...] = proj(a0 + 2 * d_att, d_att)
    ga_o[...] = _silu(proj(a0 + 3 * d_att, d_att))


def _in_proj(x, gpre, w_bf, llb, *, d_lin, d_att, tm=256):
    rows, d_model = x.shape
    tm = min(tm, rows)
    assert rows % tm == 0
    d_in = w_bf.shape[1]
    widths = [d_lin] * 5 + [d_att] * 4
    return pl.pallas_call(
        functools.partial(_in_proj_body, d_lin=d_lin, d_att=d_att),
        grid=(rows // tm,),
        in_specs=[
            pl.BlockSpec((tm, d_model), lambda i: (i, 0)),
            pl.BlockSpec((1, d_model), lambda i: (0, 0)),
            pl.BlockSpec((d_model, d_in), lambda i: (0, 0)),
            pl.BlockSpec(llb.shape, lambda i: (0, 0)),
        ],
        out_specs=[pl.BlockSpec((tm, w), lambda i: (i, 0)) for w in widths],
        out_shape=[jax.ShapeDtypeStruct((rows, w), F32) for w in widths],
        compiler_params=pltpu.CompilerParams(
            dimension_semantics=("arbitrary",), vmem_limit_bytes=VMEM_LIMIT),
        name="in_proj",
    )(x, gpre, w_bf, llb)


def _hgrn_body(q_ref, g_ref, kk_ref, v_ref, gl_ref, s0_ref, glin_ref,
               o_ref, sout_ref, st_scr, b_scr, *, chunk, valid, n_chunks, n_seq):
    it = pl.program_id(2)

    @pl.when(it == 0)
    def _():
        for s in range(n_seq):
            st_scr[s] = s0_ref[s, 0].T

    row = lax.broadcasted_iota(jnp.int32, (chunk, 1), 0)

    for s in range(n_seq):
        def chunk_step(ci, carry, s=s):
            r0 = pl.multiple_of(ci * chunk, chunk)
            acc = jnp.zeros((1, LIN_HEAD_DIM), F32)
            for r in range(chunk):
                if r < valid:
                    acc = acc + g_ref[s, pl.ds(r0 + r, 1), :]
                b_scr[pl.ds(r, 1), :] = acc
            b = b_scr[...]
            b_last = b_scr[pl.ds(chunk - 1, 1), :]
            q = q_ref[s, pl.ds(r0, chunk), :]
            kk = kk_ref[s, pl.ds(r0, chunk), :]
            v = v_ref[s, pl.ds(r0, chunk), :]
            if valid < chunk:
                kk = jnp.where(row < valid, kk, 0.0)
            st = st_scr[s]
            o = lax.dot_general((q * jnp.exp(b)).astype(BF16), st.astype(BF16), NT_DIMS,
                                preferred_element_type=F32)
            for j in range(valid):
                bj = b_scr[pl.ds(j, 1), :]
                kj = kk_ref[s, pl.ds(r0 + j, 1), :]
                vj = v_ref[s, pl.ds(r0 + j, 1), :]
                w = q * kj * jnp.exp(jnp.minimum(b - bj, 0.0))
                a = jnp.sum(w, axis=-1, keepdims=True)
                o = o + jnp.where(row >= j, a, 0.0) * vj
            kt = kk * jnp.exp(b_last - b)
            st_scr[s] = st * jnp.exp(b_last) + lax.dot_general(
                v.astype(BF16), kt.astype(BF16), TN_DIMS, preferred_element_type=F32)
            o_ref[s, pl.ds(r0, chunk), :] = _rms(o, glin_ref[...]) * gl_ref[s, pl.ds(r0, chunk), :]
            return carry

        lax.fori_loop(0, n_chunks, chunk_step, 0)

    @pl.when(it == pl.num_programs(2) - 1)
    def _():
        for s in range(n_seq):
            sout_ref[s, 0] = st_scr[s].T


def _hgrn(q, g, kk, v, gl, s0, glin, *, chunk, valid, t_tile, n_seq):
    nb, t, d_lin = q.shape
    heads = d_lin // LIN_HEAD_DIM
    seq_spec = pl.BlockSpec((n_seq, t_tile, LIN_HEAD_DIM), lambda b, h, i: (b, i, h))
    st_spec = pl.BlockSpec((n_seq, 1, LIN_HEAD_DIM, LIN_HEAD_DIM), lambda b, h, i: (b, h, 0, 0))
    return pl.pallas_call(
        functools.partial(_hgrn_body, chunk=chunk, valid=valid,
                          n_chunks=t_tile // chunk, n_seq=n_seq),
        grid=(nb // n_seq, heads, t // t_tile),
        in_specs=[seq_spec] * 5 + [st_spec, pl.BlockSpec((1, LIN_HEAD_DIM), lambda b, h, i: (0, 0))],
        out_specs=[seq_spec, st_spec],
        out_shape=[jax.ShapeDtypeStruct((nb, t, d_lin), F32),
                   jax.ShapeDtypeStruct(s0.shape, F32)],
        scratch_shapes=[pltpu.VMEM((n_seq, LIN_HEAD_DIM, LIN_HEAD_DIM), F32),
                        pltpu.VMEM((chunk, LIN_HEAD_DIM), F32)],
        compiler_params=pltpu.CompilerParams(
            dimension_semantics=("arbitrary", "arbitrary", "arbitrary")),
        name="hgrn",
    )(q, g, kk, v, gl, s0, glin)


def _t5_bias(dist, rb_ref, head):
    max_exact = N_BUCKETS // 2
    n = jnp.maximum(dist, 0)
    large = max_exact + (jnp.log(jnp.maximum(n, 1).astype(F32) / max_exact)
                         / math.log(MAX_DISTANCE / max_exact) * (N_BUCKETS - max_exact)).astype(jnp.int32)
    bucket = jnp.where(n < max_exact, n, jnp.minimum(large, N_BUCKETS - 1))
    out = jnp.zeros(dist.shape, F32)
    for bk in range(N_BUCKETS):
        out = jnp.where(bucket == bk, rb_ref[bk, head], out)
    return out


def _prompt_bias_body(rb_ref, o_ref, *, n_near):
    h = pl.program_id(0)
    d = pl.program_id(1)
    r = lax.broadcasted_iota(jnp.int32, (MOBA_BLOCK, MOBA_BLOCK), 0)
    c = lax.broadcasted_iota(jnp.int32, (MOBA_BLOCK, MOBA_BLOCK), 1)
    dist = d * MOBA_BLOCK + r - c
    bias = _t5_bias(dist, rb_ref, h)
    o_ref[0, 0] = jnp.where(dist >= 0, bias, NEG)


def _prompt_bias(rel_bias, n_tiles):
    heads = rel_bias.shape[1]
    return pl.pallas_call(
        functools.partial(_prompt_bias_body, n_near=n_tiles),
        grid=(heads, n_tiles),
        in_specs=[pl.BlockSpec(memory_space=pltpu.SMEM)],
        out_specs=pl.BlockSpec((1, 1, MOBA_BLOCK, MOBA_BLOCK), lambda h, d: (h, d, 0, 0)),
        out_shape=jax.ShapeDtypeStruct((heads, n_tiles, MOBA_BLOCK, MOBA_BLOCK), F32),
        name="prompt_bias",
    )(rel_bias)


def _sample_bias_body(rb_ref, o_ref, *, past_len, n_valid, rows, heads):
    h = pl.program_id(0)
    width = (past_len + LANES // heads) * heads
    r = lax.broadcasted_iota(jnp.int32, (rows, width), 0)
    col = lax.broadcasted_iota(jnp.int32, (rows, width), 1)
    kpos = col >> int(math.log2(heads))
    khead = col & (heads - 1)
    dist = past_len + r - kpos
    bias = _t5_bias(dist, rb_ref, h)
    ok = (khead == h) & (dist >= 0) & (kpos < past_len + n_valid)
    o_ref[...] = jnp.where(ok, bias, NEG)


def _sample_bias(rel_bias, *, past_len, n_valid, rows):
    heads = rel_bias.shape[1]
    assert heads & (heads - 1) == 0 and LANES % heads == 0
    width = (past_len + LANES // heads) * heads
    return pl.pallas_call(
        functools.partial(_sample_bias_body, past_len=past_len, n_valid=n_valid, rows=rows, heads=heads),
        grid=(heads,),
        in_specs=[pl.BlockSpec(memory_space=pltpu.SMEM)],
        out_specs=pl.BlockSpec((rows, width), lambda h: (h, 0)),
        out_shape=jax.ShapeDtypeStruct((heads * rows, width), F32),
        name="sample_bias",
    )(rel_bias)


def _topk_keep(gate, idx, own, axis, n):
    past = idx < own
    gm = jnp.where(past, gate, -jnp.inf)
    rank = jnp.zeros(gate.shape, jnp.int32)
    for jp in range(n):
        gj = lax.slice_in_dim(gm, jp, jp + 1, axis=axis)
        beats = (gj > gm) | ((gj == gm) & (jp < idx))
        rank = rank + beats.astype(jnp.int32)
    keep = ((rank < MOBA_TOPK) & past) | (idx == own)
    return jnp.where(keep, 0.0, NEG)


def _moba_gate_body(q_ref, k_ref, a_ref, km_scr, *, n_blocks):
    t = n_blocks * MOBA_BLOCK
    q2 = q_ref[0]
    lane = lax.broadcasted_iota(jnp.int32, (1, LANES), 1)
    for n in range(n_blocks):
        km_scr[pl.ds(n, 1), :] = jnp.sum(
            k_ref[0, n * MOBA_BLOCK:(n + 1) * MOBA_BLOCK, :], axis=0, keepdims=True) * (1.0 / MOBA_BLOCK)
    km = km_scr[...]
    blk = lax.broadcasted_iota(jnp.int32, (n_blocks, t), 0)
    own = lax.broadcasted_iota(jnp.int32, (n_blocks, t), 1) >> int(math.log2(MOBA_BLOCK))
    parts = []
    for hh in range(LANES // ATT_HEAD_DIM):
        kmh = jnp.where(_in_head(lane, hh), km, 0.0)
        gate_t = lax.dot_general(kmh, q2, NT_DIMS, precision=lax.Precision.HIGHEST,
                                 preferred_element_type=F32)
        parts.append(_topk_keep(gate_t, blk, own, axis=0, n=n_blocks))
    used = len(parts) * n_blocks
    parts.append(jnp.zeros((LANES - used, t), F32))
    a_ref[0, 0] = jnp.concatenate(parts, axis=0).T


def _moba_gate(aq, ak):
    b, t, d_att = aq.shape
    pairs = d_att // LANES
    n_blocks = t // MOBA_BLOCK
    seq = pl.BlockSpec((1, t, LANES), lambda i, p: (i, 0, p))
    return pl.pallas_call(
        functools.partial(_moba_gate_body, n_blocks=n_blocks),
        grid=(b, pairs),
        in_specs=[seq, seq],
        out_specs=pl.BlockSpec((1, 1, t, LANES), lambda i, p: (i, p, 0, 0)),
        out_shape=jax.ShapeDtypeStruct((b, pairs, t, LANES), F32),
        scratch_shapes=[pltpu.VMEM((n_blocks, LANES), F32)],
        compiler_params=pltpu.CompilerParams(
            dimension_semantics=("arbitrary", "arbitrary"), vmem_limit_bytes=VMEM_LIMIT),
        name="moba_gate",
    )(aq, ak)


def _moba_prompt_body(q_ref, k_ref, v_ref, a_ref, bias_ref, o_ref, m_scr, l_scr, acc_scr, *, n_bias):
    i = pl.program_id(2)
    heads = LANES // ATT_HEAD_DIM
    lane = lax.broadcasted_iota(jnp.int32, (1, LANES), 1)
    lane_full = lax.broadcasted_iota(jnp.int32, (MOBA_BLOCK, LANES), 1)
    q2 = q_ref[0] * (ATT_HEAD_DIM ** -0.5)
    keep = a_ref[0, 0]
    q_aug = [jnp.concatenate([jnp.where(_in_head(lane, hh), q2, 0.0), keep], axis=1).astype(BF16)
             for hh in range(heads)]
    n_blocks = pl.num_programs(2)
    for hh in range(heads):
        m_scr[hh] = jnp.full((MOBA_BLOCK, 1), M_INIT, F32)
        l_scr[hh] = jnp.zeros((MOBA_BLOCK, 1), F32)
        acc_scr[hh] = jnp.zeros((MOBA_BLOCK, LANES), F32)

    def step(j, carry):
        r0 = pl.multiple_of(j * MOBA_BLOCK, MOBA_BLOCK)
        k2 = k_ref[0, pl.ds(r0, MOBA_BLOCK), :]
        v2 = v_ref[0, pl.ds(r0, MOBA_BLOCK), :].astype(BF16)
        d = jnp.minimum(i - j, n_bias - 1)
        for hh in range(heads):
            onehot = jnp.where(lane_full == hh * n_blocks + j, 1.0, 0.0)
            k_aug = jnp.concatenate([k2, onehot], axis=1).astype(BF16)
            s = lax.dot_general(q_aug[hh], k_aug, NT_DIMS, preferred_element_type=F32)
            s = s + bias_ref[hh, d]
            m_old = m_scr[hh]
            m_new = jnp.maximum(m_old, jnp.max(s, axis=-1, keepdims=True))
            p = jnp.exp(s - m_new)
            alpha = jnp.exp(m_old - m_new)
            l_scr[hh] = alpha * l_scr[hh] + jnp.sum(p, axis=-1, keepdims=True)
            acc_scr[hh] = alpha * acc_scr[hh] + jnp.dot(p.astype(BF16), v2, preferred_element_type=F32)
            m_scr[hh] = m_new
        return carry

    lax.fori_loop(0, i + 1, step, 0)
    out = jnp.zeros((MOBA_BLOCK, LANES), F32)
    for hh in range(heads):
        out = jnp.where(_in_head(lane, hh), acc_scr[hh] / l_scr[hh], out)
    o_ref[0] = out


def _moba_prompt(aq, ak, av, keep, bias):
    b, t, d_att = aq.shape
    pairs = d_att // LANES
    n_blocks = t // MOBA_BLOCK
    heads = LANES // ATT_HEAD_DIM
    n_bias = bias.shape[1]
    seq = pl.BlockSpec((1, t, LANES), lambda ib, p, i: (ib, 0, p))
    tile = pl.BlockSpec((1, MOBA_BLOCK, LANES), lambda ib, p, i: (ib, i, p))
    return pl.pallas_call(
        functools.partial(_moba_prompt_body, n_bias=n_bias),
        grid=(b, pairs, n_blocks),
        in_specs=[tile, seq, seq,
                  pl.BlockSpec((1, 1, MOBA_BLOCK, LANES), lambda ib, p, i: (ib, p, i, 0)),
                  pl.BlockSpec((heads, n_bias, MOBA_BLOCK, MOBA_BLOCK), lambda ib, p, i: (p, 0, 0, 0))],
        out_specs=tile,
        out_shape=jax.ShapeDtypeStruct((b, t, d_att), F32),
        scratch_shapes=[pltpu.VMEM((heads, MOBA_BLOCK, 1), F32),
                        pltpu.VMEM((heads, MOBA_BLOCK, 1), F32),
                        pltpu.VMEM((heads, MOBA_BLOCK, LANES), F32)],
        compiler_params=pltpu.CompilerParams(
            dimension_semantics=("arbitrary", "arbitrary", "arbitrary"), vmem_limit_bytes=VMEM_LIMIT),
        name="moba_prompt",
    )(aq, ak, av, keep, bias)


def _moba_sample_body(pt_ref, q_ref, kn_ref, vn_ref, bias_ref, *rest, n_pages, page, heads, rows):
    k_pages = rest[:n_pages]
    v_pages = rest[n_pages:2 * n_pages]
    o_ref = rest[2 * n_pages]
    s_scr = rest[2 * n_pages + 1]
    n_q = heads * rows
    cols = page * heads
    pages_per_block = MOBA_BLOCK // page
    n_blocks = n_pages // pages_per_block

    q = q_ref[0] * (ATT_HEAD_DIM ** -0.5)
    q_bf = q.astype(BF16)

    km = []
    for n in range(n_blocks):
        tot = jnp.zeros((heads, ATT_HEAD_DIM), F32)
        for p in range(pages_per_block):
            tot = tot + jnp.sum(k_pages[n * pages_per_block + p][...], axis=0)
        km.append(tot * (1.0 / MOBA_BLOCK))
    km = jnp.concatenate(km, axis=0)
    g = lax.dot_general(q, km, NT_DIMS, precision=lax.Precision.HIGHEST, preferred_element_type=F32)
    r_idx = lax.broadcasted_iota(jnp.int32, g.shape, 0)
    c_idx = lax.broadcasted_iota(jnp.int32, g.shape, 1)
    log_rows = int(math.log2(rows))
    log_heads = int(math.log2(heads))
    g = jnp.where((r_idx >> log_rows) == (c_idx & (heads - 1)), g, 0.0)
    fold_r = lax.broadcasted_iota(jnp.int32, (g.shape[1], LANES), 0)
    fold_c = lax.broadcasted_iota(jnp.int32, (g.shape[1], LANES), 1)
    fold = jnp.where((fold_r >> log_heads) == fold_c, 1.0, 0.0)
    gate = jnp.dot(g, fold, precision=lax.Precision.HIGHEST, preferred_element_type=F32)
    blk = lax.broadcasted_iota(jnp.int32, gate.shape, 1)
    keep = _topk_keep(gate, blk, jnp.full(gate.shape, n_blocks, jnp.int32), axis=1, n=n_blocks)

    def keys(p, page_refs, new_ref):
        if p < n_pages:
            return page_refs[p][...].reshape(cols, ATT_HEAD_DIM).astype(BF16)
        new = new_ref[0]
        return jnp.concatenate([new, jnp.zeros((LANES - new.shape[0], ATT_HEAD_DIM), F32)], axis=0).astype(BF16)

    def col_range(p):
        return (p * cols, (p + 1) * cols) if p < n_pages else (n_pages * cols, n_pages * cols + LANES)

    m = jnp.full((n_q, 1), M_INIT, F32)
    for p in range(n_pages + 1):
        lo, hi = col_range(p)
        s = lax.dot_general(q_bf, keys(p, k_pages, kn_ref), NT_DIMS, preferred_element_type=F32)
        s = s + bias_ref[:, lo:hi]
        if p < n_pages:
            n = p // pages_per_block
            s = s + keep[:, n:n + 1]
        s_scr[:, lo:hi] = s
        m = jnp.maximum(m, jnp.max(s, axis=-1, keepdims=True))

    l = jnp.zeros((n_q, 1), F32)
    acc = jnp.zeros((n_q, ATT_HEAD_DIM), F32)
    for p in range(n_pages + 1):
        lo, hi = col_range(p)
        pr = jnp.exp(s_scr[:, lo:hi] - m)
        l = l + jnp.sum(pr, axis=-1, keepdims=True)
        acc = acc + jnp.dot(pr.astype(BF16), keys(p, v_pages, vn_ref), preferred_element_type=F32)
    o_ref[0] = acc / l


def _moba_sample(q_rows, k_new, v_new, cache_k, cache_v, page_table, bias, *, heads, rows):
    nb, n_q, hd = q_rows.shape
    n_pages = page_table.shape[1]
    page = cache_k.shape[2]
    assert hd == ATT_HEAD_DIM and cache_k.shape[3:] == (heads, hd) and MOBA_BLOCK % page == 0
    assert n_q == heads * rows
    tok = pl.BlockSpec((1, n_q, hd), lambda b, pt: (b, 0, 0))

    def page_spec(p):
        return pl.BlockSpec((None, None, page, heads, hd), lambda b, pt, p=p: (0, pt[b, p], 0, 0, 0))

    grid_spec = pltpu.PrefetchScalarGridSpec(
        num_scalar_prefetch=1,
        grid=(nb,),
        in_specs=[tok, tok, tok, pl.BlockSpec(bias.shape, lambda b, pt: (0, 0))]
                 + [page_spec(p) for p in range(n_pages)] * 2,
        out_specs=tok,
        scratch_shapes=[pltpu.VMEM(bias.shape, F32)],
    )
    return pl.pallas_call(
        functools.partial(_moba_sample_body, n_pages=n_pages, page=page, heads=heads, rows=rows),
        grid_spec=grid_spec,
        out_shape=jax.ShapeDtypeStruct((nb, n_q, hd), F32),
        compiler_params=pltpu.CompilerParams(
            dimension_semantics=("arbitrary",), vmem_limit_bytes=VMEM_LIMIT),
        name="moba_sample",
    )(page_table, q_rows, k_new, v_new, bias, *([cache_k] * n_pages), *([cache_v] * n_pages))


def _out_proj_body(ol_ref, oa_ref, ga_ref, x_ref, w_ref, gpost_ref, y_ref, *, d_lin):
    o = jnp.dot(ol_ref[...].astype(BF16), w_ref[0:d_lin, :], preferred_element_type=F32)
    o = o + jnp.dot((oa_ref[...] * ga_ref[...]).astype(BF16), w_ref[d_lin:, :],
                    preferred_element_type=F32)
    y_ref[...] = x_ref[...] + _rms(o, gpost_ref[...])


def _out_proj(o_lin, o_att, ga, x, w_bf, gpost, *, tm=256):
    rows, d_model = x.shape
    tm = min(tm, rows)
    assert rows % tm == 0
    d_lin = o_lin.shape[1]
    d_att = o_att.shape[1]
    return pl.pallas_call(
        functools.partial(_out_proj_body, d_lin=d_lin),
        grid=(rows // tm,),
        in_specs=[
            pl.BlockSpec((tm, d_lin), lambda i: (i, 0)),
            pl.BlockSpec((tm, d_att), lambda i: (i, 0)),
            pl.BlockSpec((tm, d_att), lambda i: (i, 0)),
            pl.BlockSpec((tm, d_model), lambda i: (i, 0)),
            pl.BlockSpec(w_bf.shape, lambda i: (0, 0)),
            pl.BlockSpec((1, d_model), lambda i: (0, 0)),
        ],
        out_specs=pl.BlockSpec((tm, d_model), lambda i: (i, 0)),
        out_shape=jax.ShapeDtypeStruct((rows, d_model), F32),
        compiler_params=pltpu.CompilerParams(dimension_semantics=("arbitrary",)),
        name="out_proj",
    )(o_lin, o_att, ga, x, w_bf, gpost)


def kernel(x_prompt, x_sample, cache_k, cache_v, state_hgrn, page_table, w_in, w_out,
           norm_pre, norm_post, norm_lin_out, lin_lower_bound, rel_bias):
    depth = w_in.shape[0]
    assert depth == 1 and lin_lower_bound.shape[0] == depth + 1
    b, t, d_model = x_prompt.shape
    nb, ts, _ = x_sample.shape
    d_lin = lin_lower_bound.shape[1]
    d_att = w_out.shape[1] - d_lin
    lin_heads = d_lin // LIN_HEAD_DIM
    att_heads = rel_bias.shape[1]
    n_pages = page_table.shape[1]
    page = cache_k.shape[2]
    past_len = n_pages * page
    assert d_att == att_heads * ATT_HEAD_DIM and t % MOBA_BLOCK == 0
    assert past_len % MOBA_BLOCK == 0 and ts <= SUBLANES

    w_in_bf = w_in[0].astype(BF16)
    w_out_bf = w_out[0].astype(BF16)
    gpre, gpost, glin = norm_pre, norm_post, norm_lin_out
    proj = functools.partial(_in_proj, gpre=gpre, w_bf=w_in_bf, llb=lin_lower_bound,
                             d_lin=d_lin, d_att=d_att)

    xp = x_prompt.reshape(b * t, d_model)
    q, g, kk, v, gl, aq, ak, av, ga = [a.reshape(b, t, -1) for a in proj(xp)]
    s0 = jnp.zeros((b, lin_heads, LIN_HEAD_DIM, LIN_HEAD_DIM), F32)
    o_lin, s_prompt = _hgrn(q, g, kk, v, gl, s0, glin, chunk=16, valid=16, t_tile=512, n_seq=1)
    keep = _moba_gate(aq, ak)
    n_bias = 6
    assert (n_bias - 1) * MOBA_BLOCK - (MOBA_BLOCK - 1) >= MAX_DISTANCE
    bias = _prompt_bias(rel_bias, n_bias)
    o_att = _moba_prompt(aq, ak, av, keep, bias)
    y_prompt = _out_proj(o_lin.reshape(b * t, d_lin), o_att.reshape(b * t, d_att),
                         ga.reshape(b * t, d_att), xp, w_out_bf, gpost).reshape(b, t, d_model)

    rows = SUBLANES
    xs = jnp.pad(x_sample, ((0, 0), (0, rows - ts), (0, 0))).reshape(nb * rows, d_model)
    qs, gs, kks, vs, gls, aqs, aks, avs, gas = [a.reshape(nb, rows, -1) for a in proj(xs)]
    o_lin_s, s_sample = _hgrn(qs, gs, kks, vs, gls, state_hgrn[0], glin,
                              chunk=rows, valid=ts, t_tile=rows, n_seq=8)
    bias_s = _sample_bias(rel_bias, past_len=past_len, n_valid=ts, rows=rows)
    q_rows = aqs.reshape(nb, rows, att_heads, ATT_HEAD_DIM).transpose(0, 2, 1, 3).reshape(
        nb, att_heads * rows, ATT_HEAD_DIM)
    o_rows = _moba_sample(q_rows, aks.reshape(nb, rows * att_heads, ATT_HEAD_DIM),
                          avs.reshape(nb, rows * att_heads, ATT_HEAD_DIM),
                          cache_k, cache_v, page_table, bias_s, heads=att_heads, rows=rows)
    o_att_s = o_rows.reshape(nb, att_heads, rows, ATT_HEAD_DIM).transpose(0, 2, 1, 3)
    y_s = _out_proj(o_lin_s.reshape(nb * rows, d_lin), o_att_s.reshape(nb * rows, d_att),
                    gas.reshape(nb * rows, d_att), xs, w_out_bf, gpost)
    y_sample = y_s.reshape(nb, rows, d_model)[:, :ts]

    kv_shape = (1, b, t, att_heads, ATT_HEAD_DIM)
    kvs_shape = (1, nb, ts, att_heads, ATT_HEAD_DIM)
    return (y_prompt, y_sample,
            ak.reshape(kv_shape), av.reshape(kv_shape), s_prompt[None],
            aks[:, :ts].reshape(kvs_shape), avs[:, :ts].reshape(kvs_shape), s_sample[None])
```

```python
import functools
import math

import jax
import jax.numpy as jnp
from jax import lax
from jax.experimental import pallas as pl
from jax.experimental.pallas import tpu as pltpu

F32 = jnp.float32
BF16 = jnp.bfloat16

EPS = 1e-6
LIN_HEAD_DIM = 128
ATT_HEAD_DIM = 64
MOBA_BLOCK = 256
MOBA_TOPK = 3
N_BUCKETS = 32
MAX_DISTANCE = 1024
NEG = -2e30
M_INIT = -1e30
LANES = 128
SUBLANES = 8
VMEM_LIMIT = 56 * 1024 * 1024
HEADS_PER_TILE = LANES // ATT_HEAD_DIM

NT_DIMS = (((1,), (1,)), ((), ()))
TN_DIMS = (((0,), (0,)), ((), ()))


def _sigmoid(x):
    return 1.0 / (1.0 + jnp.exp(-x))


def _silu(x):
    return x * _sigmoid(x)


def _rms(x, g):
    return x * lax.rsqrt(jnp.mean(x * x, axis=-1, keepdims=True) + EPS) * g


def _in_head(idx, head):
    return (idx >= head * ATT_HEAD_DIM) & (idx < (head + 1) * ATT_HEAD_DIM)


def _in_proj_body(x_ref, gpre_ref, w_ref, llb_ref, *rest, d_lin, d_att, feat_major):
    if feat_major:
        wt_ref, q_o, g_o, kk_o, v_o, gl_o, ga_o, ak_o, aqt_o, akt_o, avt_o = rest
    else:
        q_o, g_o, kk_o, v_o, gl_o, ga_o, aq_o, ak_o, av_o = rest
    h = _rms(x_ref[...], gpre_ref[...]).astype(BF16)

    def proj(col, width):
        return jnp.dot(h, w_ref[:, col:col + width], preferred_element_type=F32)

    def proj_t(row, width):
        return lax.dot_general(wt_ref[row:row + width, :], h, NT_DIMS, preferred_element_type=F32)

    llb = llb_ref[...]
    e = jnp.exp(llb - jnp.max(llb, axis=0, keepdims=True))
    lb = e[0:1, :] / jnp.sum(e, axis=0, keepdims=True)

    q_o[...] = _silu(proj(0, d_lin))
    f = lb + (1.0 - lb) * _sigmoid(proj(d_lin, d_lin))
    g_o[...] = jnp.log(f)
    kk_o[...] = 1.0 - f
    v_o[...] = proj(2 * d_lin, d_lin)
    gl_o[...] = _silu(proj(3 * d_lin, d_lin))
    a0 = 4 * d_lin
    ga_o[...] = _silu(proj(a0 + 3 * d_att, d_att))
    if feat_major:
        aqt_o[0] = proj_t(0, d_att)
        akt = proj_t(d_att, d_att)
        akt_o[0] = akt
        ak_o[...] = akt.T
        avt_o[0] = proj_t(2 * d_att, d_att)
    else:
        aq_o[...] = proj(a0, d_att)
        ak_o[...] = proj(a0 + d_att, d_att)
        av_o[...] = proj(a0 + 2 * d_att, d_att)


def _in_proj(x, gpre, w_bf, llb, *, d_lin, d_att, seq_len=None, tm=256):
    rows, d_model = x.shape
    tm = min(tm, rows)
    assert rows % tm == 0
    d_in = w_bf.shape[1]
    feat_major = seq_len is not None
    row_spec = lambda w: pl.BlockSpec((tm, w), lambda i: (i, 0))
    in_specs = [
        pl.BlockSpec((tm, d_model), lambda i: (i, 0)),
        pl.BlockSpec((1, d_model), lambda i: (0, 0)),
        pl.BlockSpec((d_model, d_in), lambda i: (0, 0)),
        pl.BlockSpec(llb.shape, lambda i: (0, 0)),
    ]
    args = [x, gpre, w_bf, llb]
    out_specs = [row_spec(d_lin)] * 5 + [row_spec(d_att)]
    out_shape = [jax.ShapeDtypeStruct((rows, d_lin), F32)] * 5 + [jax.ShapeDtypeStruct((rows, d_att), F32)]
    if feat_major:
        assert seq_len % tm == 0
        tiles = seq_len // tm
        a0 = 4 * d_lin
        wt_bf = w_bf[:, a0:a0 + 3 * d_att].T
        in_specs.append(pl.BlockSpec(wt_bf.shape, lambda i: (0, 0)))
        args.append(wt_bf)
        t_spec = pl.BlockSpec((1, d_att, tm), lambda i: (i // tiles, 0, i % tiles))
        t_shape = jax.ShapeDtypeStruct((rows // seq_len, d_att, seq_len), F32)
        out_specs += [row_spec(d_att)] + [t_spec] * 3
        out_shape += [jax.ShapeDtypeStruct((rows, d_att), F32)] + [t_shape] * 3
    else:
        out_specs += [row_spec(d_att)] * 3
        out_shape += [jax.ShapeDtypeStruct((rows, d_att), F32)] * 3
    return pl.pallas_call(
        functools.partial(_in_proj_body, d_lin=d_lin, d_att=d_att, feat_major=feat_major),
        grid=(rows // tm,),
        in_specs=in_specs,
        out_specs=out_specs,
        out_shape=out_shape,
        compiler_params=pltpu.CompilerParams(
            dimension_semantics=("arbitrary",), vmem_limit_bytes=VMEM_LIMIT),
        name="in_proj",
    )(*args)


def _hgrn_body(q_ref, g_ref, kk_ref, v_ref, gl_ref, s0_ref, glin_ref,
               o_ref, sout_ref, st_scr, b_scr, *, chunk, valid, n_chunks, n_seq):
    it = pl.program_id(2)

    @pl.when(it == 0)
    def _():
        for s in range(n_seq):
            st_scr[s] = s0_ref[s, 0].T

    row = lax.broadcasted_iota(jnp.int32, (chunk, 1), 0)

    for s in range(n_seq):
        def chunk_step(ci, carry, s=s):
            r0 = pl.multiple_of(ci * chunk, chunk)
            acc = jnp.zeros((1, LIN_HEAD_DIM), F32)
            for r in range(chunk):
                if r < valid:
                    acc = acc + g_ref[s, pl.ds(r0 + r, 1), :]
                b_scr[pl.ds(r, 1), :] = acc
            b = b_scr[...]
            b_last = b_scr[pl.ds(chunk - 1, 1), :]
            q = q_ref[s, pl.ds(r0, chunk), :]
            kk = kk_ref[s, pl.ds(r0, chunk), :]
            v = v_ref[s, pl.ds(r0, chunk), :]
            if valid < chunk:
                kk = jnp.where(row < valid, kk, 0.0)
            st = st_scr[s]
            o = lax.dot_general((q * jnp.exp(b)).astype(BF16), st.astype(BF16), NT_DIMS,
                                preferred_element_type=F32)
            for j in range(valid):
                bj = b_scr[pl.ds(j, 1), :]
                kj = kk_ref[s, pl.ds(r0 + j, 1), :]
                vj = v_ref[s, pl.ds(r0 + j, 1), :]
                w = q * kj * jnp.exp(jnp.minimum(b - bj, 0.0))
                a = jnp.sum(w, axis=-1, keepdims=True)
                o = o + jnp.where(row >= j, a, 0.0) * vj
            kt = kk * jnp.exp(b_last - b)
            st_scr[s] = st * jnp.exp(b_last) + lax.dot_general(
                v.astype(BF16), kt.astype(BF16), TN_DIMS, preferred_element_type=F32)
            o_ref[s, pl.ds(r0, chunk), :] = _rms(o, glin_ref[...]) * gl_ref[s, pl.ds(r0, chunk), :]
            return carry

        lax.fori_loop(0, n_chunks, chunk_step, 0)

    @pl.when(it == pl.num_programs(2) - 1)
    def _():
        for s in range(n_seq):
            sout_ref[s, 0] = st_scr[s].T


def _hgrn(q, g, kk, v, gl, s0, glin, *, chunk, valid, t_tile, n_seq):
    nb, t, d_lin = q.shape
    heads = d_lin // LIN_HEAD_DIM
    seq_spec = pl.BlockSpec((n_seq, t_tile, LIN_HEAD_DIM), lambda b, h, i: (b, i, h))
    st_spec = pl.BlockSpec((n_seq, 1, LIN_HEAD_DIM, LIN_HEAD_DIM), lambda b, h, i: (b, h, 0, 0))
    return pl.pallas_call(
        functools.partial(_hgrn_body, chunk=chunk, valid=valid,
                          n_chunks=t_tile // chunk, n_seq=n_seq),
        grid=(nb // n_seq, heads, t // t_tile),
        in_specs=[seq_spec] * 5 + [st_spec, pl.BlockSpec((1, LIN_HEAD_DIM), lambda b, h, i: (0, 0))],
        out_specs=[seq_spec, st_spec],
        out_shape=[jax.ShapeDtypeStruct((nb, t, d_lin), F32),
                   jax.ShapeDtypeStruct(s0.shape, F32)],
        scratch_shapes=[pltpu.VMEM((n_seq, LIN_HEAD_DIM, LIN_HEAD_DIM), F32),
                        pltpu.VMEM((chunk, LIN_HEAD_DIM), F32)],
        compiler_params=pltpu.CompilerParams(
            dimension_semantics=("arbitrary", "arbitrary", "arbitrary")),
        name="hgrn",
    )(q, g, kk, v, gl, s0, glin)


def _t5_bias(dist, rb_ref, head):
    max_exact = N_BUCKETS // 2
    n = jnp.maximum(dist, 0)
    large = max_exact + (jnp.log(jnp.maximum(n, 1).astype(F32) / max_exact)
                         / math.log(MAX_DISTANCE / max_exact) * (N_BUCKETS - max_exact)).astype(jnp.int32)
    bucket = jnp.where(n < max_exact, n, jnp.minimum(large, N_BUCKETS - 1))
    out = jnp.zeros(dist.shape, F32)
    for bk in range(N_BUCKETS):
        out = jnp.where(bucket == bk, rb_ref[bk, head], out)
    return out


def _prompt_bias_body(rb_ref, o_ref):
    h = pl.program_id(0)
    d = pl.program_id(1)
    key = lax.broadcasted_iota(jnp.int32, (MOBA_BLOCK, MOBA_BLOCK), 0)
    qry = lax.broadcasted_iota(jnp.int32, (MOBA_BLOCK, MOBA_BLOCK), 1)
    dist = d * MOBA_BLOCK + qry - key
    o_ref[0, 0] = jnp.where(dist >= 0, _t5_bias(dist, rb_ref, h), NEG)


def _prompt_bias(rel_bias, n_tiles):
    heads = rel_bias.shape[1]
    return pl.pallas_call(
        _prompt_bias_body,
        grid=(heads, n_tiles),
        in_specs=[pl.BlockSpec(memory_space=pltpu.SMEM)],
        out_specs=pl.BlockSpec((1, 1, MOBA_BLOCK, MOBA_BLOCK), lambda h, d: (h, d, 0, 0)),
        out_shape=jax.ShapeDtypeStruct((heads, n_tiles, MOBA_BLOCK, MOBA_BLOCK), F32),
        name="prompt_bias",
    )(rel_bias)


def _sample_bias_body(rb_ref, o_ref, *, past_len, n_valid, rows):
    h = pl.program_id(0)
    width = past_len + LANES
    r = lax.broadcasted_iota(jnp.int32, (rows, width), 0)
    kpos = lax.broadcasted_iota(jnp.int32, (rows, width), 1)
    dist = past_len + r - kpos
    ok = (dist >= 0) & (kpos < past_len + n_valid)
    o_ref[...] = jnp.where(ok, _t5_bias(dist, rb_ref, h), NEG)


def _sample_bias(rel_bias, *, past_len, n_valid, rows):
    heads = rel_bias.shape[1]
    width = past_len + LANES
    return pl.pallas_call(
        functools.partial(_sample_bias_body, past_len=past_len, n_valid=n_valid, rows=rows),
        grid=(heads,),
        in_specs=[pl.BlockSpec(memory_space=pltpu.SMEM)],
        out_specs=pl.BlockSpec((rows, width), lambda h: (h, 0)),
        out_shape=jax.ShapeDtypeStruct((heads * rows, width), F32),
        name="sample_bias",
    )(rel_bias)


def _topk_keep(gate, idx, own, axis, n):
    past = idx < own
    gm = jnp.where(past, gate, -jnp.inf)
    rank = jnp.zeros(gate.shape, jnp.int32)
    for jp in range(n):
        gj = lax.slice_in_dim(gm, jp, jp + 1, axis=axis)
        beats = (gj > gm) | ((gj == gm) & (jp < idx))
        rank = rank + beats.astype(jnp.int32)
    keep = ((rank < MOBA_TOPK) & past) | (idx == own)
    return jnp.where(keep, 0.0, NEG)


def _moba_gate_body(qt_ref, k_ref, a_ref, km_scr, *, n_blocks):
    t = n_blocks * MOBA_BLOCK
    qt = qt_ref[0]
    lane = lax.broadcasted_iota(jnp.int32, (1, LANES), 1)
    for n in range(n_blocks):
        km_scr[pl.ds(n, 1), :] = jnp.sum(
            k_ref[0, n * MOBA_BLOCK:(n + 1) * MOBA_BLOCK, :], axis=0, keepdims=True) * (1.0 / MOBA_BLOCK)
    km = km_scr[...]
    blk = lax.broadcasted_iota(jnp.int32, (n_blocks, t), 0)
    own = lax.broadcasted_iota(jnp.int32, (n_blocks, t), 1) >> int(math.log2(MOBA_BLOCK))
    for hh in range(HEADS_PER_TILE):
        kmh = jnp.where(_in_head(lane, hh), km, 0.0)
        gate_t = jnp.dot(kmh, qt, precision=lax.Precision.HIGHEST, preferred_element_type=F32)
        a_ref[0, 0, hh * n_blocks:(hh + 1) * n_blocks, :] = _topk_keep(gate_t, blk, own, axis=0, n=n_blocks)


def _moba_gate(aqt, ak):
    b, d_att, t = aqt.shape
    tiles = d_att // LANES
    n_blocks = t // MOBA_BLOCK
    return pl.pallas_call(
        functools.partial(_moba_gate_body, n_blocks=n_blocks),
        grid=(b, tiles),
        in_specs=[pl.BlockSpec((1, LANES, t), lambda i, p: (i, p, 0)),
                  pl.BlockSpec((1, t, LANES), lambda i, p: (i, 0, p))],
        out_specs=pl.BlockSpec((1, 1, HEADS_PER_TILE * n_blocks, t), lambda i, p: (i, p, 0, 0)),
        out_shape=jax.ShapeDtypeStruct((b, tiles, HEADS_PER_TILE * n_blocks, t), F32),
        scratch_shapes=[pltpu.VMEM((n_blocks, LANES), F32)],
        compiler_params=pltpu.CompilerParams(
            dimension_semantics=("arbitrary", "arbitrary"), vmem_limit_bytes=VMEM_LIMIT),
        name="moba_gate",
    )(aqt, ak)


def _moba_prompt_body(qt_ref, k_ref, vt_ref, keep_ref, bias_ref, o_ref, s_scr, acc_scr, *, n_bias, n_blocks):
    i = pl.program_id(2)
    feat = lax.broadcasted_iota(jnp.int32, (LANES, 1), 0)
    lane_full = lax.broadcasted_iota(jnp.int32, (MOBA_BLOCK, LANES), 1)
    qt = qt_ref[0] * (ATT_HEAD_DIM ** -0.5)
    keep = keep_ref[0, 0]
    pad = jnp.zeros((LANES - keep.shape[0], MOBA_BLOCK), F32)
    q_aug = [jnp.concatenate([jnp.where(_in_head(feat, hh), qt, 0.0), keep, pad], axis=0).astype(BF16)
             for hh in range(HEADS_PER_TILE)]

    def scores(j, m):
        r0 = pl.multiple_of(j * MOBA_BLOCK, MOBA_BLOCK)
        k2 = k_ref[0, pl.ds(r0, MOBA_BLOCK), :]
        d = jnp.minimum(i - j, n_bias - 1)
        m_new = []
        for hh in range(HEADS_PER_TILE):
            onehot = jnp.where(lane_full == hh * n_blocks + j, 1.0, 0.0)
            k_aug = jnp.concatenate([k2, onehot], axis=1).astype(BF16)
            s = jnp.dot(k_aug, q_aug[hh], preferred_element_type=F32) + bias_ref[hh, d]
            s_scr[hh, j] = s
            m_new.append(jnp.maximum(m[hh], jnp.max(s, axis=0, keepdims=True)))
        return tuple(m_new)

    m = lax.fori_loop(0, i + 1, scores,
                      tuple(jnp.full((1, MOBA_BLOCK), M_INIT, F32) for _ in range(HEADS_PER_TILE)))
    for hh in range(HEADS_PER_TILE):
        acc_scr[hh] = jnp.zeros((LANES, MOBA_BLOCK), F32)

    def values(j, l):
        r0 = pl.multiple_of(j * MOBA_BLOCK, MOBA_BLOCK)
        vt = vt_ref[0, :, pl.ds(r0, MOBA_BLOCK)].astype(BF16)
        l_new = []
        for hh in range(HEADS_PER_TILE):
            p = jnp.exp(s_scr[hh, j] - m[hh])
            l_new.append(l[hh] + jnp.sum(p, axis=0, keepdims=True))
            acc_scr[hh] += jnp.dot(vt, p.astype(BF16), preferred_element_type=F32)
        return tuple(l_new)

    l = lax.fori_loop(0, i + 1, values,
                      tuple(jnp.zeros((1, MOBA_BLOCK), F32) for _ in range(HEADS_PER_TILE)))
    out = jnp.zeros((LANES, MOBA_BLOCK), F32)
    for hh in range(HEADS_PER_TILE):
        out = jnp.where(_in_head(feat, hh), acc_scr[hh] / l[hh], out)
    o_ref[0] = out.T


def _moba_prompt(aqt, ak, avt, keep, bias):
    b, d_att, t = aqt.shape
    tiles = d_att // LANES
    n_blocks = t // MOBA_BLOCK
    n_bias = bias.shape[1]
    return pl.pallas_call(
        functools.partial(_moba_prompt_body, n_bias=n_bias, n_blocks=n_blocks),
        grid=(b, tiles, n_blocks),
        in_specs=[pl.BlockSpec((1, LANES, MOBA_BLOCK), lambda ib, p, i: (ib, p, i)),
                  pl.BlockSpec((1, t, LANES), lambda ib, p, i: (ib, 0, p)),
                  pl.BlockSpec((1, LANES, t), lambda ib, p, i: (ib, p, 0)),
                  pl.BlockSpec((1, 1, keep.shape[2], MOBA_BLOCK), lambda ib, p, i: (ib, p, 0, i)),
                  pl.BlockSpec((HEADS_PER_TILE, n_bias, MOBA_BLOCK, MOBA_BLOCK), lambda ib, p, i: (p, 0, 0, 0))],
        out_specs=pl.BlockSpec((1, MOBA_BLOCK, LANES), lambda ib, p, i: (ib, i, p)),
        out_shape=jax.ShapeDtypeStruct((b, t, d_att), F32),
        scratch_shapes=[pltpu.VMEM((HEADS_PER_TILE, n_blocks, MOBA_BLOCK, MOBA_BLOCK), F32),
                        pltpu.VMEM((HEADS_PER_TILE, LANES, MOBA_BLOCK), F32)],
        compiler_params=pltpu.CompilerParams(
            dimension_semantics=("arbitrary", "arbitrary", "arbitrary"), vmem_limit_bytes=VMEM_LIMIT),
        name="moba_prompt",
    )(aqt, ak, avt, keep, bias)


def _moba_sample_body(pt_ref, q_ref, kn_ref, vn_ref, bias_ref, *rest, n_pages, heads, rows):
    k_pages = rest[:n_pages]
    v_pages = rest[n_pages:2 * n_pages]
    o_ref = rest[2 * n_pages]
    s_scr = rest[2 * n_pages + 1]
    d_att = heads * ATT_HEAD_DIM
    page = k_pages[0].shape[-1]
    pages_per_block = MOBA_BLOCK // page
    n_blocks = n_pages // pages_per_block
    n_q = heads * rows
    feat = lax.broadcasted_iota(jnp.int32, (1, d_att), 1)
    lane = lax.broadcasted_iota(jnp.int32, (n_q, LANES), 1)

    q = q_ref[0] * (ATT_HEAD_DIM ** -0.5)
    q_bd = jnp.concatenate([jnp.where(_in_head(feat, h), q, 0.0) for h in range(heads)], axis=0).astype(BF16)

    gate = jnp.zeros((n_q, LANES), F32)
    for n in range(n_blocks):
        tot = jnp.zeros((n_q, 1), F32)
        for p in range(n * pages_per_block, (n + 1) * pages_per_block):
            kt = k_pages[p][...].reshape(d_att, page).astype(BF16)
            s = jnp.dot(q_bd, kt, preferred_element_type=F32)
            s_scr[:, p * page:(p + 1) * page] = s
            tot = tot + jnp.sum(s, axis=-1, keepdims=True)
        gate = jnp.where(lane == n, tot * (1.0 / MOBA_BLOCK), gate)
    keep = _topk_keep(gate, lane, jnp.full(gate.shape, n_blocks, jnp.int32), axis=1, n=n_blocks)

    def new_rows(ref):
        return jnp.concatenate([ref[0], jnp.zeros((LANES - rows, d_att), F32)], axis=0).astype(BF16)

    own = n_pages * page
    s_own = lax.dot_general(q_bd, new_rows(kn_ref), NT_DIMS, preferred_element_type=F32)
    s_own = s_own + bias_ref[:, own:own + LANES]
    s_scr[:, own:own + LANES] = s_own
    m = jnp.max(s_own, axis=-1, keepdims=True)
    for p in range(n_pages):
        n = p // pages_per_block
        s = s_scr[:, p * page:(p + 1) * page] + bias_ref[:, p * page:(p + 1) * page] + keep[:, n:n + 1]
        s_scr[:, p * page:(p + 1) * page] = s
        m = jnp.maximum(m, jnp.max(s, axis=-1, keepdims=True))

    pr = jnp.exp(s_scr[:, own:own + LANES] - m)
    l = jnp.sum(pr, axis=-1, keepdims=True)
    acc = jnp.dot(pr.astype(BF16), new_rows(vn_ref), preferred_element_type=F32)
    for p in range(n_pages):
        pr = jnp.exp(s_scr[:, p * page:(p + 1) * page] - m)
        l = l + jnp.sum(pr, axis=-1, keepdims=True)
        vt = v_pages[p][...].reshape(d_att, page).astype(BF16)
        acc = acc + lax.dot_general(pr.astype(BF16), vt, NT_DIMS, preferred_element_type=F32)
    acc = acc / l
    out = jnp.zeros((rows, d_att), F32)
    for h in range(heads):
        out = jnp.where(_in_head(feat, h), acc[h * rows:(h + 1) * rows, :], out)
    o_ref[0] = out


def _moba_sample(aq, ak, av, cache_kt, cache_vt, page_table, bias, *, heads):
    nb, rows, d_att = aq.shape
    n_pages = page_table.shape[1]
    page = cache_kt.shape[-1]
    assert cache_kt.shape[2:] == (heads, ATT_HEAD_DIM, page) and MOBA_BLOCK % page == 0 and page == LANES
    tok = pl.BlockSpec((1, rows, d_att), lambda b, pt: (b, 0, 0))

    def page_spec(p):
        return pl.BlockSpec((None, None, heads, ATT_HEAD_DIM, page), lambda b, pt, p=p: (0, pt[b, p], 0, 0, 0))

    grid_spec = pltpu.PrefetchScalarGridSpec(
        num_scalar_prefetch=1,
        grid=(nb,),
        in_specs=[tok, tok, tok, pl.BlockSpec(bias.shape, lambda b, pt: (0, 0))]
                 + [page_spec(p) for p in range(n_pages)] * 2,
        out_specs=tok,
        scratch_shapes=[pltpu.VMEM(bias.shape, F32)],
    )
    return pl.pallas_call(
        functools.partial(_moba_sample_body, n_pages=n_pages, heads=heads, rows=rows),
        grid_spec=grid_spec,
        out_shape=jax.ShapeDtypeStruct((nb, rows, d_att), F32),
        compiler_params=pltpu.CompilerParams(
            dimension_semantics=("arbitrary",), vmem_limit_bytes=VMEM_LIMIT),
        name="moba_sample",
    )(page_table, aq, ak, av, bias, *([cache_kt] * n_pages), *([cache_vt] * n_pages))


def _out_proj_body(ol_ref, oa_ref, ga_ref, x_ref, w_ref, gpost_ref, y_ref, *, d_lin):
    o = jnp.dot(ol_ref[...].astype(BF16), w_ref[0:d_lin, :], preferred_element_type=F32)
    o = o + jnp.dot((oa_ref[...] * ga_ref[...]).astype(BF16), w_ref[d_lin:, :],
                    preferred_element_type=F32)
    y_ref[...] = x_ref[...] + _rms(o, gpost_ref[...])


def _out_proj(o_lin, o_att, ga, x, w_bf, gpost, *, tm=256):
    rows, d_model = x.shape
    tm = min(tm, rows)
    assert rows % tm == 0
    d_lin = o_lin.shape[1]
    d_att = o_att.shape[1]
    return pl.pallas_call(
        functools.partial(_out_proj_body, d_lin=d_lin),
        grid=(rows // tm,),
        in_specs=[
            pl.BlockSpec((tm, d_lin), lambda i: (i, 0)),
            pl.BlockSpec((tm, d_att), lambda i: (i, 0)),
            pl.BlockSpec((tm, d_att), lambda i: (i, 0)),
            pl.BlockSpec((tm, d_model), lambda i: (i, 0)),
            pl.BlockSpec(w_bf.shape, lambda i: (0, 0)),
            pl.BlockSpec((1, d_model), lambda i: (0, 0)),
        ],
        out_specs=pl.BlockSpec((tm, d_model), lambda i: (i, 0)),
        out_shape=jax.ShapeDtypeStruct((rows, d_model), F32),
        compiler_params=pltpu.CompilerParams(dimension_semantics=("arbitrary",)),
        name="out_proj",
    )(o_lin, o_att, ga, x, w_bf, gpost)


def kernel(x_prompt, x_sample, cache_k, cache_v, state_hgrn, page_table, w_in, w_out,
           norm_pre, norm_post, norm_lin_out, lin_lower_bound, rel_bias):
    depth = w_in.shape[0]
    assert depth == 1 and lin_lower_bound.shape[0] == depth + 1
    b, t, d_model = x_prompt.shape
    nb, ts, _ = x_sample.shape
    d_lin = lin_lower_bound.shape[1]
    d_att = w_out.shape[1] - d_lin
    lin_heads = d_lin // LIN_HEAD_DIM
    att_heads = rel_bias.shape[1]
    n_pages = page_table.shape[1]
    page = cache_k.shape[2]
    past_len = n_pages * page
    assert d_att == att_heads * ATT_HEAD_DIM and t % MOBA_BLOCK == 0
    assert past_len % MOBA_BLOCK == 0 and ts <= SUBLANES

    w_in_bf = w_in[0].astype(BF16)
    w_out_bf = w_out[0].astype(BF16)
    gpre, gpost, glin = norm_pre, norm_post, norm_lin_out
    proj = functools.partial(_in_proj, gpre=gpre, w_bf=w_in_bf, llb=lin_lower_bound,
                             d_lin=d_lin, d_att=d_att)

    xp = x_prompt.reshape(b * t, d_model)
    q, g, kk, v, gl, ga, ak, aqt, akt, avt = proj(xp, seq_len=t)
    q, g, kk, v, gl, ak = [a.reshape(b, t, -1) for a in (q, g, kk, v, gl, ak)]
    s0 = jnp.zeros((b, lin_heads, LIN_HEAD_DIM, LIN_HEAD_DIM), F32)
    o_lin, s_prompt = _hgrn(q, g, kk, v, gl, s0, glin, chunk=16, valid=16, t_tile=512, n_seq=1)
    keep = _moba_gate(aqt, ak)
    n_bias = 6
    assert (n_bias - 1) * MOBA_BLOCK - (MOBA_BLOCK - 1) >= MAX_DISTANCE
    bias = _prompt_bias(rel_bias, n_bias)
    o_att = _moba_prompt(aqt, ak, avt, keep, bias)
    y_prompt = _out_proj(o_lin.reshape(b * t, d_lin), o_att.reshape(b * t, d_att),
                         ga, xp, w_out_bf, gpost).reshape(b, t, d_model)

    rows = SUBLANES
    xs = jnp.pad(x_sample, ((0, 0), (0, rows - ts), (0, 0))).reshape(nb * rows, d_model)
    qs, gs, kks, vs, gls, gas, aqs, aks, avs = [a.reshape(nb, rows, -1) for a in proj(xs)]
    o_lin_s, s_sample = _hgrn(qs, gs, kks, vs, gls, state_hgrn[0], glin,
                              chunk=rows, valid=ts, t_tile=rows, n_seq=8)
    bias_s = _sample_bias(rel_bias, past_len=past_len, n_valid=ts, rows=rows)
    cache_kt = cache_k.transpose(0, 1, 3, 4, 2)
    cache_vt = cache_v.transpose(0, 1, 3, 4, 2)
    o_att_s = _moba_sample(aqs, aks, avs, cache_kt, cache_vt, page_table, bias_s, heads=att_heads)
    y_s = _out_proj(o_lin_s.reshape(nb * rows, d_lin), o_att_s.reshape(nb * rows, d_att),
                    gas.reshape(nb * rows, d_att), xs, w_out_bf, gpost)
    y_sample = y_s.reshape(nb, rows, d_model)[:, :ts]

    def prompt_kv(a):
        return a.reshape(1, b, att_heads, ATT_HEAD_DIM, t).transpose(0, 1, 4, 2, 3)

    kvs_shape = (1, nb, ts, att_heads, ATT_HEAD_DIM)
    return (y_prompt, y_sample, prompt_kv(akt), prompt_kv(avt), s_prompt[None],
            aks[:, :ts].reshape(kvs_shape), avs[:, :ts].reshape(kvs_shape), s_sample[None])
```

```python
import functools
import math

import jax
import jax.numpy as jnp
from jax import lax
from jax.experimental import pallas as pl
from jax.experimental.pallas import tpu as pltpu

F32 = jnp.float32
BF16 = jnp.bfloat16

EPS = 1e-6
LIN_HEAD_DIM = 128
ATT_HEAD_DIM = 64
MOBA_BLOCK = 256
MOBA_TOPK = 3
N_BUCKETS = 32
MAX_DISTANCE = 1024
NEG = -2e30
M_INIT = -1e30
LANES = 128
SUBLANES = 8
VMEM_LIMIT = 56 * 1024 * 1024
HEADS_PER_TILE = LANES // ATT_HEAD_DIM

NT_DIMS = (((1,), (1,)), ((), ()))
TN_DIMS = (((0,), (0,)), ((), ()))


def _sigmoid(x):
    return 1.0 / (1.0 + jnp.exp(-x))


def _silu(x):
    return x * _sigmoid(x)


def _rms(x, g):
    return x * lax.rsqrt(jnp.mean(x * x, axis=-1, keepdims=True) + EPS) * g


def _in_head(idx, head):
    return (idx >= head * ATT_HEAD_DIM) & (idx < (head + 1) * ATT_HEAD_DIM)


def _in_proj_body(x_ref, gpre_ref, w_ref, llb_ref, *rest, d_lin, d_att, feat_major):
    if feat_major:
        wt_ref, q_o, g_o, kk_o, v_o, gl_o, ga_o, ak_o, aqt_o, akt_o, avt_o = rest
    else:
        q_o, g_o, kk_o, v_o, gl_o, ga_o, aq_o, ak_o, av_o = rest
    h = _rms(x_ref[...], gpre_ref[...]).astype(BF16)

    def proj(col, width):
        return jnp.dot(h, w_ref[:, col:col + width], preferred_element_type=F32)

    def proj_t(row, width):
        return lax.dot_general(wt_ref[row:row + width, :], h, NT_DIMS, preferred_element_type=F32)

    llb = llb_ref[...]
    e = jnp.exp(llb - jnp.max(llb, axis=0, keepdims=True))
    lb = e[0:1, :] / jnp.sum(e, axis=0, keepdims=True)

    q_o[...] = _silu(proj(0, d_lin))
    f = lb + (1.0 - lb) * _sigmoid(proj(d_lin, d_lin))
    g_o[...] = jnp.log(f)
    kk_o[...] = 1.0 - f
    v_o[...] = proj(2 * d_lin, d_lin)
    gl_o[...] = _silu(proj(3 * d_lin, d_lin))
    a0 = 4 * d_lin
    ga_o[...] = _silu(proj(a0 + 3 * d_att, d_att))
    if feat_major:
        aqt_o[0] = proj_t(0, d_att)
        akt = proj_t(d_att, d_att)
        akt_o[0] = akt
        ak_o[...] = akt.T
        avt_o[0] = proj_t(2 * d_att, d_att)
    else:
        aq_o[...] = proj(a0, d_att)
        ak_o[...] = proj(a0 + d_att, d_att)
        av_o[...] = proj(a0 + 2 * d_att, d_att)


def _in_proj(x, gpre, w_bf, llb, *, d_lin, d_att, seq_len=None, tm=256):
    rows, d_model = x.shape
    tm = min(tm, rows)
    assert rows % tm == 0
    d_in = w_bf.shape[1]
    feat_major = seq_len is not None
    row_spec = lambda w: pl.BlockSpec((tm, w), lambda i: (i, 0))
    in_specs = [
        pl.BlockSpec((tm, d_model), lambda i: (i, 0)),
        pl.BlockSpec((1, d_model), lambda i: (0, 0)),
        pl.BlockSpec((d_model, d_in), lambda i: (0, 0)),
        pl.BlockSpec(llb.shape, lambda i: (0, 0)),
    ]
    args = [x, gpre, w_bf, llb]
    out_specs = [row_spec(d_lin)] * 5 + [row_spec(d_att)]
    out_shape = [jax.ShapeDtypeStruct((rows, d_lin), F32)] * 5 + [jax.ShapeDtypeStruct((rows, d_att), F32)]
    if feat_major:
        assert seq_len % tm == 0
        tiles = seq_len // tm
        a0 = 4 * d_lin
        wt_bf = w_bf[:, a0:a0 + 3 * d_att].T
        in_specs.append(pl.BlockSpec(wt_bf.shape, lambda i: (0, 0)))
        args.append(wt_bf)
        t_spec = pl.BlockSpec((1, d_att, tm), lambda i: (i // tiles, 0, i % tiles))
        t_shape = jax.ShapeDtypeStruct((rows // seq_len, d_att, seq_len), F32)
        out_specs += [row_spec(d_att)] + [t_spec] * 3
        out_shape += [jax.ShapeDtypeStruct((rows, d_att), F32)] + [t_shape] * 3
    else:
        out_specs += [row_spec(d_att)] * 3
        out_shape += [jax.ShapeDtypeStruct((rows, d_att), F32)] * 3
    return pl.pallas_call(
        functools.partial(_in_proj_body, d_lin=d_lin, d_att=d_att, feat_major=feat_major),
        grid=(rows // tm,),
        in_specs=in_specs,
        out_specs=out_specs,
        out_shape=out_shape,
        compiler_params=pltpu.CompilerParams(
            dimension_semantics=("arbitrary",), vmem_limit_bytes=VMEM_LIMIT),
        name="in_proj",
    )(*args)


def _hgrn_body(q_ref, g_ref, kk_ref, v_ref, gl_ref, s0_ref, glin_ref,
               o_ref, sout_ref, st_scr, b_scr, *, chunk, valid, n_chunks, n_seq):
    it = pl.program_id(2)

    @pl.when(it == 0)
    def _():
        for s in range(n_seq):
            st_scr[s] = s0_ref[s, 0].T

    row = lax.broadcasted_iota(jnp.int32, (chunk, 1), 0)

    for s in range(n_seq):
        def chunk_step(ci, carry, s=s):
            r0 = pl.multiple_of(ci * chunk, chunk)
            acc = jnp.zeros((1, LIN_HEAD_DIM), F32)
            for r in range(chunk):
                if r < valid:
                    acc = acc + g_ref[s, pl.ds(r0 + r, 1), :]
                b_scr[pl.ds(r, 1), :] = acc
            b = b_scr[...]
            b_last = b_scr[pl.ds(chunk - 1, 1), :]
            q = q_ref[s, pl.ds(r0, chunk), :]
            kk = kk_ref[s, pl.ds(r0, chunk), :]
            v = v_ref[s, pl.ds(r0, chunk), :]
            if valid < chunk:
                kk = jnp.where(row < valid, kk, 0.0)
            st = st_scr[s]
            o = lax.dot_general((q * jnp.exp(b)).astype(BF16), st.astype(BF16), NT_DIMS,
                                preferred_element_type=F32)
            for j in range(valid):
                bj = b_scr[pl.ds(j, 1), :]
                kj = kk_ref[s, pl.ds(r0 + j, 1), :]
                vj = v_ref[s, pl.ds(r0 + j, 1), :]
                w = q * kj * jnp.exp(jnp.minimum(b - bj, 0.0))
                a = jnp.sum(w, axis=-1, keepdims=True)
                o = o + jnp.where(row >= j, a, 0.0) * vj
            kt = kk * jnp.exp(b_last - b)
            st_scr[s] = st * jnp.exp(b_last) + lax.dot_general(
                v.astype(BF16), kt.astype(BF16), TN_DIMS, preferred_element_type=F32)
            o_ref[s, pl.ds(r0, chunk), :] = _rms(o, glin_ref[...]) * gl_ref[s, pl.ds(r0, chunk), :]
            return carry

        lax.fori_loop(0, n_chunks, chunk_step, 0)

    @pl.when(it == pl.num_programs(2) - 1)
    def _():
        for s in range(n_seq):
            sout_ref[s, 0] = st_scr[s].T


def _hgrn(q, g, kk, v, gl, s0, glin, *, chunk, valid, t_tile, n_seq):
    nb, t, d_lin = q.shape
    heads = d_lin // LIN_HEAD_DIM
    seq_spec = pl.BlockSpec((n_seq, t_tile, LIN_HEAD_DIM), lambda b, h, i: (b, i, h))
    st_spec = pl.BlockSpec((n_seq, 1, LIN_HEAD_DIM, LIN_HEAD_DIM), lambda b, h, i: (b, h, 0, 0))
    return pl.pallas_call(
        functools.partial(_hgrn_body, chunk=chunk, valid=valid,
                          n_chunks=t_tile // chunk, n_seq=n_seq),
        grid=(nb // n_seq, heads, t // t_tile),
        in_specs=[seq_spec] * 5 + [st_spec, pl.BlockSpec((1, LIN_HEAD_DIM), lambda b, h, i: (0, 0))],
        out_specs=[seq_spec, st_spec],
        out_shape=[jax.ShapeDtypeStruct((nb, t, d_lin), F32),
                   jax.ShapeDtypeStruct(s0.shape, F32)],
        scratch_shapes=[pltpu.VMEM((n_seq, LIN_HEAD_DIM, LIN_HEAD_DIM), F32),
                        pltpu.VMEM((chunk, LIN_HEAD_DIM), F32)],
        compiler_params=pltpu.CompilerParams(
            dimension_semantics=("arbitrary", "arbitrary", "arbitrary")),
        name="hgrn",
    )(q, g, kk, v, gl, s0, glin)


SAFE_DECAY = 60.0


def _hgrn_tile_body(q_ref, g_ref, kk_ref, v_ref, gl_ref, s0_ref, glin_ref, o_ref, sout_ref,
                    st_scr, b_scr, bl_scr, oi_scr, kkp_scr, bp_scr, vp_scr, *, chunk, t_tile):
    it = pl.program_id(2)

    @pl.when(it == 0)
    def _():
        st_scr[...] = s0_ref[0, 0].T

    lg = int(math.log2(chunk))
    n_groups = t_tile // LANES
    r = lax.broadcasted_iota(jnp.int32, (LANES, LANES), 0)
    c = lax.broadcasted_iota(jnp.int32, (LANES, LANES), 1)
    same = (r >> lg) == (c >> lg)
    causal = same & (c <= r)
    for gi in range(n_groups):
        rows = slice(gi * LANES, (gi + 1) * LANES)
        g = g_ref[0, rows, :]
        b_scr[rows, :] = jnp.dot(jnp.where(causal, 1.0, 0.0), g, precision=lax.Precision.HIGHEST,
                                 preferred_element_type=F32)
        bl_scr[rows, :] = jnp.dot(jnp.where(same, 1.0, 0.0), g, precision=lax.Precision.HIGHEST,
                                  preferred_element_type=F32)

    def intra_mxu():
        for gi in range(n_groups):
            rows = slice(gi * LANES, (gi + 1) * LANES)
            b = b_scr[rows, :]
            qe = (q_ref[0, rows, :] * jnp.exp(b)).astype(BF16)
            ke = (kk_ref[0, rows, :] * jnp.exp(-b)).astype(BF16)
            att = lax.dot_general(qe, ke, NT_DIMS, preferred_element_type=F32)
            att = jnp.where(causal, att, 0.0).astype(BF16)
            oi_scr[rows, :] = jnp.dot(att, v_ref[0, rows, :].astype(BF16), preferred_element_type=F32)

    def intra_pairs():
        zeros = jnp.zeros((chunk, LIN_HEAD_DIM), F32)
        for dst, src in ((kkp_scr, kk_ref[0]), (bp_scr, b_scr[...]), (vp_scr, v_ref[0])):
            dst[0:chunk, :] = zeros
            dst[chunk:, :] = src
        oi_scr[...] = jnp.zeros(oi_scr.shape, F32)
        q = q_ref[0]
        b = b_scr[...]
        tmod = lax.broadcasted_iota(jnp.int32, (t_tile, 1), 0) & (chunk - 1)

        def lag(d, carry):
            start = chunk - d
            w = q * kkp_scr[pl.ds(start, t_tile), :] * jnp.exp(
                jnp.minimum(b - bp_scr[pl.ds(start, t_tile), :], 0.0))
            a = jnp.sum(w, axis=-1, keepdims=True)
            oi_scr[...] += jnp.where(tmod >= d, a, 0.0) * vp_scr[pl.ds(start, t_tile), :]
            return carry

        lax.fori_loop(0, chunk, lag, 0)

    lax.cond(jnp.min(b_scr[...]) >= -SAFE_DECAY, intra_mxu, intra_pairs)

    st = st_scr[...]
    for n in range(t_tile // chunk):
        rows = slice(n * chunk, (n + 1) * chunk)
        b = b_scr[rows, :]
        bl = bl_scr[rows, :]
        o = lax.dot_general((q_ref[0, rows, :] * jnp.exp(b)).astype(BF16), st.astype(BF16), NT_DIMS,
                            preferred_element_type=F32) + oi_scr[rows, :]
        kt = (kk_ref[0, rows, :] * jnp.exp(bl - b)).astype(BF16)
        st = st * jnp.exp(bl[0:1, :]) + lax.dot_general(
            v_ref[0, rows, :].astype(BF16), kt, TN_DIMS, preferred_element_type=F32)
        o_ref[0, rows, :] = _rms(o, glin_ref[...]) * gl_ref[0, rows, :]
    st_scr[...] = st

    @pl.when(it == pl.num_programs(2) - 1)
    def _():
        sout_ref[0, 0] = st.T


def _hgrn_tiles(q, g, kk, v, gl, s0, glin, *, chunk, t_tile):
    nb, t, d_lin = q.shape
    heads = d_lin // LIN_HEAD_DIM
    assert LANES % chunk == 0 and t_tile % LANES == 0 and t % t_tile == 0
    seq_spec = pl.BlockSpec((1, t_tile, LIN_HEAD_DIM), lambda b, h, i: (b, i, h))
    st_spec = pl.BlockSpec((1, 1, LIN_HEAD_DIM, LIN_HEAD_DIM), lambda b, h, i: (b, h, 0, 0))
    tile = pltpu.VMEM((t_tile, LIN_HEAD_DIM), F32)
    halo = pltpu.VMEM((chunk + t_tile, LIN_HEAD_DIM), F32)
    return pl.pallas_call(
        functools.partial(_hgrn_tile_body, chunk=chunk, t_tile=t_tile),
        grid=(nb, heads, t // t_tile),
        in_specs=[seq_spec] * 5 + [st_spec, pl.BlockSpec((1, LIN_HEAD_DIM), lambda b, h, i: (0, 0))],
        out_specs=[seq_spec, st_spec],
        out_shape=[jax.ShapeDtypeStruct((nb, t, d_lin), F32),
                   jax.ShapeDtypeStruct(s0.shape, F32)],
        scratch_shapes=[pltpu.VMEM((LIN_HEAD_DIM, LIN_HEAD_DIM), F32), tile, tile, tile, halo, halo, halo],
        compiler_params=pltpu.CompilerParams(
            dimension_semantics=("arbitrary", "arbitrary", "arbitrary")),
        name="hgrn_tiles",
    )(q, g, kk, v, gl, s0, glin)


def _t5_bias(dist, rb_ref, head):
    max_exact = N_BUCKETS // 2
    n = jnp.maximum(dist, 0)
    large = max_exact + (jnp.log(jnp.maximum(n, 1).astype(F32) / max_exact)
                         / math.log(MAX_DISTANCE / max_exact) * (N_BUCKETS - max_exact)).astype(jnp.int32)
    bucket = jnp.where(n < max_exact, n, jnp.minimum(large, N_BUCKETS - 1))
    out = jnp.zeros(dist.shape, F32)
    for bk in range(N_BUCKETS):
        out = jnp.where(bucket == bk, rb_ref[bk, head], out)
    return out


def _prompt_bias_body(rb_ref, o_ref):
    h = pl.program_id(0)
    d = pl.program_id(1)
    key = lax.broadcasted_iota(jnp.int32, (MOBA_BLOCK, MOBA_BLOCK), 0)
    qry = lax.broadcasted_iota(jnp.int32, (MOBA_BLOCK, MOBA_BLOCK), 1)
    dist = d * MOBA_BLOCK + qry - key
    o_ref[0, 0] = jnp.where(dist >= 0, _t5_bias(dist, rb_ref, h), NEG)


def _prompt_bias(rel_bias, n_tiles):
    heads = rel_bias.shape[1]
    return pl.pallas_call(
        _prompt_bias_body,
        grid=(heads, n_tiles),
        in_specs=[pl.BlockSpec(memory_space=pltpu.SMEM)],
        out_specs=pl.BlockSpec((1, 1, MOBA_BLOCK, MOBA_BLOCK), lambda h, d: (h, d, 0, 0)),
        out_shape=jax.ShapeDtypeStruct((heads, n_tiles, MOBA_BLOCK, MOBA_BLOCK), F32),
        name="prompt_bias",
    )(rel_bias)


def _sample_bias_body(rb_ref, o_ref, *, past_len, n_valid, rows):
    h = pl.program_id(0)
    width = past_len + LANES
    r = lax.broadcasted_iota(jnp.int32, (rows, width), 0)
    kpos = lax.broadcasted_iota(jnp.int32, (rows, width), 1)
    dist = past_len + r - kpos
    ok = (dist >= 0) & (kpos < past_len + n_valid)
    o_ref[...] = jnp.where(ok, _t5_bias(dist, rb_ref, h), NEG)


def _sample_bias(rel_bias, *, past_len, n_valid, rows):
    heads = rel_bias.shape[1]
    width = past_len + LANES
    return pl.pallas_call(
        functools.partial(_sample_bias_body, past_len=past_len, n_valid=n_valid, rows=rows),
        grid=(heads,),
        in_specs=[pl.BlockSpec(memory_space=pltpu.SMEM)],
        out_specs=pl.BlockSpec((rows, width), lambda h: (h, 0)),
        out_shape=jax.ShapeDtypeStruct((heads * rows, width), F32),
        name="sample_bias",
    )(rel_bias)


def _topk_keep(gate, idx, own, axis, n):
    past = idx < own
    gm = jnp.where(past, gate, -jnp.inf)
    rank = jnp.zeros(gate.shape, jnp.int32)
    for jp in range(n):
        gj = lax.slice_in_dim(gm, jp, jp + 1, axis=axis)
        beats = (gj > gm) | ((gj == gm) & (jp < idx))
        rank = rank + beats.astype(jnp.int32)
    keep = ((rank < MOBA_TOPK) & past) | (idx == own)
    return jnp.where(keep, 0.0, NEG)


def _moba_gate_body(qt_ref, k_ref, a_ref, km_scr, *, n_blocks):
    t = n_blocks * MOBA_BLOCK
    qt = qt_ref[0]
    lane = lax.broadcasted_iota(jnp.int32, (1, LANES), 1)
    for n in range(n_blocks):
        km_scr[pl.ds(n, 1), :] = jnp.sum(
            k_ref[0, n * MOBA_BLOCK:(n + 1) * MOBA_BLOCK, :], axis=0, keepdims=True) * (1.0 / MOBA_BLOCK)
    km = km_scr[...]
    blk = lax.broadcasted_iota(jnp.int32, (n_blocks, t), 0)
    own = lax.broadcasted_iota(jnp.int32, (n_blocks, t), 1) >> int(math.log2(MOBA_BLOCK))
    for hh in range(HEADS_PER_TILE):
        kmh = jnp.where(_in_head(lane, hh), km, 0.0)
        gate_t = jnp.dot(kmh, qt, precision=lax.Precision.HIGHEST, preferred_element_type=F32)
        a_ref[0, 0, hh * n_blocks:(hh + 1) * n_blocks, :] = _topk_keep(gate_t, blk, own, axis=0, n=n_blocks)


def _moba_gate(aqt, ak):
    b, d_att, t = aqt.shape
    tiles = d_att // LANES
    n_blocks = t // MOBA_BLOCK
    return pl.pallas_call(
        functools.partial(_moba_gate_body, n_blocks=n_blocks),
        grid=(b, tiles),
        in_specs=[pl.BlockSpec((1, LANES, t), lambda i, p: (i, p, 0)),
                  pl.BlockSpec((1, t, LANES), lambda i, p: (i, 0, p))],
        out_specs=pl.BlockSpec((1, 1, HEADS_PER_TILE * n_blocks, t), lambda i, p: (i, p, 0, 0)),
        out_shape=jax.ShapeDtypeStruct((b, tiles, HEADS_PER_TILE * n_blocks, t), F32),
        scratch_shapes=[pltpu.VMEM((n_blocks, LANES), F32)],
        compiler_params=pltpu.CompilerParams(
            dimension_semantics=("arbitrary", "arbitrary"), vmem_limit_bytes=VMEM_LIMIT),
        name="moba_gate",
    )(aqt, ak)


def _moba_prompt_body(qt_ref, k_ref, vt_ref, keep_ref, bias_ref, o_ref, s_scr, acc_scr, *, n_bias, n_blocks):
    i = pl.program_id(2)
    feat = lax.broadcasted_iota(jnp.int32, (LANES, 1), 0)
    lane_full = lax.broadcasted_iota(jnp.int32, (MOBA_BLOCK, LANES), 1)
    qt = qt_ref[0] * (ATT_HEAD_DIM ** -0.5)
    keep = keep_ref[0, 0]
    pad = jnp.zeros((LANES - keep.shape[0], MOBA_BLOCK), F32)
    q_aug = [jnp.concatenate([jnp.where(_in_head(feat, hh), qt, 0.0), keep, pad], axis=0).astype(BF16)
             for hh in range(HEADS_PER_TILE)]

    def scores(j, m):
        r0 = pl.multiple_of(j * MOBA_BLOCK, MOBA_BLOCK)
        k2 = k_ref[0, pl.ds(r0, MOBA_BLOCK), :]
        d = jnp.minimum(i - j, n_bias - 1)
        m_new = []
        for hh in range(HEADS_PER_TILE):
            onehot = jnp.where(lane_full == hh * n_blocks + j, 1.0, 0.0)
            k_aug = jnp.concatenate([k2, onehot], axis=1).astype(BF16)
            s = jnp.dot(k_aug, q_aug[hh], preferred_element_type=F32) + bias_ref[hh, d]
            s_scr[hh, j] = s
            m_new.append(jnp.maximum(m[hh], jnp.max(s, axis=0, keepdims=True)))
        return tuple(m_new)

    m = lax.fori_loop(0, i + 1, scores,
                      tuple(jnp.full((1, MOBA_BLOCK), M_INIT, F32) for _ in range(HEADS_PER_TILE)))
    for hh in range(HEADS_PER_TILE):
        acc_scr[hh] = jnp.zeros((LANES, MOBA_BLOCK), F32)

    def values(j, l):
        r0 = pl.multiple_of(j * MOBA_BLOCK, MOBA_BLOCK)
        vt = vt_ref[0, :, pl.ds(r0, MOBA_BLOCK)].astype(BF16)
        l_new = []
        for hh in range(HEADS_PER_TILE):
            p = jnp.exp(s_scr[hh, j] - m[hh])
            l_new.append(l[hh] + jnp.sum(p, axis=0, keepdims=True))
            acc_scr[hh] += jnp.dot(vt, p.astype(BF16), preferred_element_type=F32)
        return tuple(l_new)

    l = lax.fori_loop(0, i + 1, values,
                      tuple(jnp.zeros((1, MOBA_BLOCK), F32) for _ in range(HEADS_PER_TILE)))
    out = jnp.zeros((LANES, MOBA_BLOCK), F32)
    for hh in range(HEADS_PER_TILE):
        out = jnp.where(_in_head(feat, hh), acc_scr[hh] / l[hh], out)
    o_ref[0] = out.T


def _moba_prompt(aqt, ak, avt, keep, bias):
    b, d_att, t = aqt.shape
    tiles = d_att // LANES
    n_blocks = t // MOBA_BLOCK
    n_bias = bias.shape[1]
    return pl.pallas_call(
        functools.partial(_moba_prompt_body, n_bias=n_bias, n_blocks=n_blocks),
        grid=(b, tiles, n_blocks),
        in_specs=[pl.BlockSpec((1, LANES, MOBA_BLOCK), lambda ib, p, i: (ib, p, i)),
                  pl.BlockSpec((1, t, LANES), lambda ib, p, i: (ib, 0, p)),
                  pl.BlockSpec((1, LANES, t), lambda ib, p, i: (ib, p, 0)),
                  pl.BlockSpec((1, 1, keep.shape[2], MOBA_BLOCK), lambda ib, p, i: (ib, p, 0, i)),
                  pl.BlockSpec((HEADS_PER_TILE, n_bias, MOBA_BLOCK, MOBA_BLOCK), lambda ib, p, i: (p, 0, 0, 0))],
        out_specs=pl.BlockSpec((1, MOBA_BLOCK, LANES), lambda ib, p, i: (ib, i, p)),
        out_shape=jax.ShapeDtypeStruct((b, t, d_att), F32),
        scratch_shapes=[pltpu.VMEM((HEADS_PER_TILE, n_blocks, MOBA_BLOCK, MOBA_BLOCK), F32),
                        pltpu.VMEM((HEADS_PER_TILE, LANES, MOBA_BLOCK), F32)],
        compiler_params=pltpu.CompilerParams(
            dimension_semantics=("arbitrary", "arbitrary", "arbitrary"), vmem_limit_bytes=VMEM_LIMIT),
        name="moba_prompt",
    )(aqt, ak, avt, keep, bias)


def _moba_sample_body(pt_ref, q_ref, kn_ref, vn_ref, bias_ref, *rest, n_pages, heads, rows):
    k_pages = rest[:n_pages]
    v_pages = rest[n_pages:2 * n_pages]
    o_ref = rest[2 * n_pages]
    s_scr = rest[2 * n_pages + 1]
    d_att = heads * ATT_HEAD_DIM
    page = k_pages[0].shape[-1]
    pages_per_block = MOBA_BLOCK // page
    n_blocks = n_pages // pages_per_block
    n_q = heads * rows
    feat = lax.broadcasted_iota(jnp.int32, (1, d_att), 1)
    lane = lax.broadcasted_iota(jnp.int32, (n_q, LANES), 1)

    q = q_ref[0] * (ATT_HEAD_DIM ** -0.5)
    q_bd = jnp.concatenate([jnp.where(_in_head(feat, h), q, 0.0) for h in range(heads)], axis=0).astype(BF16)

    gate = jnp.zeros((n_q, LANES), F32)
    for n in range(n_blocks):
        tot = jnp.zeros((n_q, 1), F32)
        for p in range(n * pages_per_block, (n + 1) * pages_per_block):
            kt = k_pages[p][...].reshape(d_att, page).astype(BF16)
            s = jnp.dot(q_bd, kt, preferred_element_type=F32)
            s_scr[:, p * page:(p + 1) * page] = s
            tot = tot + jnp.sum(s, axis=-1, keepdims=True)
        gate = jnp.where(lane == n, tot * (1.0 / MOBA_BLOCK), gate)
    keep = _topk_keep(gate, lane, jnp.full(gate.shape, n_blocks, jnp.int32), axis=1, n=n_blocks)

    def new_rows(ref):
        return jnp.concatenate([ref[0], jnp.zeros((LANES - rows, d_att), F32)], axis=0).astype(BF16)

    own = n_pages * page
    s_own = lax.dot_general(q_bd, new_rows(kn_ref), NT_DIMS, preferred_element_type=F32)
    s_own = s_own + bias_ref[:, own:own + LANES]
    s_scr[:, own:own + LANES] = s_own
    m = jnp.max(s_own, axis=-1, keepdims=True)
    for p in range(n_pages):
        n = p // pages_per_block
        s = s_scr[:, p * page:(p + 1) * page] + bias_ref[:, p * page:(p + 1) * page] + keep[:, n:n + 1]
        s_scr[:, p * page:(p + 1) * page] = s
        m = jnp.maximum(m, jnp.max(s, axis=-1, keepdims=True))

    pr = jnp.exp(s_scr[:, own:own + LANES] - m)
    l = jnp.sum(pr, axis=-1, keepdims=True)
    acc = jnp.dot(pr.astype(BF16), new_rows(vn_ref), preferred_element_type=F32)
    for p in range(n_pages):
        pr = jnp.exp(s_scr[:, p * page:(p + 1) * page] - m)
        l = l + jnp.sum(pr, axis=-1, keepdims=True)
        vt = v_pages[p][...].reshape(d_att, page).astype(BF16)
        acc = acc + lax.dot_general(pr.astype(BF16), vt, NT_DIMS, preferred_element_type=F32)
    acc = acc / l
    out = jnp.zeros((rows, d_att), F32)
    for h in range(heads):
        out = jnp.where(_in_head(feat, h), acc[h * rows:(h + 1) * rows, :], out)
    o_ref[0] = out


def _moba_sample(aq, ak, av, cache_kt, cache_vt, page_table, bias, *, heads):
    nb, rows, d_att = aq.shape
    n_pages = page_table.shape[1]
    page = cache_kt.shape[-1]
    assert cache_kt.shape[2:] == (heads, ATT_HEAD_DIM, page) and MOBA_BLOCK % page == 0 and page == LANES
    tok = pl.BlockSpec((1, rows, d_att), lambda b, pt: (b, 0, 0))

    def page_spec(p):
        return pl.BlockSpec((None, None, heads, ATT_HEAD_DIM, page), lambda b, pt, p=p: (0, pt[b, p], 0, 0, 0))

    grid_spec = pltpu.PrefetchScalarGridSpec(
        num_scalar_prefetch=1,
        grid=(nb,),
        in_specs=[tok, tok, tok, pl.BlockSpec(bias.shape, lambda b, pt: (0, 0))]
                 + [page_spec(p) for p in range(n_pages)] * 2,
        out_specs=tok,
        scratch_shapes=[pltpu.VMEM(bias.shape, F32)],
    )
    return pl.pallas_call(
        functools.partial(_moba_sample_body, n_pages=n_pages, heads=heads, rows=rows),
        grid_spec=grid_spec,
        out_shape=jax.ShapeDtypeStruct((nb, rows, d_att), F32),
        compiler_params=pltpu.CompilerParams(
            dimension_semantics=("arbitrary",), vmem_limit_bytes=VMEM_LIMIT),
        name="moba_sample",
    )(page_table, aq, ak, av, bias, *([cache_kt] * n_pages), *([cache_vt] * n_pages))


def _out_proj_body(ol_ref, oa_ref, ga_ref, x_ref, w_ref, gpost_ref, y_ref, *, d_lin):
    o = jnp.dot(ol_ref[...].astype(BF16), w_ref[0:d_lin, :], preferred_element_type=F32)
    o = o + jnp.dot((oa_ref[...] * ga_ref[...]).astype(BF16), w_ref[d_lin:, :],
                    preferred_element_type=F32)
    y_ref[...] = x_ref[...] + _rms(o, gpost_ref[...])


def _out_proj(o_lin, o_att, ga, x, w_bf, gpost, *, tm=256):
    rows, d_model = x.shape
    tm = min(tm, rows)
    assert rows % tm == 0
    d_lin = o_lin.shape[1]
    d_att = o_att.shape[1]
    return pl.pallas_call(
        functools.partial(_out_proj_body, d_lin=d_lin),
        grid=(rows // tm,),
        in_specs=[
            pl.BlockSpec((tm, d_lin), lambda i: (i, 0)),
            pl.BlockSpec((tm, d_att), lambda i: (i, 0)),
            pl.BlockSpec((tm, d_att), lambda i: (i, 0)),
            pl.BlockSpec((tm, d_model), lambda i: (i, 0)),
            pl.BlockSpec(w_bf.shape, lambda i: (0, 0)),
            pl.BlockSpec((1, d_model), lambda i: (0, 0)),
        ],
        out_specs=pl.BlockSpec((tm, d_model), lambda i: (i, 0)),
        out_shape=jax.ShapeDtypeStruct((rows, d_model), F32),
        compiler_params=pltpu.CompilerParams(dimension_semantics=("arbitrary",)),
        name="out_proj",
    )(o_lin, o_att, ga, x, w_bf, gpost)


def kernel(x_prompt, x_sample, cache_k, cache_v, state_hgrn, page_table, w_in, w_out,
           norm_pre, norm_post, norm_lin_out, lin_lower_bound, rel_bias):
    depth = w_in.shape[0]
    assert depth == 1 and lin_lower_bound.shape[0] == depth + 1
    b, t, d_model = x_prompt.shape
    nb, ts, _ = x_sample.shape
    d_lin = lin_lower_bound.shape[1]
    d_att = w_out.shape[1] - d_lin
    lin_heads = d_lin // LIN_HEAD_DIM
    att_heads = rel_bias.shape[1]
    n_pages = page_table.shape[1]
    page = cache_k.shape[2]
    past_len = n_pages * page
    assert d_att == att_heads * ATT_HEAD_DIM and t % MOBA_BLOCK == 0
    assert past_len % MOBA_BLOCK == 0 and ts <= SUBLANES

    w_in_bf = w_in[0].astype(BF16)
    w_out_bf = w_out[0].astype(BF16)
    gpre, gpost, glin = norm_pre, norm_post, norm_lin_out
    proj = functools.partial(_in_proj, gpre=gpre, w_bf=w_in_bf, llb=lin_lower_bound,
                             d_lin=d_lin, d_att=d_att)

    xp = x_prompt.reshape(b * t, d_model)
    q, g, kk, v, gl, ga, ak, aqt, akt, avt = proj(xp, seq_len=t)
    q, g, kk, v, gl, ak = [a.reshape(b, t, -1) for a in (q, g, kk, v, gl, ak)]
    s0 = jnp.zeros((b, lin_heads, LIN_HEAD_DIM, LIN_HEAD_DIM), F32)
    o_lin, s_prompt = _hgrn_tiles(q, g, kk, v, gl, s0, glin, chunk=32, t_tile=512)
    keep = _moba_gate(aqt, ak)
    n_bias = 6
    assert (n_bias - 1) * MOBA_BLOCK - (MOBA_BLOCK - 1) >= MAX_DISTANCE
    bias = _prompt_bias(rel_bias, n_bias)
    o_att = _moba_prompt(aqt, ak, avt, keep, bias)
    y_prompt = _out_proj(o_lin.reshape(b * t, d_lin), o_att.reshape(b * t, d_att),
                         ga, xp, w_out_bf, gpost).reshape(b, t, d_model)

    rows = SUBLANES
    xs = jnp.pad(x_sample, ((0, 0), (0, rows - ts), (0, 0))).reshape(nb * rows, d_model)
    qs, gs, kks, vs, gls, gas, aqs, aks, avs = [a.reshape(nb, rows, -1) for a in proj(xs)]
    o_lin_s, s_sample = _hgrn(qs, gs, kks, vs, gls, state_hgrn[0], glin,
                              chunk=rows, valid=ts, t_tile=rows, n_seq=8)
    bias_s = _sample_bias(rel_bias, past_len=past_len, n_valid=ts, rows=rows)
    cache_kt = cache_k.transpose(0, 1, 3, 4, 2)
    cache_vt = cache_v.transpose(0, 1, 3, 4, 2)
    o_att_s = _moba_sample(aqs, aks, avs, cache_kt, cache_vt, page_table, bias_s, heads=att_heads)
    y_s = _out_proj(o_lin_s.reshape(nb * rows, d_lin), o_att_s.reshape(nb * rows, d_att),
                    gas.reshape(nb * rows, d_att), xs, w_out_bf, gpost)
    y_sample = y_s.reshape(nb, rows, d_model)[:, :ts]

    def prompt_kv(a):
        return a.reshape(1, b, att_heads, ATT_HEAD_DIM, t).transpose(0, 1, 4, 2, 3)

    kvs_shape = (1, nb, ts, att_heads, ATT_HEAD_DIM)
    return (y_prompt, y_sample, prompt_kv(akt), prompt_kv(avt), s_prompt[None],
            aks[:, :ts].reshape(kvs_shape), avs[:, :ts].reshape(kvs_shape), s_sample[None])
```

```python
import functools
import math

import jax
import jax.numpy as jnp
from jax import lax
from jax.experimental import pallas as pl
from jax.experimental.pallas import tpu as pltpu

F32 = jnp.float32
BF16 = jnp.bfloat16

EPS = 1e-6
LIN_HEAD_DIM = 128
ATT_HEAD_DIM = 64
MOBA_BLOCK = 256
MOBA_TOPK = 3
N_BUCKETS = 32
MAX_DISTANCE = 1024
NEG = -2e30
M_INIT = -1e30
LOG2E = 1.4426950408889634
LANES = 128
SUBLANES = 8
VMEM_LIMIT = 56 * 1024 * 1024
HEADS_PER_TILE = LANES // ATT_HEAD_DIM

NT_DIMS = (((1,), (1,)), ((), ()))
TN_DIMS = (((0,), (0,)), ((), ()))


def _sigmoid(x):
    return 1.0 / (1.0 + jnp.exp(-x))


def _silu(x):
    return x * _sigmoid(x)


def _rms(x, g):
    return x * lax.rsqrt(jnp.mean(x * x, axis=-1, keepdims=True) + EPS) * g


def _in_head(idx, head):
    return (idx >= head * ATT_HEAD_DIM) & (idx < (head + 1) * ATT_HEAD_DIM)


def _in_proj_body(x_ref, gpre_ref, w_ref, llb_ref, *rest, d_lin, d_att, feat_major):
    if feat_major:
        wt_ref, q_o, g_o, kk_o, v_o, gl_o, ga_o, ak_o, aqt_o, akt_o, avt_o = rest
    else:
        q_o, g_o, kk_o, v_o, gl_o, ga_o, aq_o, ak_o, av_o = rest
    h = _rms(x_ref[...], gpre_ref[...]).astype(BF16)

    def proj(col, width):
        return jnp.dot(h, w_ref[:, col:col + width], preferred_element_type=F32)

    def proj_t(row, width):
        return lax.dot_general(wt_ref[row:row + width, :], h, NT_DIMS, preferred_element_type=F32)

    llb = llb_ref[...]
    e = jnp.exp(llb - jnp.max(llb, axis=0, keepdims=True))
    lb = e[0:1, :] / jnp.sum(e, axis=0, keepdims=True)

    q_o[...] = _silu(proj(0, d_lin))
    f = lb + (1.0 - lb) * _sigmoid(proj(d_lin, d_lin))
    g_o[...] = jnp.log(f)
    kk_o[...] = 1.0 - f
    v_o[...] = proj(2 * d_lin, d_lin)
    gl_o[...] = _silu(proj(3 * d_lin, d_lin))
    a0 = 4 * d_lin
    ga_o[...] = _silu(proj(a0 + 3 * d_att, d_att))
    if feat_major:
        aqt_o[0] = proj_t(0, d_att)
        akt = proj_t(d_att, d_att)
        akt_o[0] = akt
        ak_o[...] = akt.T
        avt_o[0] = proj_t(2 * d_att, d_att)
    else:
        aq_o[...] = proj(a0, d_att)
        ak_o[...] = proj(a0 + d_att, d_att)
        av_o[...] = proj(a0 + 2 * d_att, d_att)


def _in_proj(x, gpre, w_bf, llb, *, d_lin, d_att, seq_len=None, tm=256):
    rows, d_model = x.shape
    tm = min(tm, rows)
    assert rows % tm == 0
    d_in = w_bf.shape[1]
    feat_major = seq_len is not None
    row_spec = lambda w: pl.BlockSpec((tm, w), lambda i: (i, 0))
    in_specs = [
        pl.BlockSpec((tm, d_model), lambda i: (i, 0)),
        pl.BlockSpec((1, d_model), lambda i: (0, 0)),
        pl.BlockSpec((d_model, d_in), lambda i: (0, 0)),
        pl.BlockSpec(llb.shape, lambda i: (0, 0)),
    ]
    args = [x, gpre, w_bf, llb]
    out_specs = [row_spec(d_lin)] * 5 + [row_spec(d_att)]
    out_shape = [jax.ShapeDtypeStruct((rows, d_lin), F32)] * 5 + [jax.ShapeDtypeStruct((rows, d_att), F32)]
    if feat_major:
        assert seq_len % tm == 0
        tiles = seq_len // tm
        a0 = 4 * d_lin
        wt_bf = w_bf[:, a0:a0 + 3 * d_att].T
        in_specs.append(pl.BlockSpec(wt_bf.shape, lambda i: (0, 0)))
        args.append(wt_bf)
        t_spec = pl.BlockSpec((1, d_att, tm), lambda i: (i // tiles, 0, i % tiles))
        t_shape = jax.ShapeDtypeStruct((rows // seq_len, d_att, seq_len), F32)
        out_specs += [row_spec(d_att)] + [t_spec] * 3
        out_shape += [jax.ShapeDtypeStruct((rows, d_att), F32)] + [t_shape] * 3
    else:
        out_specs += [row_spec(d_att)] * 3
        out_shape += [jax.ShapeDtypeStruct((rows, d_att), F32)] * 3
    return pl.pallas_call(
        functools.partial(_in_proj_body, d_lin=d_lin, d_att=d_att, feat_major=feat_major),
        grid=(rows // tm,),
        in_specs=in_specs,
        out_specs=out_specs,
        out_shape=out_shape,
        compiler_params=pltpu.CompilerParams(
            dimension_semantics=("arbitrary",), vmem_limit_bytes=VMEM_LIMIT),
        name="in_proj",
    )(*args)


def _hgrn_body(q_ref, g_ref, kk_ref, v_ref, gl_ref, s0_ref, glin_ref,
               o_ref, sout_ref, st_scr, b_scr, *, chunk, valid, n_chunks, n_seq):
    it = pl.program_id(2)

    @pl.when(it == 0)
    def _():
        for s in range(n_seq):
            st_scr[s] = s0_ref[s, 0].T

    row = lax.broadcasted_iota(jnp.int32, (chunk, 1), 0)

    for s in range(n_seq):
        def chunk_step(ci, carry, s=s):
            r0 = pl.multiple_of(ci * chunk, chunk)
            acc = jnp.zeros((1, LIN_HEAD_DIM), F32)
            for r in range(chunk):
                if r < valid:
                    acc = acc + g_ref[s, pl.ds(r0 + r, 1), :]
                b_scr[pl.ds(r, 1), :] = acc
            b = b_scr[...]
            b_last = b_scr[pl.ds(chunk - 1, 1), :]
            q = q_ref[s, pl.ds(r0, chunk), :]
            kk = kk_ref[s, pl.ds(r0, chunk), :]
            v = v_ref[s, pl.ds(r0, chunk), :]
            if valid < chunk:
                kk = jnp.where(row < valid, kk, 0.0)
            st = st_scr[s]
            o = lax.dot_general((q * jnp.exp(b)).astype(BF16), st.astype(BF16), NT_DIMS,
                                preferred_element_type=F32)
            for j in range(valid):
                bj = b_scr[pl.ds(j, 1), :]
                kj = kk_ref[s, pl.ds(r0 + j, 1), :]
                vj = v_ref[s, pl.ds(r0 + j, 1), :]
                w = q * kj * jnp.exp(jnp.minimum(b - bj, 0.0))
                a = jnp.sum(w, axis=-1, keepdims=True)
                o = o + jnp.where(row >= j, a, 0.0) * vj
            kt = kk * jnp.exp(b_last - b)
            st_scr[s] = st * jnp.exp(b_last) + lax.dot_general(
                v.astype(BF16), kt.astype(BF16), TN_DIMS, preferred_element_type=F32)
            o_ref[s, pl.ds(r0, chunk), :] = _rms(o, glin_ref[...]) * gl_ref[s, pl.ds(r0, chunk), :]
            return carry

        lax.fori_loop(0, n_chunks, chunk_step, 0)

    @pl.when(it == pl.num_programs(2) - 1)
    def _():
        for s in range(n_seq):
            sout_ref[s, 0] = st_scr[s].T


def _hgrn(q, g, kk, v, gl, s0, glin, *, chunk, valid, t_tile, n_seq):
    nb, t, d_lin = q.shape
    heads = d_lin // LIN_HEAD_DIM
    seq_spec = pl.BlockSpec((n_seq, t_tile, LIN_HEAD_DIM), lambda b, h, i: (b, i, h))
    st_spec = pl.BlockSpec((n_seq, 1, LIN_HEAD_DIM, LIN_HEAD_DIM), lambda b, h, i: (b, h, 0, 0))
    return pl.pallas_call(
        functools.partial(_hgrn_body, chunk=chunk, valid=valid,
                          n_chunks=t_tile // chunk, n_seq=n_seq),
        grid=(nb // n_seq, heads, t // t_tile),
        in_specs=[seq_spec] * 5 + [st_spec, pl.BlockSpec((1, LIN_HEAD_DIM), lambda b, h, i: (0, 0))],
        out_specs=[seq_spec, st_spec],
        out_shape=[jax.ShapeDtypeStruct((nb, t, d_lin), F32),
                   jax.ShapeDtypeStruct(s0.shape, F32)],
        scratch_shapes=[pltpu.VMEM((n_seq, LIN_HEAD_DIM, LIN_HEAD_DIM), F32),
                        pltpu.VMEM((chunk, LIN_HEAD_DIM), F32)],
        compiler_params=pltpu.CompilerParams(
            dimension_semantics=("arbitrary", "arbitrary", "arbitrary")),
        name="hgrn",
    )(q, g, kk, v, gl, s0, glin)


SAFE_DECAY = 60.0


def _hgrn_tile_body(q_ref, g_ref, kk_ref, v_ref, gl_ref, s0_ref, glin_ref, o_ref, sout_ref,
                    st_scr, b_scr, oi_scr, u_scr, kkp_scr, bp_scr, vp_scr, *, chunk, t_tile):
    it = pl.program_id(2)

    @pl.when(it == 0)
    def _():
        st_scr[...] = s0_ref[0, 0].T

    lg = int(math.log2(chunk))
    n_groups = t_tile // LANES
    r = lax.broadcasted_iota(jnp.int32, (LANES, LANES), 0)
    c = lax.broadcasted_iota(jnp.int32, (LANES, LANES), 1)
    causal = ((r >> lg) == (c >> lg)) & (c <= r)
    tri = jnp.where(causal, 1.0, 0.0).astype(BF16)
    for gi in range(n_groups):
        rows = slice(gi * LANES, (gi + 1) * LANES)
        rest = g_ref[0, rows, :]
        b = jnp.zeros((LANES, LIN_HEAD_DIM), F32)
        for _ in range(3):
            piece = rest.astype(BF16)
            rest = rest - piece.astype(F32)
            b = b + jnp.dot(tri, piece, preferred_element_type=F32)
        b_scr[rows, :] = b

    def intra_mxu():
        for gi in range(n_groups):
            rows = slice(gi * LANES, (gi + 1) * LANES)
            b = b_scr[rows, :]
            qe = (q_ref[0, rows, :] * jnp.exp(b)).astype(BF16)
            ke = (kk_ref[0, rows, :] * jnp.exp(-b)).astype(BF16)
            att = lax.dot_general(qe, ke, NT_DIMS, preferred_element_type=F32)
            att = jnp.where(causal, att, 0.0).astype(BF16)
            oi_scr[rows, :] = jnp.dot(att, v_ref[0, rows, :].astype(BF16), preferred_element_type=F32)

    def intra_pairs():
        zeros = jnp.zeros((chunk, LIN_HEAD_DIM), F32)
        for dst, src in ((kkp_scr, kk_ref[0]), (bp_scr, b_scr[...]), (vp_scr, v_ref[0])):
            dst[0:chunk, :] = zeros
            dst[chunk:, :] = src
        oi_scr[...] = jnp.zeros(oi_scr.shape, F32)
        q = q_ref[0]
        b = b_scr[...]
        tmod = lax.broadcasted_iota(jnp.int32, (t_tile, 1), 0) & (chunk - 1)

        def lag(d, carry):
            start = chunk - d
            w = q * kkp_scr[pl.ds(start, t_tile), :] * jnp.exp(
                jnp.minimum(b - bp_scr[pl.ds(start, t_tile), :], 0.0))
            a = jnp.sum(w, axis=-1, keepdims=True)
            oi_scr[...] += jnp.where(tmod >= d, a, 0.0) * vp_scr[pl.ds(start, t_tile), :]
            return carry

        lax.fori_loop(0, chunk, lag, 0)

    lax.cond(jnp.min(b_scr[...]) >= -SAFE_DECAY, intra_mxu, intra_pairs)

    n_chunks = t_tile // chunk
    for n in range(n_chunks):
        rows = slice(n * chunk, (n + 1) * chunk)
        b_last = b_scr[(n + 1) * chunk - 1:(n + 1) * chunk, :]
        kt = (kk_ref[0, rows, :] * jnp.exp(b_last - b_scr[rows, :])).astype(BF16)
        u_scr[n] = lax.dot_general(v_ref[0, rows, :].astype(BF16), kt, TN_DIMS, preferred_element_type=F32)

    st = st_scr[...]
    for n in range(n_chunks):
        rows = slice(n * chunk, (n + 1) * chunk)
        b_last = b_scr[(n + 1) * chunk - 1:(n + 1) * chunk, :]
        o = lax.dot_general((q_ref[0, rows, :] * jnp.exp(b_scr[rows, :])).astype(BF16), st.astype(BF16),
                            NT_DIMS, preferred_element_type=F32) + oi_scr[rows, :]
        st = st * jnp.exp(b_last) + u_scr[n]
        o_ref[0, rows, :] = _rms(o, glin_ref[...]) * gl_ref[0, rows, :]
    st_scr[...] = st

    @pl.when(it == pl.num_programs(2) - 1)
    def _():
        sout_ref[0, 0] = st.T


def _hgrn_tiles(q, g, kk, v, gl, s0, glin, *, chunk, t_tile):
    nb, t, d_lin = q.shape
    heads = d_lin // LIN_HEAD_DIM
    assert LANES % chunk == 0 and t_tile % LANES == 0 and t % t_tile == 0
    seq_spec = pl.BlockSpec((1, t_tile, LIN_HEAD_DIM), lambda b, h, i: (b, i, h))
    st_spec = pl.BlockSpec((1, 1, LIN_HEAD_DIM, LIN_HEAD_DIM), lambda b, h, i: (b, h, 0, 0))
    tile = pltpu.VMEM((t_tile, LIN_HEAD_DIM), F32)
    halo = pltpu.VMEM((chunk + t_tile, LIN_HEAD_DIM), F32)
    return pl.pallas_call(
        functools.partial(_hgrn_tile_body, chunk=chunk, t_tile=t_tile),
        grid=(nb, heads, t // t_tile),
        in_specs=[seq_spec] * 5 + [st_spec, pl.BlockSpec((1, LIN_HEAD_DIM), lambda b, h, i: (0, 0))],
        out_specs=[seq_spec, st_spec],
        out_shape=[jax.ShapeDtypeStruct((nb, t, d_lin), F32),
                   jax.ShapeDtypeStruct(s0.shape, F32)],
        scratch_shapes=[pltpu.VMEM((LIN_HEAD_DIM, LIN_HEAD_DIM), F32), tile, tile,
                        pltpu.VMEM((t_tile // chunk, LIN_HEAD_DIM, LIN_HEAD_DIM), F32), halo, halo, halo],
        compiler_params=pltpu.CompilerParams(
            dimension_semantics=("arbitrary", "arbitrary", "arbitrary")),
        name="hgrn_tiles",
    )(q, g, kk, v, gl, s0, glin)


def _t5_bias(dist, rb_ref, head):
    max_exact = N_BUCKETS // 2
    n = jnp.maximum(dist, 0)
    large = max_exact + (jnp.log(jnp.maximum(n, 1).astype(F32) / max_exact)
                         / math.log(MAX_DISTANCE / max_exact) * (N_BUCKETS - max_exact)).astype(jnp.int32)
    bucket = jnp.where(n < max_exact, n, jnp.minimum(large, N_BUCKETS - 1))
    out = jnp.zeros(dist.shape, F32)
    for bk in range(N_BUCKETS):
        out = jnp.where(bucket == bk, rb_ref[bk, head], out)
    return out


def _prompt_bias_body(rb_ref, o_ref):
    h = pl.program_id(0)
    d = pl.program_id(1)
    key = lax.broadcasted_iota(jnp.int32, (MOBA_BLOCK, MOBA_BLOCK), 0)
    qry = lax.broadcasted_iota(jnp.int32, (MOBA_BLOCK, MOBA_BLOCK), 1)
    dist = d * MOBA_BLOCK + qry - key
    o_ref[0, 0] = jnp.where(dist >= 0, _t5_bias(dist, rb_ref, h) * LOG2E, NEG)


def _prompt_bias(rel_bias, n_tiles):
    heads = rel_bias.shape[1]
    return pl.pallas_call(
        _prompt_bias_body,
        grid=(heads, n_tiles),
        in_specs=[pl.BlockSpec(memory_space=pltpu.SMEM)],
        out_specs=pl.BlockSpec((1, 1, MOBA_BLOCK, MOBA_BLOCK), lambda h, d: (h, d, 0, 0)),
        out_shape=jax.ShapeDtypeStruct((heads, n_tiles, MOBA_BLOCK, MOBA_BLOCK), F32),
        name="prompt_bias",
    )(rel_bias)


def _sample_bias_body(rb_ref, o_ref, *, past_len, n_valid, rows):
    h = pl.program_id(0)
    width = past_len + LANES
    r = lax.broadcasted_iota(jnp.int32, (rows, width), 0)
    kpos = lax.broadcasted_iota(jnp.int32, (rows, width), 1)
    dist = past_len + r - kpos
    ok = (dist >= 0) & (kpos < past_len + n_valid)
    o_ref[...] = jnp.where(ok, _t5_bias(dist, rb_ref, h), NEG)


def _sample_bias(rel_bias, *, past_len, n_valid, rows):
    heads = rel_bias.shape[1]
    width = past_len + LANES
    return pl.pallas_call(
        functools.partial(_sample_bias_body, past_len=past_len, n_valid=n_valid, rows=rows),
        grid=(heads,),
        in_specs=[pl.BlockSpec(memory_space=pltpu.SMEM)],
        out_specs=pl.BlockSpec((rows, width), lambda h: (h, 0)),
        out_shape=jax.ShapeDtypeStruct((heads * rows, width), F32),
        name="sample_bias",
    )(rel_bias)


def _topk_keep(gate, idx, own, axis, n):
    past = idx < own
    gm = jnp.where(past, gate, -jnp.inf)
    rank = jnp.zeros(gate.shape, jnp.int32)
    for jp in range(n):
        gj = lax.slice_in_dim(gm, jp, jp + 1, axis=axis)
        beats = (gj > gm) | ((gj == gm) & (jp < idx))
        rank = rank + beats.astype(jnp.int32)
    keep = ((rank < MOBA_TOPK) & past) | (idx == own)
    return jnp.where(keep, 0.0, NEG)


def _moba_gate_body(qt_ref, k_ref, a_ref, km_scr, *, n_blocks):
    t = n_blocks * MOBA_BLOCK
    qt = qt_ref[0]
    lane = lax.broadcasted_iota(jnp.int32, (1, LANES), 1)
    for n in range(n_blocks):
        km_scr[pl.ds(n, 1), :] = jnp.sum(
            k_ref[0, n * MOBA_BLOCK:(n + 1) * MOBA_BLOCK, :], axis=0, keepdims=True) * (1.0 / MOBA_BLOCK)
    km = km_scr[...]
    blk = lax.broadcasted_iota(jnp.int32, (n_blocks, t), 0)
    own = lax.broadcasted_iota(jnp.int32, (n_blocks, t), 1) >> int(math.log2(MOBA_BLOCK))
    for hh in range(HEADS_PER_TILE):
        kmh = jnp.where(_in_head(lane, hh), km, 0.0)
        gate_t = jnp.dot(kmh, qt, precision=lax.Precision.HIGHEST, preferred_element_type=F32)
        a_ref[0, 0, hh * n_blocks:(hh + 1) * n_blocks, :] = _topk_keep(gate_t, blk, own, axis=0, n=n_blocks)


def _moba_gate(aqt, ak):
    b, d_att, t = aqt.shape
    tiles = d_att // LANES
    n_blocks = t // MOBA_BLOCK
    return pl.pallas_call(
        functools.partial(_moba_gate_body, n_blocks=n_blocks),
        grid=(b, tiles),
        in_specs=[pl.BlockSpec((1, LANES, t), lambda i, p: (i, p, 0)),
                  pl.BlockSpec((1, t, LANES), lambda i, p: (i, 0, p))],
        out_specs=pl.BlockSpec((1, 1, HEADS_PER_TILE * n_blocks, t), lambda i, p: (i, p, 0, 0)),
        out_shape=jax.ShapeDtypeStruct((b, tiles, HEADS_PER_TILE * n_blocks, t), F32),
        scratch_shapes=[pltpu.VMEM((n_blocks, LANES), F32)],
        compiler_params=pltpu.CompilerParams(
            dimension_semantics=("arbitrary", "arbitrary"), vmem_limit_bytes=VMEM_LIMIT),
        name="moba_gate",
    )(aqt, ak)


def _moba_prompt_body(qt_ref, k_ref, vt_ref, keep_ref, bias_ref, o_ref, kaug_scr, vtaug_scr, s_scr,
                      *, n_bias, n_blocks, group):
    i = pl.program_id(2)
    rows = group * MOBA_BLOCK
    t = n_blocks * MOBA_BLOCK
    lg_block = int(math.log2(MOBA_BLOCK))
    feat = lax.broadcasted_iota(jnp.int32, (LANES, 1), 0)
    ones_row = [((hh + 1) % HEADS_PER_TILE) * ATT_HEAD_DIM for hh in range(HEADS_PER_TILE)]

    @pl.when(i == 0)
    def _():
        blk = lax.broadcasted_iota(jnp.int32, (rows, LANES), 0) >> lg_block
        lane = lax.broadcasted_iota(jnp.int32, (rows, LANES), 1)
        for c in range(t // rows):
            sl = slice(c * rows, (c + 1) * rows)
            for hh in range(HEADS_PER_TILE):
                onehot = jnp.where(lane == hh * n_blocks + c * group + blk, 1.0, 0.0)
                kaug_scr[hh, sl, :] = jnp.concatenate([k_ref[0, sl, :], onehot], axis=1).astype(BF16)
                vtaug_scr[hh, :, sl] = jnp.where(feat == ones_row[hh], 1.0, vt_ref[0, :, sl]).astype(BF16)

    qt = qt_ref[0] * (ATT_HEAD_DIM ** -0.5 * LOG2E)
    keep = keep_ref[0, 0]
    pad = jnp.zeros((LANES - keep.shape[0], MOBA_BLOCK), F32)
    q_aug = [jnp.concatenate([jnp.where(_in_head(feat, hh), qt, 0.0), keep, pad], axis=0).astype(BF16)
             for hh in range(HEADS_PER_TILE)]
    n_groups = (i >> int(math.log2(group))) + 1

    def scores(gi, m):
        r0 = pl.multiple_of(gi * rows, rows)
        m_new = []
        for hh in range(HEADS_PER_TILE):
            s = jnp.dot(kaug_scr[hh, pl.ds(r0, rows), :], q_aug[hh], preferred_element_type=F32)
            mh = m[hh]
            for u in range(group):
                d = jnp.clip(i - (gi * group + u), 0, n_bias - 1)
                su = s[u * MOBA_BLOCK:(u + 1) * MOBA_BLOCK] + bias_ref[hh, d]
                s_scr[hh, pl.ds(r0 + u * MOBA_BLOCK, MOBA_BLOCK), :] = su
                mh = jnp.maximum(mh, jnp.max(su, axis=0, keepdims=True))
            m_new.append(mh)
        return tuple(m_new)

    m = lax.fori_loop(0, n_groups, scores,
                      tuple(jnp.full((1, MOBA_BLOCK), M_INIT, F32) for _ in range(HEADS_PER_TILE)))

    def values(gi, acc):
        r0 = pl.multiple_of(gi * rows, rows)
        out = []
        for hh in range(HEADS_PER_TILE):
            p = jnp.exp2(s_scr[hh, pl.ds(r0, rows), :] - m[hh]).astype(BF16)
            out.append(acc[hh] + jnp.dot(vtaug_scr[hh, :, pl.ds(r0, rows)], p, preferred_element_type=F32))
        return tuple(out)

    acc = lax.fori_loop(0, n_groups, values,
                        tuple(jnp.zeros((LANES, MOBA_BLOCK), F32) for _ in range(HEADS_PER_TILE)))
    out = jnp.zeros((LANES, MOBA_BLOCK), F32)
    for hh in range(HEADS_PER_TILE):
        out = jnp.where(_in_head(feat, hh), acc[hh] / acc[hh][ones_row[hh]:ones_row[hh] + 1, :], out)
    o_ref[0] = out.T


def _moba_prompt(aqt, ak, avt, keep, bias, *, group=4):
    b, d_att, t = aqt.shape
    tiles = d_att // LANES
    n_blocks = t // MOBA_BLOCK
    n_bias = bias.shape[1]
    group = min(group, n_blocks)
    assert n_blocks % group == 0 and group & (group - 1) == 0
    return pl.pallas_call(
        functools.partial(_moba_prompt_body, n_bias=n_bias, n_blocks=n_blocks, group=group),
        grid=(b, tiles, n_blocks),
        in_specs=[pl.BlockSpec((1, LANES, MOBA_BLOCK), lambda ib, p, i: (ib, p, i)),
                  pl.BlockSpec((1, t, LANES), lambda ib, p, i: (ib, 0, p)),
                  pl.BlockSpec((1, LANES, t), lambda ib, p, i: (ib, p, 0)),
                  pl.BlockSpec((1, 1, keep.shape[2], MOBA_BLOCK), lambda ib, p, i: (ib, p, 0, i)),
                  pl.BlockSpec((HEADS_PER_TILE, n_bias, MOBA_BLOCK, MOBA_BLOCK), lambda ib, p, i: (p, 0, 0, 0))],
        out_specs=pl.BlockSpec((1, MOBA_BLOCK, LANES), lambda ib, p, i: (ib, i, p)),
        out_shape=jax.ShapeDtypeStruct((b, t, d_att), F32),
        scratch_shapes=[pltpu.VMEM((HEADS_PER_TILE, t, 2 * LANES), BF16),
                        pltpu.VMEM((HEADS_PER_TILE, LANES, t), BF16),
                        pltpu.VMEM((HEADS_PER_TILE, t, MOBA_BLOCK), F32)],
        compiler_params=pltpu.CompilerParams(
            dimension_semantics=("arbitrary", "arbitrary", "arbitrary"), vmem_limit_bytes=VMEM_LIMIT),
        name="moba_prompt",
    )(aqt, ak, avt, keep, bias)


def _moba_sample_body(pt_ref, q_ref, kn_ref, vn_ref, bias_ref, *rest, n_pages, heads, rows):
    k_pages = rest[:n_pages]
    v_pages = rest[n_pages:2 * n_pages]
    o_ref = rest[2 * n_pages]
    s_scr = rest[2 * n_pages + 1]
    d_att = heads * ATT_HEAD_DIM
    page = k_pages[0].shape[-1]
    pages_per_block = MOBA_BLOCK // page
    n_blocks = n_pages // pages_per_block
    n_q = heads * rows
    feat = lax.broadcasted_iota(jnp.int32, (1, d_att), 1)
    lane = lax.broadcasted_iota(jnp.int32, (n_q, LANES), 1)

    q = q_ref[0] * (ATT_HEAD_DIM ** -0.5)
    q_bd = jnp.concatenate([jnp.where(_in_head(feat, h), q, 0.0) for h in range(heads)], axis=0).astype(BF16)

    gate = jnp.zeros((n_q, LANES), F32)
    for n in range(n_blocks):
        tot = jnp.zeros((n_q, 1), F32)
        for p in range(n * pages_per_block, (n + 1) * pages_per_block):
            kt = k_pages[p][...].reshape(d_att, page).astype(BF16)
            s = jnp.dot(q_bd, kt, preferred_element_type=F32)
            s_scr[:, p * page:(p + 1) * page] = s
            tot = tot + jnp.sum(s, axis=-1, keepdims=True)
        gate = jnp.where(lane == n, tot * (1.0 / MOBA_BLOCK), gate)
    keep = _topk_keep(gate, lane, jnp.full(gate.shape, n_blocks, jnp.int32), axis=1, n=n_blocks)

    def new_rows(ref):
        return jnp.concatenate([ref[0], jnp.zeros((LANES - rows, d_att), F32)], axis=0).astype(BF16)

    own = n_pages * page
    s_own = lax.dot_general(q_bd, new_rows(kn_ref), NT_DIMS, preferred_element_type=F32)
    s_own = s_own + bias_ref[:, own:own + LANES]
    s_scr[:, own:own + LANES] = s_own
    m = jnp.max(s_own, axis=-1, keepdims=True)
    for p in range(n_pages):
        n = p // pages_per_block
        s = s_scr[:, p * page:(p + 1) * page] + bias_ref[:, p * page:(p + 1) * page] + keep[:, n:n + 1]
        s_scr[:, p * page:(p + 1) * page] = s
        m = jnp.maximum(m, jnp.max(s, axis=-1, keepdims=True))

    pr = jnp.exp(s_scr[:, own:own + LANES] - m)
    l = jnp.sum(pr, axis=-1, keepdims=True)
    acc = jnp.dot(pr.astype(BF16), new_rows(vn_ref), preferred_element_type=F32)
    for p in range(n_pages):
        pr = jnp.exp(s_scr[:, p * page:(p + 1) * page] - m)
        l = l + jnp.sum(pr, axis=-1, keepdims=True)
        vt = v_pages[p][...].reshape(d_att, page).astype(BF16)
        acc = acc + lax.dot_general(pr.astype(BF16), vt, NT_DIMS, preferred_element_type=F32)
    acc = acc / l
    out = jnp.zeros((rows, d_att), F32)
    for h in range(heads):
        out = jnp.where(_in_head(feat, h), acc[h * rows:(h + 1) * rows, :], out)
    o_ref[0] = out


def _moba_sample(aq, ak, av, cache_kt, cache_vt, page_table, bias, *, heads):
    nb, rows, d_att = aq.shape
    n_pages = page_table.shape[1]
    page = cache_kt.shape[-1]
    assert cache_kt.shape[2:] == (heads, ATT_HEAD_DIM, page) and MOBA_BLOCK % page == 0 and page == LANES
    tok = pl.BlockSpec((1, rows, d_att), lambda b, pt: (b, 0, 0))

    def page_spec(p):
        return pl.BlockSpec((None, None, heads, ATT_HEAD_DIM, page), lambda b, pt, p=p: (0, pt[b, p], 0, 0, 0))

    grid_spec = pltpu.PrefetchScalarGridSpec(
        num_scalar_prefetch=1,
        grid=(nb,),
        in_specs=[tok, tok, tok, pl.BlockSpec(bias.shape, lambda b, pt: (0, 0))]
                 + [page_spec(p) for p in range(n_pages)] * 2,
        out_specs=tok,
        scratch_shapes=[pltpu.VMEM(bias.shape, F32)],
    )
    return pl.pallas_call(
        functools.partial(_moba_sample_body, n_pages=n_pages, heads=heads, rows=rows),
        grid_spec=grid_spec,
        out_shape=jax.ShapeDtypeStruct((nb, rows, d_att), F32),
        compiler_params=pltpu.CompilerParams(
            dimension_semantics=("arbitrary",), vmem_limit_bytes=VMEM_LIMIT),
        name="moba_sample",
    )(page_table, aq, ak, av, bias, *([cache_kt] * n_pages), *([cache_vt] * n_pages))


def _out_proj_body(ol_ref, oa_ref, ga_ref, x_ref, w_ref, gpost_ref, y_ref, *, d_lin):
    o = jnp.dot(ol_ref[...].astype(BF16), w_ref[0:d_lin, :], preferred_element_type=F32)
    o = o + jnp.dot((oa_ref[...] * ga_ref[...]).astype(BF16), w_ref[d_lin:, :],
                    preferred_element_type=F32)
    y_ref[...] = x_ref[...] + _rms(o, gpost_ref[...])


def _out_proj(o_lin, o_att, ga, x, w_bf, gpost, *, tm=256):
    rows, d_model = x.shape
    tm = min(tm, rows)
    assert rows % tm == 0
    d_lin = o_lin.shape[1]
    d_att = o_att.shape[1]
    return pl.pallas_call(
        functools.partial(_out_proj_body, d_lin=d_lin),
        grid=(rows // tm,),
        in_specs=[
            pl.BlockSpec((tm, d_lin), lambda i: (i, 0)),
            pl.BlockSpec((tm, d_att), lambda i: (i, 0)),
            pl.BlockSpec((tm, d_att), lambda i: (i, 0)),
            pl.BlockSpec((tm, d_model), lambda i: (i, 0)),
            pl.BlockSpec(w_bf.shape, lambda i: (0, 0)),
            pl.BlockSpec((1, d_model), lambda i: (0, 0)),
        ],
        out_specs=pl.BlockSpec((tm, d_model), lambda i: (i, 0)),
        out_shape=jax.ShapeDtypeStruct((rows, d_model), F32),
        compiler_params=pltpu.CompilerParams(dimension_semantics=("arbitrary",)),
        name="out_proj",
    )(o_lin, o_att, ga, x, w_bf, gpost)


def kernel(x_prompt, x_sample, cache_k, cache_v, state_hgrn, page_table, w_in, w_out,
           norm_pre, norm_post, norm_lin_out, lin_lower_bound, rel_bias):
    depth = w_in.shape[0]
    assert depth == 1 and lin_lower_bound.shape[0] == depth + 1
    b, t, d_model = x_prompt.shape
    nb, ts, _ = x_sample.shape
    d_lin = lin_lower_bound.shape[1]
    d_att = w_out.shape[1] - d_lin
    lin_heads = d_lin // LIN_HEAD_DIM
    att_heads = rel_bias.shape[1]
    n_pages = page_table.shape[1]
    page = cache_k.shape[2]
    past_len = n_pages * page
    assert d_att == att_heads * ATT_HEAD_DIM and t % MOBA_BLOCK == 0
    assert past_len % MOBA_BLOCK == 0 and ts <= SUBLANES

    w_in_bf = w_in[0].astype(BF16)
    w_out_bf = w_out[0].astype(BF16)
    gpre, gpost, glin = norm_pre, norm_post, norm_lin_out
    proj = functools.partial(_in_proj, gpre=gpre, w_bf=w_in_bf, llb=lin_lower_bound,
                             d_lin=d_lin, d_att=d_att)

    xp = x_prompt.reshape(b * t, d_model)
    q, g, kk, v, gl, ga, ak, aqt, akt, avt = proj(xp, seq_len=t)
    q, g, kk, v, gl, ak = [a.reshape(b, t, -1) for a in (q, g, kk, v, gl, ak)]
    s0 = jnp.zeros((b, lin_heads, LIN_HEAD_DIM, LIN_HEAD_DIM), F32)
    o_lin, s_prompt = _hgrn_tiles(q, g, kk, v, gl, s0, glin, chunk=32, t_tile=512)
    keep = _moba_gate(aqt, ak)
    n_bias = 6
    assert (n_bias - 1) * MOBA_BLOCK - (MOBA_BLOCK - 1) >= MAX_DISTANCE
    bias = _prompt_bias(rel_bias, n_bias)
    o_att = _moba_prompt(aqt, ak, avt, keep, bias)
    y_prompt = _out_proj(o_lin.reshape(b * t, d_lin), o_att.reshape(b * t, d_att),
                         ga, xp, w_out_bf, gpost).reshape(b, t, d_model)

    rows = SUBLANES
    xs = jnp.pad(x_sample, ((0, 0), (0, rows - ts), (0, 0))).reshape(nb * rows, d_model)
    qs, gs, kks, vs, gls, gas, aqs, aks, avs = [a.reshape(nb, rows, -1) for a in proj(xs)]
    o_lin_s, s_sample = _hgrn(qs, gs, kks, vs, gls, state_hgrn[0], glin,
                              chunk=rows, valid=ts, t_tile=rows, n_seq=8)
    bias_s = _sample_bias(rel_bias, past_len=past_len, n_valid=ts, rows=rows)
    cache_kt = cache_k.transpose(0, 1, 3, 4, 2)
    cache_vt = cache_v.transpose(0, 1, 3, 4, 2)
    o_att_s = _moba_sample(aqs, aks, avs, cache_kt, cache_vt, page_table, bias_s, heads=att_heads)
    y_s = _out_proj(o_lin_s.reshape(nb * rows, d_lin), o_att_s.reshape(nb * rows, d_att),
                    gas.reshape(nb * rows, d_att), xs, w_out_bf, gpost)
    y_sample = y_s.reshape(nb, rows, d_model)[:, :ts]

    def prompt_kv(a):
        return a.reshape(1, b, att_heads, ATT_HEAD_DIM, t).transpose(0, 1, 4, 2, 3)

    kvs_shape = (1, nb, ts, att_heads, ATT_HEAD_DIM)
    return (y_prompt, y_sample, prompt_kv(akt), prompt_kv(avt), s_prompt[None],
            aks[:, :ts].reshape(kvs_shape), avs[:, :ts].reshape(kvs_shape), s_sample[None])
```

```python
import functools
import math

import jax
import jax.numpy as jnp
from jax import lax
from jax.experimental import pallas as pl
from jax.experimental.pallas import tpu as pltpu

F32 = jnp.float32
BF16 = jnp.bfloat16

EPS = 1e-6
LIN_HEAD_DIM = 128
ATT_HEAD_DIM = 64
MOBA_BLOCK = 256
MOBA_TOPK = 3
N_BUCKETS = 32
MAX_DISTANCE = 1024
NEG = -2e30
M_INIT = -1e30
LOG2E = 1.4426950408889634
LANES = 128
SUBLANES = 8
VMEM_LIMIT = 56 * 1024 * 1024
HEADS_PER_TILE = LANES // ATT_HEAD_DIM

NT_DIMS = (((1,), (1,)), ((), ()))
TN_DIMS = (((0,), (0,)), ((), ()))


def _sigmoid(x):
    return 1.0 / (1.0 + jnp.exp(-x))


def _silu(x):
    return x * _sigmoid(x)


def _rms(x, g):
    return x * lax.rsqrt(jnp.mean(x * x, axis=-1, keepdims=True) + EPS) * g


def _in_head(idx, head):
    return (idx >= head * ATT_HEAD_DIM) & (idx < (head + 1) * ATT_HEAD_DIM)


def _in_proj_body(x_ref, gpre_ref, w_ref, llb_ref, *rest, d_lin, d_att, feat_major):
    if feat_major:
        wt_ref, q_o, g_o, kk_o, v_o, gl_o, ga_o, ak_o, aqt_o, akt_o, avt_o = rest
    else:
        q_o, g_o, kk_o, v_o, gl_o, ga_o, aq_o, ak_o, av_o = rest
    h = _rms(x_ref[...], gpre_ref[...]).astype(BF16)

    def proj(col, width):
        return jnp.dot(h, w_ref[:, col:col + width], preferred_element_type=F32)

    def proj_t(row, width):
        return lax.dot_general(wt_ref[row:row + width, :], h, NT_DIMS, preferred_element_type=F32)

    llb = llb_ref[...]
    e = jnp.exp(llb - jnp.max(llb, axis=0, keepdims=True))
    lb = e[0:1, :] / jnp.sum(e, axis=0, keepdims=True)

    q_o[...] = _silu(proj(0, d_lin))
    f = lb + (1.0 - lb) * _sigmoid(proj(d_lin, d_lin))
    g_o[...] = jnp.log(f)
    kk_o[...] = 1.0 - f
    v_o[...] = proj(2 * d_lin, d_lin)
    gl_o[...] = _silu(proj(3 * d_lin, d_lin))
    a0 = 4 * d_lin
    ga_o[...] = _silu(proj(a0 + 3 * d_att, d_att))
    if feat_major:
        aqt_o[0] = proj_t(0, d_att)
        akt = proj_t(d_att, d_att)
        akt_o[0] = akt
        ak_o[...] = akt.T
        avt_o[0] = proj_t(2 * d_att, d_att)
    else:
        aq_o[...] = proj(a0, d_att)
        ak_o[...] = proj(a0 + d_att, d_att)
        av_o[...] = proj(a0 + 2 * d_att, d_att)


def _in_proj(x, gpre, w_bf, llb, *, d_lin, d_att, seq_len=None, tm=256):
    rows, d_model = x.shape
    tm = min(tm, rows)
    assert rows % tm == 0
    d_in = w_bf.shape[1]
    feat_major = seq_len is not None
    row_spec = lambda w: pl.BlockSpec((tm, w), lambda i: (i, 0))
    in_specs = [
        pl.BlockSpec((tm, d_model), lambda i: (i, 0)),
        pl.BlockSpec((1, d_model), lambda i: (0, 0)),
        pl.BlockSpec((d_model, d_in), lambda i: (0, 0)),
        pl.BlockSpec(llb.shape, lambda i: (0, 0)),
    ]
    args = [x, gpre, w_bf, llb]
    out_specs = [row_spec(d_lin)] * 5 + [row_spec(d_att)]
    out_shape = [jax.ShapeDtypeStruct((rows, d_lin), F32)] * 5 + [jax.ShapeDtypeStruct((rows, d_att), F32)]
    if feat_major:
        assert seq_len % tm == 0
        tiles = seq_len // tm
        a0 = 4 * d_lin
        wt_bf = w_bf[:, a0:a0 + 3 * d_att].T
        in_specs.append(pl.BlockSpec(wt_bf.shape, lambda i: (0, 0)))
        args.append(wt_bf)
        t_spec = pl.BlockSpec((1, d_att, tm), lambda i: (i // tiles, 0, i % tiles))
        t_shape = jax.ShapeDtypeStruct((rows // seq_len, d_att, seq_len), F32)
        out_specs += [row_spec(d_att)] + [t_spec] * 3
        out_shape += [jax.ShapeDtypeStruct((rows, d_att), F32)] + [t_shape] * 3
    else:
        out_specs += [row_spec(d_att)] * 3
        out_shape += [jax.ShapeDtypeStruct((rows, d_att), F32)] * 3
    return pl.pallas_call(
        functools.partial(_in_proj_body, d_lin=d_lin, d_att=d_att, feat_major=feat_major),
        grid=(rows // tm,),
        in_specs=in_specs,
        out_specs=out_specs,
        out_shape=out_shape,
        compiler_params=pltpu.CompilerParams(
            dimension_semantics=("arbitrary",), vmem_limit_bytes=VMEM_LIMIT),
        name="in_proj",
    )(*args)


def _hgrn_body(q_ref, g_ref, kk_ref, v_ref, gl_ref, s0_ref, glin_ref,
               o_ref, sout_ref, st_scr, b_scr, *, chunk, valid, n_chunks, n_seq):
    it = pl.program_id(2)

    @pl.when(it == 0)
    def _():
        for s in range(n_seq):
            st_scr[s] = s0_ref[s, 0].T

    row = lax.broadcasted_iota(jnp.int32, (chunk, 1), 0)

    for s in range(n_seq):
        def chunk_step(ci, carry, s=s):
            r0 = pl.multiple_of(ci * chunk, chunk)
            acc = jnp.zeros((1, LIN_HEAD_DIM), F32)
            for r in range(chunk):
                if r < valid:
                    acc = acc + g_ref[s, pl.ds(r0 + r, 1), :]
                b_scr[pl.ds(r, 1), :] = acc
            b = b_scr[...]
            b_last = b_scr[pl.ds(chunk - 1, 1), :]
            q = q_ref[s, pl.ds(r0, chunk), :]
            kk = kk_ref[s, pl.ds(r0, chunk), :]
            v = v_ref[s, pl.ds(r0, chunk), :]
            if valid < chunk:
                kk = jnp.where(row < valid, kk, 0.0)
            st = st_scr[s]
            o = lax.dot_general((q * jnp.exp(b)).astype(BF16), st.astype(BF16), NT_DIMS,
                                preferred_element_type=F32)
            for j in range(valid):
                bj = b_scr[pl.ds(j, 1), :]
                kj = kk_ref[s, pl.ds(r0 + j, 1), :]
                vj = v_ref[s, pl.ds(r0 + j, 1), :]
                w = q * kj * jnp.exp(jnp.minimum(b - bj, 0.0))
                a = jnp.sum(w, axis=-1, keepdims=True)
                o = o + jnp.where(row >= j, a, 0.0) * vj
            kt = kk * jnp.exp(b_last - b)
            st_scr[s] = st * jnp.exp(b_last) + lax.dot_general(
                v.astype(BF16), kt.astype(BF16), TN_DIMS, preferred_element_type=F32)
            o_ref[s, pl.ds(r0, chunk), :] = _rms(o, glin_ref[...]) * gl_ref[s, pl.ds(r0, chunk), :]
            return carry

        lax.fori_loop(0, n_chunks, chunk_step, 0)

    @pl.when(it == pl.num_programs(2) - 1)
    def _():
        for s in range(n_seq):
            sout_ref[s, 0] = st_scr[s].T


def _hgrn(q, g, kk, v, gl, s0, glin, *, chunk, valid, t_tile, n_seq):
    nb, t, d_lin = q.shape
    heads = d_lin // LIN_HEAD_DIM
    seq_spec = pl.BlockSpec((n_seq, t_tile, LIN_HEAD_DIM), lambda b, h, i: (b, i, h))
    st_spec = pl.BlockSpec((n_seq, 1, LIN_HEAD_DIM, LIN_HEAD_DIM), lambda b, h, i: (b, h, 0, 0))
    return pl.pallas_call(
        functools.partial(_hgrn_body, chunk=chunk, valid=valid,
                          n_chunks=t_tile // chunk, n_seq=n_seq),
        grid=(nb // n_seq, heads, t // t_tile),
        in_specs=[seq_spec] * 5 + [st_spec, pl.BlockSpec((1, LIN_HEAD_DIM), lambda b, h, i: (0, 0))],
        out_specs=[seq_spec, st_spec],
        out_shape=[jax.ShapeDtypeStruct((nb, t, d_lin), F32),
                   jax.ShapeDtypeStruct(s0.shape, F32)],
        scratch_shapes=[pltpu.VMEM((n_seq, LIN_HEAD_DIM, LIN_HEAD_DIM), F32),
                        pltpu.VMEM((chunk, LIN_HEAD_DIM), F32)],
        compiler_params=pltpu.CompilerParams(
            dimension_semantics=("arbitrary", "arbitrary", "arbitrary")),
        name="hgrn",
    )(q, g, kk, v, gl, s0, glin)


SAFE_DECAY = 60.0


def _hgrn_tile_body(q_ref, g_ref, kk_ref, v_ref, gl_ref, s0_ref, glin_ref, o_ref, sout_ref,
                    st_scr, b_scr, oi_scr, u_scr, kkp_scr, bp_scr, vp_scr, *, chunk, t_tile):
    it = pl.program_id(2)

    @pl.when(it == 0)
    def _():
        st_scr[...] = s0_ref[0, 0].T

    lg = int(math.log2(chunk))
    n_groups = t_tile // LANES
    r = lax.broadcasted_iota(jnp.int32, (LANES, LANES), 0)
    c = lax.broadcasted_iota(jnp.int32, (LANES, LANES), 1)
    causal = ((r >> lg) == (c >> lg)) & (c <= r)
    tri = jnp.where(causal, 1.0, 0.0).astype(BF16)
    for gi in range(n_groups):
        rows = slice(gi * LANES, (gi + 1) * LANES)
        rest = g_ref[0, rows, :]
        b = jnp.zeros((LANES, LIN_HEAD_DIM), F32)
        for _ in range(3):
            piece = rest.astype(BF16)
            rest = rest - piece.astype(F32)
            b = b + jnp.dot(tri, piece, preferred_element_type=F32)
        b_scr[rows, :] = b

    def intra_mxu():
        for gi in range(n_groups):
            rows = slice(gi * LANES, (gi + 1) * LANES)
            b = b_scr[rows, :]
            qe = (q_ref[0, rows, :] * jnp.exp(b)).astype(BF16)
            ke = (kk_ref[0, rows, :] * jnp.exp(-b)).astype(BF16)
            att = lax.dot_general(qe, ke, NT_DIMS, preferred_element_type=F32)
            att = jnp.where(causal, att, 0.0).astype(BF16)
            oi_scr[rows, :] = jnp.dot(att, v_ref[0, rows, :].astype(BF16), preferred_element_type=F32)

    def intra_pairs():
        zeros = jnp.zeros((chunk, LIN_HEAD_DIM), F32)
        for dst, src in ((kkp_scr, kk_ref[0]), (bp_scr, b_scr[...]), (vp_scr, v_ref[0])):
            dst[0:chunk, :] = zeros
            dst[chunk:, :] = src
        oi_scr[...] = jnp.zeros(oi_scr.shape, F32)
        q = q_ref[0]
        b = b_scr[...]
        tmod = lax.broadcasted_iota(jnp.int32, (t_tile, 1), 0) & (chunk - 1)

        def lag(d, carry):
            start = chunk - d
            w = q * kkp_scr[pl.ds(start, t_tile), :] * jnp.exp(
                jnp.minimum(b - bp_scr[pl.ds(start, t_tile), :], 0.0))
            a = jnp.sum(w, axis=-1, keepdims=True)
            oi_scr[...] += jnp.where(tmod >= d, a, 0.0) * vp_scr[pl.ds(start, t_tile), :]
            return carry

        lax.fori_loop(0, chunk, lag, 0)

    lax.cond(jnp.min(b_scr[...]) >= -SAFE_DECAY, intra_mxu, intra_pairs)

    n_chunks = t_tile // chunk
    for n in range(n_chunks):
        rows = slice(n * chunk, (n + 1) * chunk)
        b_last = b_scr[(n + 1) * chunk - 1:(n + 1) * chunk, :]
        kt = (kk_ref[0, rows, :] * jnp.exp(b_last - b_scr[rows, :])).astype(BF16)
        u_scr[n] = lax.dot_general(v_ref[0, rows, :].astype(BF16), kt, TN_DIMS, preferred_element_type=F32)

    st = st_scr[...]
    for n in range(n_chunks):
        rows = slice(n * chunk, (n + 1) * chunk)
        b_last = b_scr[(n + 1) * chunk - 1:(n + 1) * chunk, :]
        o = lax.dot_general((q_ref[0, rows, :] * jnp.exp(b_scr[rows, :])).astype(BF16), st.astype(BF16),
                            NT_DIMS, preferred_element_type=F32) + oi_scr[rows, :]
        st = st * jnp.exp(b_last) + u_scr[n]
        o_ref[0, rows, :] = (_rms(o, glin_ref[...]) * gl_ref[0, rows, :]).astype(o_ref.dtype)
    st_scr[...] = st

    @pl.when(it == pl.num_programs(2) - 1)
    def _():
        sout_ref[0, 0] = st.T


def _hgrn_tiles(q, g, kk, v, gl, s0, glin, *, chunk, t_tile):
    nb, t, d_lin = q.shape
    heads = d_lin // LIN_HEAD_DIM
    assert LANES % chunk == 0 and t_tile % LANES == 0 and t % t_tile == 0
    seq_spec = pl.BlockSpec((1, t_tile, LIN_HEAD_DIM), lambda b, h, i: (b, i, h))
    st_spec = pl.BlockSpec((1, 1, LIN_HEAD_DIM, LIN_HEAD_DIM), lambda b, h, i: (b, h, 0, 0))
    tile = pltpu.VMEM((t_tile, LIN_HEAD_DIM), F32)
    halo = pltpu.VMEM((chunk + t_tile, LIN_HEAD_DIM), F32)
    return pl.pallas_call(
        functools.partial(_hgrn_tile_body, chunk=chunk, t_tile=t_tile),
        grid=(nb, heads, t // t_tile),
        in_specs=[seq_spec] * 5 + [st_spec, pl.BlockSpec((1, LIN_HEAD_DIM), lambda b, h, i: (0, 0))],
        out_specs=[seq_spec, st_spec],
        out_shape=[jax.ShapeDtypeStruct((nb, t, d_lin), BF16),
                   jax.ShapeDtypeStruct(s0.shape, F32)],
        scratch_shapes=[pltpu.VMEM((LIN_HEAD_DIM, LIN_HEAD_DIM), F32), tile, tile,
                        pltpu.VMEM((t_tile // chunk, LIN_HEAD_DIM, LIN_HEAD_DIM), F32), halo, halo, halo],
        compiler_params=pltpu.CompilerParams(
            dimension_semantics=("arbitrary", "arbitrary", "arbitrary")),
        name="hgrn_tiles",
    )(q, g, kk, v, gl, s0, glin)


def _t5_bias(dist, rb_ref, head):
    max_exact = N_BUCKETS // 2
    n = jnp.maximum(dist, 0)
    large = max_exact + (jnp.log(jnp.maximum(n, 1).astype(F32) / max_exact)
                         / math.log(MAX_DISTANCE / max_exact) * (N_BUCKETS - max_exact)).astype(jnp.int32)
    bucket = jnp.where(n < max_exact, n, jnp.minimum(large, N_BUCKETS - 1))
    out = jnp.zeros(dist.shape, F32)
    for bk in range(N_BUCKETS):
        out = jnp.where(bucket == bk, rb_ref[bk, head], out)
    return out


def _prompt_bias_body(rb_ref, o_ref):
    h = pl.program_id(0)
    d = pl.program_id(1)
    x = lax.broadcasted_iota(jnp.int32, (SUBLANES, 2 * MOBA_BLOCK), 1)
    dist = (d - 1) * MOBA_BLOCK + x
    by_dist = jnp.where(dist >= 0, _t5_bias(dist, rb_ref, h) * LOG2E, NEG)
    wide = jnp.broadcast_to(by_dist[0:1, :], (MOBA_BLOCK, 2 * MOBA_BLOCK))
    o_ref[0, 0] = pltpu.roll(wide, 0, 1, stride=1, stride_axis=0)[:, MOBA_BLOCK:]


def _prompt_bias(rel_bias, n_tiles):
    heads = rel_bias.shape[1]
    return pl.pallas_call(
        _prompt_bias_body,
        grid=(heads, n_tiles),
        in_specs=[pl.BlockSpec(memory_space=pltpu.SMEM)],
        out_specs=pl.BlockSpec((1, 1, MOBA_BLOCK, MOBA_BLOCK), lambda h, d: (h, d, 0, 0)),
        out_shape=jax.ShapeDtypeStruct((heads, n_tiles, MOBA_BLOCK, MOBA_BLOCK), F32),
        name="prompt_bias",
    )(rel_bias)


def _sample_bias_body(rb_ref, o_ref, *, past_len, n_valid, rows):
    h = pl.program_id(0)
    width = past_len + LANES
    r = lax.broadcasted_iota(jnp.int32, (rows, width), 0)
    kpos = lax.broadcasted_iota(jnp.int32, (rows, width), 1)
    dist = past_len + r - kpos
    ok = (dist >= 0) & (kpos < past_len + n_valid)
    o_ref[...] = jnp.where(ok, _t5_bias(dist, rb_ref, h), NEG)


def _sample_bias(rel_bias, *, past_len, n_valid, rows):
    heads = rel_bias.shape[1]
    width = past_len + LANES
    return pl.pallas_call(
        functools.partial(_sample_bias_body, past_len=past_len, n_valid=n_valid, rows=rows),
        grid=(heads,),
        in_specs=[pl.BlockSpec(memory_space=pltpu.SMEM)],
        out_specs=pl.BlockSpec((rows, width), lambda h: (h, 0)),
        out_shape=jax.ShapeDtypeStruct((heads * rows, width), F32),
        name="sample_bias",
    )(rel_bias)


def _topk_keep(gate, idx, own, axis, n):
    past = idx < own
    gm = jnp.where(past, gate, -jnp.inf)
    rank = jnp.zeros(gate.shape, jnp.int32)
    for jp in range(n):
        gj = lax.slice_in_dim(gm, jp, jp + 1, axis=axis)
        beats = (gj > gm) | ((gj == gm) & (jp < idx))
        rank = rank + beats.astype(jnp.int32)
    keep = ((rank < MOBA_TOPK) & past) | (idx == own)
    return jnp.where(keep, 0.0, NEG)


def _moba_gate_body(qt_ref, k_ref, a_ref, km_scr, *, n_blocks):
    t = n_blocks * MOBA_BLOCK
    qt = qt_ref[0]
    lane = lax.broadcasted_iota(jnp.int32, (1, LANES), 1)
    for n in range(n_blocks):
        km_scr[pl.ds(n, 1), :] = jnp.sum(
            k_ref[0, n * MOBA_BLOCK:(n + 1) * MOBA_BLOCK, :], axis=0, keepdims=True) * (1.0 / MOBA_BLOCK)
    km = km_scr[...]
    blk = lax.broadcasted_iota(jnp.int32, (n_blocks, t), 0)
    own = lax.broadcasted_iota(jnp.int32, (n_blocks, t), 1) >> int(math.log2(MOBA_BLOCK))
    for hh in range(HEADS_PER_TILE):
        kmh = jnp.where(_in_head(lane, hh), km, 0.0)
        gate_t = jnp.dot(kmh, qt, precision=lax.Precision.HIGHEST, preferred_element_type=F32)
        a_ref[0, 0, hh * n_blocks:(hh + 1) * n_blocks, :] = _topk_keep(gate_t, blk, own, axis=0, n=n_blocks)


def _moba_gate(aqt, ak):
    b, d_att, t = aqt.shape
    tiles = d_att // LANES
    n_blocks = t // MOBA_BLOCK
    return pl.pallas_call(
        functools.partial(_moba_gate_body, n_blocks=n_blocks),
        grid=(b, tiles),
        in_specs=[pl.BlockSpec((1, LANES, t), lambda i, p: (i, p, 0)),
                  pl.BlockSpec((1, t, LANES), lambda i, p: (i, 0, p))],
        out_specs=pl.BlockSpec((1, 1, HEADS_PER_TILE * n_blocks, t), lambda i, p: (i, p, 0, 0)),
        out_shape=jax.ShapeDtypeStruct((b, tiles, HEADS_PER_TILE * n_blocks, t), F32),
        scratch_shapes=[pltpu.VMEM((n_blocks, LANES), F32)],
        compiler_params=pltpu.CompilerParams(
            dimension_semantics=("arbitrary", "arbitrary"), vmem_limit_bytes=VMEM_LIMIT),
        name="moba_gate",
    )(aqt, ak)


def _moba_prompt_body(qt_ref, k_ref, vt_ref, keep_ref, bias_ref, ga_ref, o_ref, kaug_scr, vtaug_scr, s_scr,
                      *, n_bias, n_blocks, group):
    i = pl.program_id(2)
    rows = group * MOBA_BLOCK
    t = n_blocks * MOBA_BLOCK
    lg_block = int(math.log2(MOBA_BLOCK))
    feat = lax.broadcasted_iota(jnp.int32, (LANES, 1), 0)
    ones_row = [((hh + 1) % HEADS_PER_TILE) * ATT_HEAD_DIM for hh in range(HEADS_PER_TILE)]

    @pl.when(i == 0)
    def _():
        blk = lax.broadcasted_iota(jnp.int32, (rows, LANES), 0) >> lg_block
        lane = lax.broadcasted_iota(jnp.int32, (rows, LANES), 1)
        for c in range(t // rows):
            sl = slice(c * rows, (c + 1) * rows)
            for hh in range(HEADS_PER_TILE):
                onehot = jnp.where(lane == hh * n_blocks + c * group + blk, 1.0, 0.0)
                kaug_scr[hh, sl, :] = jnp.concatenate([k_ref[0, sl, :], onehot], axis=1).astype(BF16)
                vtaug_scr[hh, :, sl] = jnp.where(feat == ones_row[hh], 1.0, vt_ref[0, :, sl]).astype(BF16)

    qt = qt_ref[0] * (ATT_HEAD_DIM ** -0.5 * LOG2E)
    keep = keep_ref[0, 0]
    pad = jnp.zeros((LANES - keep.shape[0], MOBA_BLOCK), F32)
    q_aug = [jnp.concatenate([jnp.where(_in_head(feat, hh), qt, 0.0), keep, pad], axis=0).astype(BF16)
             for hh in range(HEADS_PER_TILE)]
    n_groups = (i >> int(math.log2(group))) + 1

    def scores(gi, m):
        r0 = pl.multiple_of(gi * rows, rows)
        m_new = []
        for hh in range(HEADS_PER_TILE):
            s = jnp.dot(kaug_scr[hh, pl.ds(r0, rows), :], q_aug[hh], preferred_element_type=F32)
            mh = m[hh]
            for u in range(group):
                d = jnp.clip(i - (gi * group + u), 0, n_bias - 1)
                su = s[u * MOBA_BLOCK:(u + 1) * MOBA_BLOCK] + bias_ref[hh, d]
                s_scr[hh, pl.ds(r0 + u * MOBA_BLOCK, MOBA_BLOCK), :] = su
                mh = jnp.maximum(mh, jnp.max(su, axis=0, keepdims=True))
            m_new.append(mh)
        return tuple(m_new)

    def values(gi, m, m_before, acc):
        r0 = pl.multiple_of(gi * rows, rows)
        out = []
        for hh in range(HEADS_PER_TILE):
            p = jnp.exp2(s_scr[hh, pl.ds(r0, rows), :] - m[hh]).astype(BF16)
            out.append(acc[hh] * jnp.exp2(m_before[hh] - m[hh])
                       + jnp.dot(vtaug_scr[hh, :, pl.ds(r0, rows)], p, preferred_element_type=F32))
        return tuple(out)

    m_init = tuple(jnp.full((1, MOBA_BLOCK), M_INIT, F32) for _ in range(HEADS_PER_TILE))
    acc0 = tuple(jnp.zeros((LANES, MOBA_BLOCK), F32) for _ in range(HEADS_PER_TILE))

    def stage(gi, carry):
        m, m_before, acc = carry
        acc = values(gi - 1, m, m_before, acc)
        return scores(gi, m), m, acc

    m, m_before, acc = lax.fori_loop(1, n_groups, stage, (scores(0, m_init), m_init, acc0))
    acc = values(n_groups - 1, m, m_before, acc)
    out = jnp.zeros((LANES, MOBA_BLOCK), F32)
    for hh in range(HEADS_PER_TILE):
        out = jnp.where(_in_head(feat, hh), acc[hh] / acc[hh][ones_row[hh]:ones_row[hh] + 1, :], out)
    o_ref[0] = (out.T * ga_ref[0]).astype(o_ref.dtype)


def _moba_prompt(aqt, ak, avt, keep, bias, ga, *, group=4):
    b, d_att, t = aqt.shape
    tiles = d_att // LANES
    n_blocks = t // MOBA_BLOCK
    n_bias = bias.shape[1]
    group = min(group, n_blocks)
    assert n_blocks % group == 0 and group & (group - 1) == 0
    return pl.pallas_call(
        functools.partial(_moba_prompt_body, n_bias=n_bias, n_blocks=n_blocks, group=group),
        grid=(b, tiles, n_blocks),
        in_specs=[pl.BlockSpec((1, LANES, MOBA_BLOCK), lambda ib, p, i: (ib, p, i)),
                  pl.BlockSpec((1, t, LANES), lambda ib, p, i: (ib, 0, p)),
                  pl.BlockSpec((1, LANES, t), lambda ib, p, i: (ib, p, 0)),
                  pl.BlockSpec((1, 1, keep.shape[2], MOBA_BLOCK), lambda ib, p, i: (ib, p, 0, i)),
                  pl.BlockSpec((HEADS_PER_TILE, n_bias, MOBA_BLOCK, MOBA_BLOCK), lambda ib, p, i: (p, 0, 0, 0)),
                  pl.BlockSpec((1, MOBA_BLOCK, LANES), lambda ib, p, i: (ib, i, p))],
        out_specs=pl.BlockSpec((1, MOBA_BLOCK, LANES), lambda ib, p, i: (ib, i, p)),
        out_shape=jax.ShapeDtypeStruct((b, t, d_att), BF16),
        scratch_shapes=[pltpu.VMEM((HEADS_PER_TILE, t, 2 * LANES), BF16),
                        pltpu.VMEM((HEADS_PER_TILE, LANES, t), BF16),
                        pltpu.VMEM((HEADS_PER_TILE, t, MOBA_BLOCK), F32)],
        compiler_params=pltpu.CompilerParams(
            dimension_semantics=("arbitrary", "arbitrary", "arbitrary"), vmem_limit_bytes=VMEM_LIMIT),
        name="moba_prompt",
    )(aqt, ak, avt, keep, bias, ga.reshape(b, t, d_att))


def _moba_sample_body(pt_ref, q_ref, kn_ref, vn_ref, bias_ref, *rest, n_pages, heads, rows):
    k_pages = rest[:n_pages]
    v_pages = rest[n_pages:2 * n_pages]
    o_ref = rest[2 * n_pages]
    s_scr = rest[2 * n_pages + 1]
    d_att = heads * ATT_HEAD_DIM
    page = k_pages[0].shape[-1]
    pages_per_block = MOBA_BLOCK // page
    n_blocks = n_pages // pages_per_block
    n_q = heads * rows
    feat = lax.broadcasted_iota(jnp.int32, (1, d_att), 1)
    lane = lax.broadcasted_iota(jnp.int32, (n_q, LANES), 1)

    q = q_ref[0] * (ATT_HEAD_DIM ** -0.5)
    q_bd = jnp.concatenate([jnp.where(_in_head(feat, h), q, 0.0) for h in range(heads)], axis=0).astype(BF16)

    gate = jnp.zeros((n_q, LANES), F32)
    for n in range(n_blocks):
        tot = jnp.zeros((n_q, 1), F32)
        for p in range(n * pages_per_block, (n + 1) * pages_per_block):
            kt = k_pages[p][...].reshape(d_att, page).astype(BF16)
            s = jnp.dot(q_bd, kt, preferred_element_type=F32)
            s_scr[:, p * page:(p + 1) * page] = s
            tot = tot + jnp.sum(s, axis=-1, keepdims=True)
        gate = jnp.where(lane == n, tot * (1.0 / MOBA_BLOCK), gate)
    keep = _topk_keep(gate, lane, jnp.full(gate.shape, n_blocks, jnp.int32), axis=1, n=n_blocks)

    def new_rows(ref):
        return jnp.concatenate([ref[0], jnp.zeros((LANES - rows, d_att), F32)], axis=0).astype(BF16)

    own = n_pages * page
    s_own = lax.dot_general(q_bd, new_rows(kn_ref), NT_DIMS, preferred_element_type=F32)
    s_own = s_own + bias_ref[:, own:own + LANES]
    s_scr[:, own:own + LANES] = s_own
    m = jnp.max(s_own, axis=-1, keepdims=True)
    for p in range(n_pages):
        n = p // pages_per_block
        s = s_scr[:, p * page:(p + 1) * page] + bias_ref[:, p * page:(p + 1) * page] + keep[:, n:n + 1]
        s_scr[:, p * page:(p + 1) * page] = s
        m = jnp.maximum(m, jnp.max(s, axis=-1, keepdims=True))

    pr = jnp.exp(s_scr[:, own:own + LANES] - m)
    l = jnp.sum(pr, axis=-1, keepdims=True)
    acc = jnp.dot(pr.astype(BF16), new_rows(vn_ref), preferred_element_type=F32)
    for p in range(n_pages):
        pr = jnp.exp(s_scr[:, p * page:(p + 1) * page] - m)
        l = l + jnp.sum(pr, axis=-1, keepdims=True)
        vt = v_pages[p][...].reshape(d_att, page).astype(BF16)
        acc = acc + lax.dot_general(pr.astype(BF16), vt, NT_DIMS, preferred_element_type=F32)
    acc = acc / l
    out = jnp.zeros((rows, d_att), F32)
    for h in range(heads):
        out = jnp.where(_in_head(feat, h), acc[h * rows:(h + 1) * rows, :], out)
    o_ref[0] = out


def _moba_sample(aq, ak, av, cache_kt, cache_vt, page_table, bias, *, heads):
    nb, rows, d_att = aq.shape
    n_pages = page_table.shape[1]
    page = cache_kt.shape[-1]
    assert cache_kt.shape[2:] == (heads, ATT_HEAD_DIM, page) and MOBA_BLOCK % page == 0 and page == LANES
    tok = pl.BlockSpec((1, rows, d_att), lambda b, pt: (b, 0, 0))

    def page_spec(p):
        return pl.BlockSpec((None, None, heads, ATT_HEAD_DIM, page), lambda b, pt, p=p: (0, pt[b, p], 0, 0, 0))

    grid_spec = pltpu.PrefetchScalarGridSpec(
        num_scalar_prefetch=1,
        grid=(nb,),
        in_specs=[tok, tok, tok, pl.BlockSpec(bias.shape, lambda b, pt: (0, 0))]
                 + [page_spec(p) for p in range(n_pages)] * 2,
        out_specs=tok,
        scratch_shapes=[pltpu.VMEM(bias.shape, F32)],
    )
    return pl.pallas_call(
        functools.partial(_moba_sample_body, n_pages=n_pages, heads=heads, rows=rows),
        grid_spec=grid_spec,
        out_shape=jax.ShapeDtypeStruct((nb, rows, d_att), F32),
        compiler_params=pltpu.CompilerParams(
            dimension_semantics=("arbitrary",), vmem_limit_bytes=VMEM_LIMIT),
        name="moba_sample",
    )(page_table, aq, ak, av, bias, *([cache_kt] * n_pages), *([cache_vt] * n_pages))


def _out_proj_body(*refs, d_lin, gated):
    if gated:
        ol_ref, oa_ref, x_ref, w_ref, gpost_ref, y_ref = refs
        oa = oa_ref[...]
    else:
        ol_ref, oa_ref, ga_ref, x_ref, w_ref, gpost_ref, y_ref = refs
        oa = oa_ref[...] * ga_ref[...]
    o = jnp.dot(ol_ref[...].astype(BF16), w_ref[0:d_lin, :], preferred_element_type=F32)
    o = o + jnp.dot(oa.astype(BF16), w_ref[d_lin:, :], preferred_element_type=F32)
    y_ref[...] = x_ref[...] + _rms(o, gpost_ref[...])


def _out_proj(o_lin, o_att, ga, x, w_bf, gpost, *, tm=256):
    rows, d_model = x.shape
    tm = min(tm, rows)
    assert rows % tm == 0
    d_lin = o_lin.shape[1]
    row_spec = lambda w: pl.BlockSpec((tm, w), lambda i: (i, 0))
    acts = [o_lin, o_att] + ([] if ga is None else [ga])
    return pl.pallas_call(
        functools.partial(_out_proj_body, d_lin=d_lin, gated=ga is None),
        grid=(rows // tm,),
        in_specs=[row_spec(a.shape[1]) for a in acts] + [
            row_spec(d_model),
            pl.BlockSpec(w_bf.shape, lambda i: (0, 0)),
            pl.BlockSpec((1, d_model), lambda i: (0, 0)),
        ],
        out_specs=row_spec(d_model),
        out_shape=jax.ShapeDtypeStruct((rows, d_model), F32),
        compiler_params=pltpu.CompilerParams(dimension_semantics=("arbitrary",)),
        name="out_proj",
    )(*acts, x, w_bf, gpost)


def kernel(x_prompt, x_sample, cache_k, cache_v, state_hgrn, page_table, w_in, w_out,
           norm_pre, norm_post, norm_lin_out, lin_lower_bound, rel_bias):
    depth = w_in.shape[0]
    assert depth == 1 and lin_lower_bound.shape[0] == depth + 1
    b, t, d_model = x_prompt.shape
    nb, ts, _ = x_sample.shape
    d_lin = lin_lower_bound.shape[1]
    d_att = w_out.shape[1] - d_lin
    lin_heads = d_lin // LIN_HEAD_DIM
    att_heads = rel_bias.shape[1]
    n_pages = page_table.shape[1]
    page = cache_k.shape[2]
    past_len = n_pages * page
    assert d_att == att_heads * ATT_HEAD_DIM and t % MOBA_BLOCK == 0
    assert past_len % MOBA_BLOCK == 0 and ts <= SUBLANES

    w_in_bf = w_in[0].astype(BF16)
    w_out_bf = w_out[0].astype(BF16)
    gpre, gpost, glin = norm_pre, norm_post, norm_lin_out
    proj = functools.partial(_in_proj, gpre=gpre, w_bf=w_in_bf, llb=lin_lower_bound,
                             d_lin=d_lin, d_att=d_att)

    xp = x_prompt.reshape(b * t, d_model)
    q, g, kk, v, gl, ga, ak, aqt, akt, avt = proj(xp, seq_len=t)
    q, g, kk, v, gl, ak = [a.reshape(b, t, -1) for a in (q, g, kk, v, gl, ak)]
    s0 = jnp.zeros((b, lin_heads, LIN_HEAD_DIM, LIN_HEAD_DIM), F32)
    o_lin, s_prompt = _hgrn_tiles(q, g, kk, v, gl, s0, glin, chunk=32, t_tile=512)
    keep = _moba_gate(aqt, ak)
    n_bias = 6
    assert (n_bias - 1) * MOBA_BLOCK - (MOBA_BLOCK - 1) >= MAX_DISTANCE
    bias = _prompt_bias(rel_bias, n_bias)
    o_att = _moba_prompt(aqt, ak, avt, keep, bias, ga)
    y_prompt = _out_proj(o_lin.reshape(b * t, d_lin), o_att.reshape(b * t, d_att),
                         None, xp, w_out_bf, gpost).reshape(b, t, d_model)

    rows = SUBLANES
    xs = jnp.pad(x_sample, ((0, 0), (0, rows - ts), (0, 0))).reshape(nb * rows, d_model)
    qs, gs, kks, vs, gls, gas, aqs, aks, avs = [a.reshape(nb, rows, -1) for a in proj(xs)]
    o_lin_s, s_sample = _hgrn(qs, gs, kks, vs, gls, state_hgrn[0], glin,
                              chunk=rows, valid=ts, t_tile=rows, n_seq=8)
    bias_s = _sample_bias(rel_bias, past_len=past_len, n_valid=ts, rows=rows)
    cache_kt = cache_k.transpose(0, 1, 3, 4, 2)
    cache_vt = cache_v.transpose(0, 1, 3, 4, 2)
    o_att_s = _moba_sample(aqs, aks, avs, cache_kt, cache_vt, page_table, bias_s, heads=att_heads)
    y_s = _out_proj(o_lin_s.reshape(nb * rows, d_lin), o_att_s.reshape(nb * rows, d_att),
                    gas.reshape(nb * rows, d_att), xs, w_out_bf, gpost)
    y_sample = y_s.reshape(nb, rows, d_model)[:, :ts]

    def prompt_kv(a):
        return a.reshape(1, b, att_heads, ATT_HEAD_DIM, t).transpose(0, 1, 4, 2, 3)

    kvs_shape = (1, nb, ts, att_heads, ATT_HEAD_DIM)
    return (y_prompt, y_sample, prompt_kv(akt), prompt_kv(avt), s_prompt[None],
            aks[:, :ts].reshape(kvs_shape), avs[:, :ts].reshape(kvs_shape), s_sample[None])
```

```python
import functools
import math

import jax
import jax.numpy as jnp
import numpy as np
from jax import lax
from jax.experimental import pallas as pl
from jax.experimental.pallas import tpu as pltpu

F32 = jnp.float32
BF16 = jnp.bfloat16

EPS = 1e-6
LIN_HEAD_DIM = 128
ATT_HEAD_DIM = 64
MOBA_BLOCK = 256
MOBA_TOPK = 3
N_BUCKETS = 32
MAX_DISTANCE = 1024
NEG = -2e30
M_INIT = -1e30
LOG2E = 1.4426950408889634
LANES = 128
SUBLANES = 8
VMEM_LIMIT = 56 * 1024 * 1024
HEADS_PER_TILE = LANES // ATT_HEAD_DIM

NT_DIMS = (((1,), (1,)), ((), ()))
TN_DIMS = (((0,), (0,)), ((), ()))


def _sigmoid(x):
    return 1.0 / (1.0 + jnp.exp(-x))


def _silu(x):
    return x * _sigmoid(x)


def _rms(x, g):
    return x * lax.rsqrt(jnp.mean(x * x, axis=-1, keepdims=True) + EPS) * g


def _in_head(idx, head):
    return (idx >= head * ATT_HEAD_DIM) & (idx < (head + 1) * ATT_HEAD_DIM)


def _in_proj_body(x_ref, gpre_ref, w_ref, llb_ref, *rest, d_lin, d_att, feat_major):
    if feat_major:
        wt_ref, q_o, g_o, kk_o, v_o, gl_o, ga_o, ak_o, aqt_o, akt_o, avt_o = rest
    else:
        q_o, g_o, kk_o, v_o, gl_o, ga_o, aq_o, ak_o, av_o = rest
    h = _rms(x_ref[...], gpre_ref[...]).astype(BF16)

    def proj(col, width):
        return jnp.dot(h, w_ref[:, col:col + width], preferred_element_type=F32)

    def proj_t(row, width):
        return lax.dot_general(wt_ref[row:row + width, :], h, NT_DIMS, preferred_element_type=F32)

    llb = llb_ref[...]
    e = jnp.exp(llb - jnp.max(llb, axis=0, keepdims=True))
    lb = e[0:1, :] / jnp.sum(e, axis=0, keepdims=True)

    q_o[...] = _silu(proj(0, d_lin))
    f = lb + (1.0 - lb) * _sigmoid(proj(d_lin, d_lin))
    g_o[...] = jnp.log(f)
    kk_o[...] = 1.0 - f
    v_o[...] = proj(2 * d_lin, d_lin)
    gl_o[...] = _silu(proj(3 * d_lin, d_lin))
    a0 = 4 * d_lin
    ga_o[...] = _silu(proj(a0 + 3 * d_att, d_att))
    if feat_major:
        aqt_o[0] = proj_t(0, d_att)
        akt = proj_t(d_att, d_att)
        akt_o[0] = akt
        ak_o[...] = akt.T
        avt_o[0] = proj_t(2 * d_att, d_att)
    else:
        aq_o[...] = proj(a0, d_att)
        ak_o[...] = proj(a0 + d_att, d_att)
        av_o[...] = proj(a0 + 2 * d_att, d_att)


def _in_proj(x, gpre, w_bf, llb, *, d_lin, d_att, seq_len=None, tm=256):
    rows, d_model = x.shape
    tm = min(tm, rows)
    assert rows % tm == 0
    d_in = w_bf.shape[1]
    feat_major = seq_len is not None
    row_spec = lambda w: pl.BlockSpec((tm, w), lambda i: (i, 0))
    in_specs = [
        pl.BlockSpec((tm, d_model), lambda i: (i, 0)),
        pl.BlockSpec((1, d_model), lambda i: (0, 0)),
        pl.BlockSpec((d_model, d_in), lambda i: (0, 0)),
        pl.BlockSpec(llb.shape, lambda i: (0, 0)),
    ]
    args = [x, gpre, w_bf, llb]
    out_specs = [row_spec(d_lin)] * 5 + [row_spec(d_att)]
    out_shape = [jax.ShapeDtypeStruct((rows, d_lin), F32)] * 5 + [jax.ShapeDtypeStruct((rows, d_att), F32)]
    if feat_major:
        assert seq_len % tm == 0
        tiles = seq_len // tm
        a0 = 4 * d_lin
        wt_bf = w_bf[:, a0:a0 + 3 * d_att].T
        in_specs.append(pl.BlockSpec(wt_bf.shape, lambda i: (0, 0)))
        args.append(wt_bf)
        t_spec = pl.BlockSpec((1, d_att, tm), lambda i: (i // tiles, 0, i % tiles))
        t_shape = jax.ShapeDtypeStruct((rows // seq_len, d_att, seq_len), F32)
        out_specs += [row_spec(d_att)] + [t_spec] * 3
        out_shape += [jax.ShapeDtypeStruct((rows, d_att), F32)] + [t_shape] * 3
    else:
        out_specs += [row_spec(d_att)] * 3
        out_shape += [jax.ShapeDtypeStruct((rows, d_att), F32)] * 3
    return pl.pallas_call(
        functools.partial(_in_proj_body, d_lin=d_lin, d_att=d_att, feat_major=feat_major),
        grid=(rows // tm,),
        in_specs=in_specs,
        out_specs=out_specs,
        out_shape=out_shape,
        compiler_params=pltpu.CompilerParams(
            dimension_semantics=("arbitrary",), vmem_limit_bytes=VMEM_LIMIT),
        name="in_proj",
    )(*args)


def _hgrn_body(q_ref, g_ref, kk_ref, v_ref, gl_ref, s0_ref, glin_ref,
               o_ref, sout_ref, st_scr, b_scr, *, chunk, valid, n_chunks, n_seq):
    it = pl.program_id(2)

    @pl.when(it == 0)
    def _():
        for s in range(n_seq):
            st_scr[s] = s0_ref[s, 0].T

    row = lax.broadcasted_iota(jnp.int32, (chunk, 1), 0)

    for s in range(n_seq):
        def chunk_step(ci, carry, s=s):
            r0 = pl.multiple_of(ci * chunk, chunk)
            acc = jnp.zeros((1, LIN_HEAD_DIM), F32)
            for r in range(chunk):
                if r < valid:
                    acc = acc + g_ref[s, pl.ds(r0 + r, 1), :]
                b_scr[pl.ds(r, 1), :] = acc
            b = b_scr[...]
            b_last = b_scr[pl.ds(chunk - 1, 1), :]
            q = q_ref[s, pl.ds(r0, chunk), :]
            kk = kk_ref[s, pl.ds(r0, chunk), :]
            v = v_ref[s, pl.ds(r0, chunk), :]
            if valid < chunk:
                kk = jnp.where(row < valid, kk, 0.0)
            st = st_scr[s]
            o = lax.dot_general((q * jnp.exp(b)).astype(BF16), st.astype(BF16), NT_DIMS,
                                preferred_element_type=F32)
            for j in range(valid):
                bj = b_scr[pl.ds(j, 1), :]
                kj = kk_ref[s, pl.ds(r0 + j, 1), :]
                vj = v_ref[s, pl.ds(r0 + j, 1), :]
                w = q * kj * jnp.exp(jnp.minimum(b - bj, 0.0))
                a = jnp.sum(w, axis=-1, keepdims=True)
                o = o + jnp.where(row >= j, a, 0.0) * vj
            kt = kk * jnp.exp(b_last - b)
            st_scr[s] = st * jnp.exp(b_last) + lax.dot_general(
                v.astype(BF16), kt.astype(BF16), TN_DIMS, preferred_element_type=F32)
            o_ref[s, pl.ds(r0, chunk), :] = _rms(o, glin_ref[...]) * gl_ref[s, pl.ds(r0, chunk), :]
            return carry

        lax.fori_loop(0, n_chunks, chunk_step, 0)

    @pl.when(it == pl.num_programs(2) - 1)
    def _():
        for s in range(n_seq):
            sout_ref[s, 0] = st_scr[s].T


def _hgrn(q, g, kk, v, gl, s0, glin, *, chunk, valid, t_tile, n_seq):
    nb, t, d_lin = q.shape
    heads = d_lin // LIN_HEAD_DIM
    seq_spec = pl.BlockSpec((n_seq, t_tile, LIN_HEAD_DIM), lambda b, h, i: (b, i, h))
    st_spec = pl.BlockSpec((n_seq, 1, LIN_HEAD_DIM, LIN_HEAD_DIM), lambda b, h, i: (b, h, 0, 0))
    return pl.pallas_call(
        functools.partial(_hgrn_body, chunk=chunk, valid=valid,
                          n_chunks=t_tile // chunk, n_seq=n_seq),
        grid=(nb // n_seq, heads, t // t_tile),
        in_specs=[seq_spec] * 5 + [st_spec, pl.BlockSpec((1, LIN_HEAD_DIM), lambda b, h, i: (0, 0))],
        out_specs=[seq_spec, st_spec],
        out_shape=[jax.ShapeDtypeStruct((nb, t, d_lin), F32),
                   jax.ShapeDtypeStruct(s0.shape, F32)],
        scratch_shapes=[pltpu.VMEM((n_seq, LIN_HEAD_DIM, LIN_HEAD_DIM), F32),
                        pltpu.VMEM((chunk, LIN_HEAD_DIM), F32)],
        compiler_params=pltpu.CompilerParams(
            dimension_semantics=("arbitrary", "arbitrary", "arbitrary")),
        name="hgrn",
    )(q, g, kk, v, gl, s0, glin)


SAFE_DECAY = 60.0


def _hgrn_tile_body(q_ref, g_ref, kk_ref, v_ref, gl_ref, s0_ref, glin_ref, o_ref, sout_ref,
                    st_scr, b_scr, oi_scr, u_scr, kkp_scr, bp_scr, vp_scr, *, chunk, t_tile):
    it = pl.program_id(2)

    @pl.when(it == 0)
    def _():
        st_scr[...] = s0_ref[0, 0].T

    lg = int(math.log2(chunk))
    n_groups = t_tile // LANES
    r = lax.broadcasted_iota(jnp.int32, (LANES, LANES), 0)
    c = lax.broadcasted_iota(jnp.int32, (LANES, LANES), 1)
    causal = ((r >> lg) == (c >> lg)) & (c <= r)
    tri = jnp.where(causal, 1.0, 0.0).astype(BF16)
    n_pieces = 3
    pieces = []
    for gi in range(n_groups):
        rest = g_ref[0, gi * LANES:(gi + 1) * LANES, :]
        for _ in range(n_pieces):
            piece = rest.astype(BF16)
            rest = rest - piece.astype(F32)
            pieces.append(piece)
    sums = jnp.dot(tri, jnp.concatenate(pieces, axis=1), preferred_element_type=F32)
    for gi in range(n_groups):
        b = jnp.zeros((LANES, LIN_HEAD_DIM), F32)
        for pi in range(n_pieces):
            col = (gi * n_pieces + pi) * LIN_HEAD_DIM
            b = b + sums[:, col:col + LIN_HEAD_DIM]
        b_scr[gi * LANES:(gi + 1) * LANES, :] = b

    def intra_mxu():
        atts = []
        for gi in range(n_groups):
            rows = slice(gi * LANES, (gi + 1) * LANES)
            b = b_scr[rows, :]
            qe = (q_ref[0, rows, :] * jnp.exp(b)).astype(BF16)
            ke = (kk_ref[0, rows, :] * jnp.exp(-b)).astype(BF16)
            att = lax.dot_general(qe, ke, NT_DIMS, preferred_element_type=F32)
            atts.append(jnp.where(causal, att, 0.0).astype(BF16))
        for gi in range(n_groups):
            rows = slice(gi * LANES, (gi + 1) * LANES)
            oi_scr[rows, :] = jnp.dot(atts[gi], v_ref[0, rows, :].astype(BF16), preferred_element_type=F32)

    def intra_pairs():
        zeros = jnp.zeros((chunk, LIN_HEAD_DIM), F32)
        for dst, src in ((kkp_scr, kk_ref[0]), (bp_scr, b_scr[...]), (vp_scr, v_ref[0])):
            dst[0:chunk, :] = zeros
            dst[chunk:, :] = src
        oi_scr[...] = jnp.zeros(oi_scr.shape, F32)
        q = q_ref[0]
        b = b_scr[...]
        tmod = lax.broadcasted_iota(jnp.int32, (t_tile, 1), 0) & (chunk - 1)

        def lag(d, carry):
            start = chunk - d
            w = q * kkp_scr[pl.ds(start, t_tile), :] * jnp.exp(
                jnp.minimum(b - bp_scr[pl.ds(start, t_tile), :], 0.0))
            a = jnp.sum(w, axis=-1, keepdims=True)
            oi_scr[...] += jnp.where(tmod >= d, a, 0.0) * vp_scr[pl.ds(start, t_tile), :]
            return carry

        lax.fori_loop(0, chunk, lag, 0)

    lax.cond(jnp.min(b_scr[...]) >= -SAFE_DECAY, intra_mxu, intra_pairs)

    n_chunks = t_tile // chunk
    for n in range(n_chunks):
        rows = slice(n * chunk, (n + 1) * chunk)
        b_last = b_scr[(n + 1) * chunk - 1:(n + 1) * chunk, :]
        kt = (kk_ref[0, rows, :] * jnp.exp(b_last - b_scr[rows, :])).astype(BF16)
        u_scr[n] = lax.dot_general(v_ref[0, rows, :].astype(BF16), kt, TN_DIMS, preferred_element_type=F32)

    st = st_scr[...]
    for n in range(n_chunks):
        rows = slice(n * chunk, (n + 1) * chunk)
        b_last = b_scr[(n + 1) * chunk - 1:(n + 1) * chunk, :]
        o = lax.dot_general((q_ref[0, rows, :] * jnp.exp(b_scr[rows, :])).astype(BF16), st.astype(BF16),
                            NT_DIMS, preferred_element_type=F32) + oi_scr[rows, :]
        st = st * jnp.exp(b_last) + u_scr[n]
        o_ref[0, rows, :] = (_rms(o, glin_ref[...]) * gl_ref[0, rows, :]).astype(o_ref.dtype)
    st_scr[...] = st

    @pl.when(it == pl.num_programs(2) - 1)
    def _():
        sout_ref[0, 0] = st.T


def _hgrn_tiles(q, g, kk, v, gl, s0, glin, *, chunk, t_tile):
    nb, t, d_lin = q.shape
    heads = d_lin // LIN_HEAD_DIM
    assert LANES % chunk == 0 and t_tile % LANES == 0 and t % t_tile == 0
    seq_spec = pl.BlockSpec((1, t_tile, LIN_HEAD_DIM), lambda b, h, i: (b, i, h))
    st_spec = pl.BlockSpec((1, 1, LIN_HEAD_DIM, LIN_HEAD_DIM), lambda b, h, i: (b, h, 0, 0))
    tile = pltpu.VMEM((t_tile, LIN_HEAD_DIM), F32)
    halo = pltpu.VMEM((chunk + t_tile, LIN_HEAD_DIM), F32)
    return pl.pallas_call(
        functools.partial(_hgrn_tile_body, chunk=chunk, t_tile=t_tile),
        grid=(nb, heads, t // t_tile),
        in_specs=[seq_spec] * 5 + [st_spec, pl.BlockSpec((1, LIN_HEAD_DIM), lambda b, h, i: (0, 0))],
        out_specs=[seq_spec, st_spec],
        out_shape=[jax.ShapeDtypeStruct((nb, t, d_lin), BF16),
                   jax.ShapeDtypeStruct(s0.shape, F32)],
        scratch_shapes=[pltpu.VMEM((LIN_HEAD_DIM, LIN_HEAD_DIM), F32), tile, tile,
                        pltpu.VMEM((t_tile // chunk, LIN_HEAD_DIM, LIN_HEAD_DIM), F32), halo, halo, halo],
        compiler_params=pltpu.CompilerParams(
            dimension_semantics=("arbitrary", "arbitrary", "arbitrary")),
        name="hgrn_tiles",
    )(q, g, kk, v, gl, s0, glin)


def _t5_first_distances():
    max_exact = N_BUCKETS // 2
    n = np.arange(0, MAX_DISTANCE + 1, dtype=np.int32)
    ratio = np.maximum(n, 1).astype(np.float32) / np.float32(max_exact)
    large = max_exact + (np.log(ratio) / np.float32(math.log(MAX_DISTANCE / max_exact))
                         * np.float32(N_BUCKETS - max_exact)).astype(np.int32)
    bucket = np.where(n < max_exact, n, np.minimum(large, N_BUCKETS - 1))
    assert np.all(np.diff(bucket) >= 0) and bucket[-1] == N_BUCKETS - 1
    return [int(np.argmax(bucket >= bk)) for bk in range(N_BUCKETS)]


_T5_FIRST_DISTANCE = _t5_first_distances()


def _t5_bias(dist, rb_ref, head):
    n = jnp.maximum(dist, 0)
    out = jnp.full(dist.shape, rb_ref[0, head], F32)
    for bk in range(1, N_BUCKETS):
        out = jnp.where(n >= _T5_FIRST_DISTANCE[bk], rb_ref[bk, head], out)
    return out


def _prompt_bias_body(rb_ref, o_ref):
    h = pl.program_id(0)
    d = pl.program_id(1)
    x = lax.broadcasted_iota(jnp.int32, (SUBLANES, 2 * MOBA_BLOCK), 1)
    dist = (d - 1) * MOBA_BLOCK + x
    by_dist = jnp.where(dist >= 0, _t5_bias(dist, rb_ref, h) * LOG2E, NEG)
    wide = jnp.broadcast_to(by_dist[0:1, :], (MOBA_BLOCK, 2 * MOBA_BLOCK))
    o_ref[0, 0] = pltpu.roll(wide, 0, 1, stride=1, stride_axis=0)[:, MOBA_BLOCK:]


def _prompt_bias(rel_bias, n_tiles):
    heads = rel_bias.shape[1]
    return pl.pallas_call(
        _prompt_bias_body,
        grid=(heads, n_tiles),
        in_specs=[pl.BlockSpec(memory_space=pltpu.SMEM)],
        out_specs=pl.BlockSpec((1, 1, MOBA_BLOCK, MOBA_BLOCK), lambda h, d: (h, d, 0, 0)),
        out_shape=jax.ShapeDtypeStruct((heads, n_tiles, MOBA_BLOCK, MOBA_BLOCK), F32),
        name="prompt_bias",
    )(rel_bias)


def _sample_bias_body(rb_ref, o_ref, *, past_len, n_valid, rows):
    h = pl.program_id(0)
    width = past_len + LANES
    r = lax.broadcasted_iota(jnp.int32, (rows, width), 0)
    kpos = lax.broadcasted_iota(jnp.int32, (rows, width), 1)
    dist = past_len + r - kpos
    ok = (dist >= 0) & (kpos < past_len + n_valid)
    o_ref[...] = jnp.where(ok, _t5_bias(dist, rb_ref, h), NEG)


def _sample_bias(rel_bias, *, past_len, n_valid, rows):
    heads = rel_bias.shape[1]
    width = past_len + LANES
    return pl.pallas_call(
        functools.partial(_sample_bias_body, past_len=past_len, n_valid=n_valid, rows=rows),
        grid=(heads,),
        in_specs=[pl.BlockSpec(memory_space=pltpu.SMEM)],
        out_specs=pl.BlockSpec((rows, width), lambda h: (h, 0)),
        out_shape=jax.ShapeDtypeStruct((heads * rows, width), F32),
        name="sample_bias",
    )(rel_bias)


def _topk_keep(gate, idx, own, axis, n):
    past = idx < own
    gm = jnp.where(past, gate, -jnp.inf)
    rank = jnp.zeros(gate.shape, jnp.int32)
    for jp in range(n):
        gj = lax.slice_in_dim(gm, jp, jp + 1, axis=axis)
        beats = (gj > gm) | ((gj == gm) & (jp < idx))
        rank = rank + beats.astype(jnp.int32)
    keep = ((rank < MOBA_TOPK) & past) | (idx == own)
    return jnp.where(keep, 0.0, NEG)


def _moba_gate_body(qt_ref, k_ref, a_ref, km_scr, *, n_blocks):
    t = n_blocks * MOBA_BLOCK
    qt = qt_ref[0]
    lane = lax.broadcasted_iota(jnp.int32, (1, LANES), 1)
    for n in range(n_blocks):
        km_scr[pl.ds(n, 1), :] = jnp.sum(
            k_ref[0, n * MOBA_BLOCK:(n + 1) * MOBA_BLOCK, :], axis=0, keepdims=True) * (1.0 / MOBA_BLOCK)
    km = km_scr[...]
    blk = lax.broadcasted_iota(jnp.int32, (n_blocks, t), 0)
    own = lax.broadcasted_iota(jnp.int32, (n_blocks, t), 1) >> int(math.log2(MOBA_BLOCK))
    for hh in range(HEADS_PER_TILE):
        kmh = jnp.where(_in_head(lane, hh), km, 0.0)
        gate_t = jnp.dot(kmh, qt, precision=lax.Precision.HIGHEST, preferred_element_type=F32)
        a_ref[0, 0, hh * n_blocks:(hh + 1) * n_blocks, :] = _topk_keep(gate_t, blk, own, axis=0, n=n_blocks)


def _moba_gate(aqt, ak):
    b, d_att, t = aqt.shape
    tiles = d_att // LANES
    n_blocks = t // MOBA_BLOCK
    return pl.pallas_call(
        functools.partial(_moba_gate_body, n_blocks=n_blocks),
        grid=(b, tiles),
        in_specs=[pl.BlockSpec((1, LANES, t), lambda i, p: (i, p, 0)),
                  pl.BlockSpec((1, t, LANES), lambda i, p: (i, 0, p))],
        out_specs=pl.BlockSpec((1, 1, HEADS_PER_TILE * n_blocks, t), lambda i, p: (i, p, 0, 0)),
        out_shape=jax.ShapeDtypeStruct((b, tiles, HEADS_PER_TILE * n_blocks, t), F32),
        scratch_shapes=[pltpu.VMEM((n_blocks, LANES), F32)],
        compiler_params=pltpu.CompilerParams(
            dimension_semantics=("arbitrary", "arbitrary"), vmem_limit_bytes=VMEM_LIMIT),
        name="moba_gate",
    )(aqt, ak)


def _moba_prompt_body(qt_ref, k_ref, vt_ref, keep_ref, bias_ref, ga_ref, o_ref, kaug_scr, vtaug_scr, s_scr,
                      *, n_bias, n_blocks, group):
    i = pl.program_id(2)
    rows = group * MOBA_BLOCK
    t = n_blocks * MOBA_BLOCK
    lg_block = int(math.log2(MOBA_BLOCK))
    feat = lax.broadcasted_iota(jnp.int32, (LANES, 1), 0)
    ones_row = [((hh + 1) % HEADS_PER_TILE) * ATT_HEAD_DIM for hh in range(HEADS_PER_TILE)]

    @pl.when(i == 0)
    def _():
        blk = lax.broadcasted_iota(jnp.int32, (rows, LANES), 0) >> lg_block
        lane = lax.broadcasted_iota(jnp.int32, (rows, LANES), 1)
        for c in range(t // rows):
            sl = slice(c * rows, (c + 1) * rows)
            for hh in range(HEADS_PER_TILE):
                onehot = jnp.where(lane == hh * n_blocks + c * group + blk, 1.0, 0.0)
                kaug_scr[hh, sl, :] = jnp.concatenate([k_ref[0, sl, :], onehot], axis=1).astype(BF16)
                vtaug_scr[hh, :, sl] = jnp.where(feat == ones_row[hh], 1.0, vt_ref[0, :, sl]).astype(BF16)

    qt = qt_ref[0] * (ATT_HEAD_DIM ** -0.5 * LOG2E)
    keep = keep_ref[0, 0]
    pad = jnp.zeros((LANES - keep.shape[0], MOBA_BLOCK), F32)
    q_aug = [jnp.concatenate([jnp.where(_in_head(feat, hh), qt, 0.0), keep, pad], axis=0).astype(BF16)
             for hh in range(HEADS_PER_TILE)]
    n_groups = (i >> int(math.log2(group))) + 1

    def scores(gi, m):
        r0 = pl.multiple_of(gi * rows, rows)
        m_new = []
        for hh in range(HEADS_PER_TILE):
            s = jnp.dot(kaug_scr[hh, pl.ds(r0, rows), :], q_aug[hh], preferred_element_type=F32)
            mh = m[hh]
            for u in range(group):
                d = jnp.clip(i - (gi * group + u), 0, n_bias - 1)
                su = s[u * MOBA_BLOCK:(u + 1) * MOBA_BLOCK] + bias_ref[hh, d]
                s_scr[hh, pl.ds(r0 + u * MOBA_BLOCK, MOBA_BLOCK), :] = su
                mh = jnp.maximum(mh, jnp.max(su, axis=0, keepdims=True))
            m_new.append(mh)
        return tuple(m_new)

    def values(gi, m, m_before, acc):
        r0 = pl.multiple_of(gi * rows, rows)
        out = []
        for hh in range(HEADS_PER_TILE):
            p = jnp.exp2(s_scr[hh, pl.ds(r0, rows), :] - m[hh]).astype(BF16)
            out.append(acc[hh] * jnp.exp2(m_before[hh] - m[hh])
                       + jnp.dot(vtaug_scr[hh, :, pl.ds(r0, rows)], p, preferred_element_type=F32))
        return tuple(out)

    m_init = tuple(jnp.full((1, MOBA_BLOCK), M_INIT, F32) for _ in range(HEADS_PER_TILE))
    acc0 = tuple(jnp.zeros((LANES, MOBA_BLOCK), F32) for _ in range(HEADS_PER_TILE))

    def stage(gi, carry):
        m, m_before, acc = carry
        acc = values(gi - 1, m, m_before, acc)
        return scores(gi, m), m, acc

    m, m_before, acc = lax.fori_loop(1, n_groups, stage, (scores(0, m_init), m_init, acc0))
    acc = values(n_groups - 1, m, m_before, acc)
    out = jnp.zeros((LANES, MOBA_BLOCK), F32)
    for hh in range(HEADS_PER_TILE):
        out = jnp.where(_in_head(feat, hh), acc[hh] / acc[hh][ones_row[hh]:ones_row[hh] + 1, :], out)
    o_ref[0] = (out.T * ga_ref[0]).astype(o_ref.dtype)


def _moba_prompt(aqt, ak, avt, keep, bias, ga, *, group=4):
    b, d_att, t = aqt.shape
    tiles = d_att // LANES
    n_blocks = t // MOBA_BLOCK
    n_bias = bias.shape[1]
    group = min(group, n_blocks)
    assert n_blocks % group == 0 and group & (group - 1) == 0
    return pl.pallas_call(
        functools.partial(_moba_prompt_body, n_bias=n_bias, n_blocks=n_blocks, group=group),
        grid=(b, tiles, n_blocks),
        in_specs=[pl.BlockSpec((1, LANES, MOBA_BLOCK), lambda ib, p, i: (ib, p, i)),
                  pl.BlockSpec((1, t, LANES), lambda ib, p, i: (ib, 0, p)),
                  pl.BlockSpec((1, LANES, t), lambda ib, p, i: (ib, p, 0)),
                  pl.BlockSpec((1, 1, keep.shape[2], MOBA_BLOCK), lambda ib, p, i: (ib, p, 0, i)),
                  pl.BlockSpec((HEADS_PER_TILE, n_bias, MOBA_BLOCK, MOBA_BLOCK), lambda ib, p, i: (p, 0, 0, 0)),
                  pl.BlockSpec((1, MOBA_BLOCK, LANES), lambda ib, p, i: (ib, i, p))],
        out_specs=pl.BlockSpec((1, MOBA_BLOCK, LANES), lambda ib, p, i: (ib, i, p)),
        out_shape=jax.ShapeDtypeStruct((b, t, d_att), BF16),
        scratch_shapes=[pltpu.VMEM((HEADS_PER_TILE, t, 2 * LANES), BF16),
                        pltpu.VMEM((HEADS_PER_TILE, LANES, t), BF16),
                        pltpu.VMEM((HEADS_PER_TILE, t, MOBA_BLOCK), F32)],
        compiler_params=pltpu.CompilerParams(
            dimension_semantics=("arbitrary", "arbitrary", "arbitrary"), vmem_limit_bytes=VMEM_LIMIT),
        name="moba_prompt",
    )(aqt, ak, avt, keep, bias, ga.reshape(b, t, d_att))


def _moba_sample_body(pt_ref, q_ref, kn_ref, vn_ref, bias_ref, *rest, n_pages, heads, rows):
    k_pages = rest[:n_pages]
    v_pages = rest[n_pages:2 * n_pages]
    o_ref = rest[2 * n_pages]
    s_scr = rest[2 * n_pages + 1]
    d_att = heads * ATT_HEAD_DIM
    page = k_pages[0].shape[-1]
    pages_per_block = MOBA_BLOCK // page
    n_blocks = n_pages // pages_per_block
    n_q = heads * rows
    feat = lax.broadcasted_iota(jnp.int32, (1, d_att), 1)
    lane = lax.broadcasted_iota(jnp.int32, (n_q, LANES), 1)

    q = q_ref[0] * (ATT_HEAD_DIM ** -0.5)
    q_bd = jnp.concatenate([jnp.where(_in_head(feat, h), q, 0.0) for h in range(heads)], axis=0).astype(BF16)

    gate = jnp.zeros((n_q, LANES), F32)
    for n in range(n_blocks):
        tot = jnp.zeros((n_q, page), F32)
        for p in range(n * pages_per_block, (n + 1) * pages_per_block):
            kt = k_pages[p][...].reshape(d_att, page).astype(BF16)
            s = jnp.dot(q_bd, kt, preferred_element_type=F32)
            s_scr[:, p * page:(p + 1) * page] = s
            tot = tot + s
        gate = jnp.where(lane == n, jnp.sum(tot, axis=-1, keepdims=True) * (1.0 / MOBA_BLOCK), gate)
    keep = _topk_keep(gate, lane, jnp.full(gate.shape, n_blocks, jnp.int32), axis=1, n=n_blocks)

    def new_rows(ref):
        return jnp.concatenate([ref[0], jnp.zeros((LANES - rows, d_att), F32)], axis=0).astype(BF16)

    own = n_pages * page
    s_own = lax.dot_general(q_bd, new_rows(kn_ref), NT_DIMS, preferred_element_type=F32)
    s_own = s_own + bias_ref[:, own:own + LANES]
    s_scr[:, own:own + LANES] = s_own
    m_wide = s_own
    for p in range(n_pages):
        n = p // pages_per_block
        s = s_scr[:, p * page:(p + 1) * page] + bias_ref[:, p * page:(p + 1) * page] + keep[:, n:n + 1]
        s_scr[:, p * page:(p + 1) * page] = s
        m_wide = jnp.maximum(m_wide, s)
    m = jnp.max(m_wide, axis=-1, keepdims=True)

    pr = jnp.exp(s_scr[:, own:own + LANES] - m)
    l_wide = pr
    acc = jnp.dot(pr.astype(BF16), new_rows(vn_ref), preferred_element_type=F32)
    for p in range(n_pages):
        pr = jnp.exp(s_scr[:, p * page:(p + 1) * page] - m)
        l_wide = l_wide + pr
        vt = v_pages[p][...].reshape(d_att, page).astype(BF16)
        acc = acc + lax.dot_general(pr.astype(BF16), vt, NT_DIMS, preferred_element_type=F32)
    acc = acc / jnp.sum(l_wide, axis=-1, keepdims=True)
    out = jnp.zeros((rows, d_att), F32)
    for h in range(heads):
        out = jnp.where(_in_head(feat, h), acc[h * rows:(h + 1) * rows, :], out)
    o_ref[0] = out


def _moba_sample(aq, ak, av, cache_kt, cache_vt, page_table, bias, *, heads):
    nb, rows, d_att = aq.shape
    n_pages = page_table.shape[1]
    page = cache_kt.shape[-1]
    assert cache_kt.shape[2:] == (heads, ATT_HEAD_DIM, page) and MOBA_BLOCK % page == 0 and page == LANES
    tok = pl.BlockSpec((1, rows, d_att), lambda b, pt: (b, 0, 0))

    def page_spec(p):
        return pl.BlockSpec((None, None, heads, ATT_HEAD_DIM, page), lambda b, pt, p=p: (0, pt[b, p], 0, 0, 0))

    grid_spec = pltpu.PrefetchScalarGridSpec(
        num_scalar_prefetch=1,
        grid=(nb,),
        in_specs=[tok, tok, tok, pl.BlockSpec(bias.shape, lambda b, pt: (0, 0))]
                 + [page_spec(p) for p in range(n_pages)] * 2,
        out_specs=tok,
        scratch_shapes=[pltpu.VMEM(bias.shape, F32)],
    )
    return pl.pallas_call(
        functools.partial(_moba_sample_body, n_pages=n_pages, heads=heads, rows=rows),
        grid_spec=grid_spec,
        out_shape=jax.ShapeDtypeStruct((nb, rows, d_att), F32),
        compiler_params=pltpu.CompilerParams(
            dimension_semantics=("arbitrary",), vmem_limit_bytes=VMEM_LIMIT),
        name="moba_sample",
    )(page_table, aq, ak, av, bias, *([cache_kt] * n_pages), *([cache_vt] * n_pages))


def _out_proj_body(*refs, d_lin, gated):
    if gated:
        ol_ref, oa_ref, x_ref, w_ref, gpost_ref, y_ref = refs
        oa = oa_ref[...]
    else:
        ol_ref, oa_ref, ga_ref, x_ref, w_ref, gpost_ref, y_ref = refs
        oa = oa_ref[...] * ga_ref[...]
    o = jnp.dot(ol_ref[...].astype(BF16), w_ref[0:d_lin, :], preferred_element_type=F32)
    o = o + jnp.dot(oa.astype(BF16), w_ref[d_lin:, :], preferred_element_type=F32)
    y_ref[...] = x_ref[...] + _rms(o, gpost_ref[...])


def _out_proj(o_lin, o_att, ga, x, w_bf, gpost, *, tm=256):
    rows, d_model = x.shape
    tm = min(tm, rows)
    assert rows % tm == 0
    d_lin = o_lin.shape[1]
    row_spec = lambda w: pl.BlockSpec((tm, w), lambda i: (i, 0))
    acts = [o_lin, o_att] + ([] if ga is None else [ga])
    return pl.pallas_call(
        functools.partial(_out_proj_body, d_lin=d_lin, gated=ga is None),
        grid=(rows // tm,),
        in_specs=[row_spec(a.shape[1]) for a in acts] + [
            row_spec(d_model),
            pl.BlockSpec(w_bf.shape, lambda i: (0, 0)),
            pl.BlockSpec((1, d_model), lambda i: (0, 0)),
        ],
        out_specs=row_spec(d_model),
        out_shape=jax.ShapeDtypeStruct((rows, d_model), F32),
        compiler_params=pltpu.CompilerParams(dimension_semantics=("arbitrary",)),
        name="out_proj",
    )(*acts, x, w_bf, gpost)


def kernel(x_prompt, x_sample, cache_k, cache_v, state_hgrn, page_table, w_in, w_out,
           norm_pre, norm_post, norm_lin_out, lin_lower_bound, rel_bias):
    depth = w_in.shape[0]
    assert depth == 1 and lin_lower_bound.shape[0] == depth + 1
    b, t, d_model = x_prompt.shape
    nb, ts, _ = x_sample.shape
    d_lin = lin_lower_bound.shape[1]
    d_att = w_out.shape[1] - d_lin
    lin_heads = d_lin // LIN_HEAD_DIM
    att_heads = rel_bias.shape[1]
    n_pages = page_table.shape[1]
    page = cache_k.shape[2]
    past_len = n_pages * page
    assert d_att == att_heads * ATT_HEAD_DIM and t % MOBA_BLOCK == 0
    assert past_len % MOBA_BLOCK == 0 and ts <= SUBLANES

    w_in_bf = w_in[0].astype(BF16)
    w_out_bf = w_out[0].astype(BF16)
    gpre, gpost, glin = norm_pre, norm_post, norm_lin_out
    proj = functools.partial(_in_proj, gpre=gpre, w_bf=w_in_bf, llb=lin_lower_bound,
                             d_lin=d_lin, d_att=d_att)

    xp = x_prompt.reshape(b * t, d_model)
    q, g, kk, v, gl, ga, ak, aqt, akt, avt = proj(xp, seq_len=t)
    q, g, kk, v, gl, ak = [a.reshape(b, t, -1) for a in (q, g, kk, v, gl, ak)]
    s0 = jnp.zeros((b, lin_heads, LIN_HEAD_DIM, LIN_HEAD_DIM), F32)
    o_lin, s_prompt = _hgrn_tiles(q, g, kk, v, gl, s0, glin, chunk=32, t_tile=512)
    keep = _moba_gate(aqt, ak)
    n_bias = 6
    assert (n_bias - 1) * MOBA_BLOCK - (MOBA_BLOCK - 1) >= MAX_DISTANCE
    bias = _prompt_bias(rel_bias, n_bias)
    o_att = _moba_prompt(aqt, ak, avt, keep, bias, ga)
    y_prompt = _out_proj(o_lin.reshape(b * t, d_lin), o_att.reshape(b * t, d_att),
                         None, xp, w_out_bf, gpost, tm=512).reshape(b, t, d_model)

    rows = SUBLANES
    xs = jnp.pad(x_sample, ((0, 0), (0, rows - ts), (0, 0))).reshape(nb * rows, d_model)
    qs, gs, kks, vs, gls, gas, aqs, aks, avs = [a.reshape(nb, rows, -1) for a in proj(xs)]
    o_lin_s, s_sample = _hgrn(qs, gs, kks, vs, gls, state_hgrn[0], glin,
                              chunk=rows, valid=ts, t_tile=rows, n_seq=8)
    bias_s = _sample_bias(rel_bias, past_len=past_len, n_valid=ts, rows=rows)
    cache_kt = cache_k.transpose(0, 1, 3, 4, 2)
    cache_vt = cache_v.transpose(0, 1, 3, 4, 2)
    o_att_s = _moba_sample(aqs, aks, avs, cache_kt, cache_vt, page_table, bias_s, heads=att_heads)
    y_s = _out_proj(o_lin_s.reshape(nb * rows, d_lin), o_att_s.reshape(nb * rows, d_att),
                    gas.reshape(nb * rows, d_att), xs, w_out_bf, gpost)
    y_sample = y_s.reshape(nb, rows, d_model)[:, :ts]

    def prompt_kv(a):
        return a.reshape(1, b, att_heads, ATT_HEAD_DIM, t).transpose(0, 1, 4, 2, 3)

    kvs_shape = (1, nb, ts, att_heads, ATT_HEAD_DIM)
    return (y_prompt, y_sample, prompt_kv(akt), prompt_kv(avt), s_prompt[None],
            aks[:, :ts].reshape(kvs_shape), avs[:, :ts].reshape(kvs_shape), s_sample[None])
```

```python
import functools
import math

import jax
import jax.numpy as jnp
import numpy as np
from jax import lax
from jax.experimental import pallas as pl
from jax.experimental.pallas import tpu as pltpu

F32 = jnp.float32
BF16 = jnp.bfloat16

EPS = 1e-6
LIN_HEAD_DIM = 128
ATT_HEAD_DIM = 64
MOBA_BLOCK = 256
MOBA_TOPK = 3
N_BUCKETS = 32
MAX_DISTANCE = 1024
NEG = -2e30
M_INIT = -1e30
LOG2E = 1.4426950408889634
LANES = 128
SUBLANES = 8
VMEM_LIMIT = 56 * 1024 * 1024
HEADS_PER_TILE = LANES // ATT_HEAD_DIM

NT_DIMS = (((1,), (1,)), ((), ()))
TN_DIMS = (((0,), (0,)), ((), ()))


def _sigmoid(x):
    return 1.0 / (1.0 + jnp.exp(-x))


def _silu(x):
    return x * _sigmoid(x)


def _rms(x, g):
    return x * lax.rsqrt(jnp.mean(x * x, axis=-1, keepdims=True) + EPS) * g


def _in_head(idx, head):
    return (idx >= head * ATT_HEAD_DIM) & (idx < (head + 1) * ATT_HEAD_DIM)


def _in_proj_body(x_ref, gpre_ref, w_ref, llb_ref, *rest, d_lin, d_att, feat_major):
    if feat_major:
        wt_ref, q_o, g_o, kk_o, v_o, gl_o, ga_o, ak_o, aqt_o, akt_o, avt_o = rest
    else:
        q_o, g_o, kk_o, v_o, gl_o, ga_o, aq_o, ak_o, av_o = rest
    h = _rms(x_ref[...], gpre_ref[...]).astype(BF16)

    def proj(col, width):
        return jnp.dot(h, w_ref[:, col:col + width], preferred_element_type=F32)

    def proj_t(row, width):
        return lax.dot_general(wt_ref[row:row + width, :], h, NT_DIMS, preferred_element_type=F32)

    llb = llb_ref[...]
    e = jnp.exp(llb - jnp.max(llb, axis=0, keepdims=True))
    lb = e[0:1, :] / jnp.sum(e, axis=0, keepdims=True)

    q_o[...] = _silu(proj(0, d_lin))
    f = lb + (1.0 - lb) * _sigmoid(proj(d_lin, d_lin))
    g_o[...] = jnp.log(f)
    kk_o[...] = 1.0 - f
    v_o[...] = proj(2 * d_lin, d_lin)
    gl_o[...] = _silu(proj(3 * d_lin, d_lin))
    a0 = 4 * d_lin
    ga_o[...] = _silu(proj(a0 + 3 * d_att, d_att))
    if feat_major:
        aqt_o[0] = proj_t(0, d_att)
        akt = proj_t(d_att, d_att)
        akt_o[0] = akt
        ak_o[...] = akt.T
        avt_o[0] = proj_t(2 * d_att, d_att)
    else:
        aq_o[...] = proj(a0, d_att)
        ak_o[...] = proj(a0 + d_att, d_att)
        av_o[...] = proj(a0 + 2 * d_att, d_att)


def _in_proj(x, gpre, w_bf, llb, *, d_lin, d_att, seq_len=None, tm=256):
    rows, d_model = x.shape
    tm = min(tm, rows)
    assert rows % tm == 0
    d_in = w_bf.shape[1]
    feat_major = seq_len is not None
    row_spec = lambda w: pl.BlockSpec((tm, w), lambda i: (i, 0))
    in_specs = [
        pl.BlockSpec((tm, d_model), lambda i: (i, 0)),
        pl.BlockSpec((1, d_model), lambda i: (0, 0)),
        pl.BlockSpec((d_model, d_in), lambda i: (0, 0)),
        pl.BlockSpec(llb.shape, lambda i: (0, 0)),
    ]
    args = [x, gpre, w_bf, llb]
    out_specs = [row_spec(d_lin)] * 5 + [row_spec(d_att)]
    out_shape = [jax.ShapeDtypeStruct((rows, d_lin), F32)] * 5 + [jax.ShapeDtypeStruct((rows, d_att), F32)]
    if feat_major:
        assert seq_len % tm == 0
        tiles = seq_len // tm
        a0 = 4 * d_lin
        wt_bf = w_bf[:, a0:a0 + 3 * d_att].T
        in_specs.append(pl.BlockSpec(wt_bf.shape, lambda i: (0, 0)))
        args.append(wt_bf)
        t_spec = pl.BlockSpec((1, d_att, tm), lambda i: (i // tiles, 0, i % tiles))
        t_shape = jax.ShapeDtypeStruct((rows // seq_len, d_att, seq_len), F32)
        out_specs += [row_spec(d_att)] + [t_spec] * 3
        out_shape += [jax.ShapeDtypeStruct((rows, d_att), F32)] + [t_shape] * 3
    else:
        out_specs += [row_spec(d_att)] * 3
        out_shape += [jax.ShapeDtypeStruct((rows, d_att), F32)] * 3
    return pl.pallas_call(
        functools.partial(_in_proj_body, d_lin=d_lin, d_att=d_att, feat_major=feat_major),
        grid=(rows // tm,),
        in_specs=in_specs,
        out_specs=out_specs,
        out_shape=out_shape,
        compiler_params=pltpu.CompilerParams(
            dimension_semantics=("arbitrary",), vmem_limit_bytes=VMEM_LIMIT),
        name="in_proj",
    )(*args)


def _hgrn_step_body(q_ref, g_ref, kk_ref, v_ref, gl_ref, s0_ref, glin_ref, o_ref, sout_ref, b_scr,
                    *, rows, valid, n_seq):
    row = lax.broadcasted_iota(jnp.int32, (rows, 1), 0)
    for s in range(n_seq):
        acc = jnp.zeros((1, LIN_HEAD_DIM), F32)
        for r in range(rows):
            if r < valid:
                acc = acc + g_ref[s, r:r + 1, :]
            b_scr[s, r:r + 1, :] = acc
    seqs = range(n_seq)
    b = [b_scr[s] for s in seqs]
    b_last = [b_scr[s, rows - 1:rows, :] for s in seqs]
    q = [q_ref[s] for s in seqs]
    kk = [jnp.where(row < valid, kk_ref[s], 0.0) for s in seqs]

    o = [jnp.dot((q[s] * jnp.exp(b[s])).astype(BF16), s0_ref[s, 0].astype(BF16), preferred_element_type=F32)
         for s in seqs]
    for s in seqs:
        for j in range(valid):
            w = q[s] * kk_ref[s, j:j + 1, :] * jnp.exp(jnp.minimum(b[s] - b_scr[s, j:j + 1, :], 0.0))
            a = jnp.sum(w, axis=-1, keepdims=True)
            o[s] = o[s] + jnp.where(row >= j, a, 0.0) * v_ref[s, j:j + 1, :]
    u = [lax.dot_general((kk[s] * jnp.exp(b_last[s] - b[s])).astype(BF16), v_ref[s].astype(BF16), TN_DIMS,
                         preferred_element_type=F32) for s in seqs]
    ones = jnp.ones((rows, LIN_HEAD_DIM), BF16)
    decay = []
    for s in seqs:
        rest = jnp.exp(b_last[s])
        pieces = jnp.zeros((rows, LIN_HEAD_DIM), F32)
        for pi in range(3):
            piece = rest.astype(BF16).astype(F32)
            rest = rest - piece
            pieces = jnp.where(row == pi, piece, pieces)
        decay.append(lax.dot_general(pieces.astype(BF16), ones, TN_DIMS, preferred_element_type=F32))
    for s in seqs:
        sout_ref[s, 0] = s0_ref[s, 0] * decay[s] + u[s]
        o_ref[s] = _rms(o[s], glin_ref[...]) * gl_ref[s]


def _hgrn_step(q, g, kk, v, gl, s0, glin, *, valid, n_seq):
    nb, rows, d_lin = q.shape
    heads = d_lin // LIN_HEAD_DIM
    assert rows >= 3 and nb % n_seq == 0
    seq_spec = pl.BlockSpec((n_seq, rows, LIN_HEAD_DIM), lambda b, h: (b, 0, h))
    st_spec = pl.BlockSpec((n_seq, 1, LIN_HEAD_DIM, LIN_HEAD_DIM), lambda b, h: (b, h, 0, 0))
    return pl.pallas_call(
        functools.partial(_hgrn_step_body, rows=rows, valid=valid, n_seq=n_seq),
        grid=(nb // n_seq, heads),
        in_specs=[seq_spec] * 5 + [st_spec, pl.BlockSpec((1, LIN_HEAD_DIM), lambda b, h: (0, 0))],
        out_specs=[seq_spec, st_spec],
        out_shape=[jax.ShapeDtypeStruct((nb, rows, d_lin), F32),
                   jax.ShapeDtypeStruct(s0.shape, F32)],
        scratch_shapes=[pltpu.VMEM((n_seq, rows, LIN_HEAD_DIM), F32)],
        compiler_params=pltpu.CompilerParams(dimension_semantics=("arbitrary", "arbitrary")),
        name="hgrn_step",
    )(q, g, kk, v, gl, s0, glin)


SAFE_DECAY = 60.0


def _hgrn_tile_body(q_ref, g_ref, kk_ref, v_ref, gl_ref, s0_ref, glin_ref, o_ref, sout_ref,
                    st_scr, b_scr, oi_scr, u_scr, kkp_scr, bp_scr, vp_scr, *, chunk, t_tile, heads):
    it = pl.program_id(1)
    hs = [slice(h * LIN_HEAD_DIM, (h + 1) * LIN_HEAD_DIM) for h in range(heads)]

    @pl.when(it == 0)
    def _():
        for h in range(heads):
            st_scr[h] = s0_ref[0, h].T

    lg = int(math.log2(chunk))
    n_groups = t_tile // LANES
    n_chunks = t_tile // chunk
    r = lax.broadcasted_iota(jnp.int32, (LANES, LANES), 0)
    c = lax.broadcasted_iota(jnp.int32, (LANES, LANES), 1)
    causal = ((r >> lg) == (c >> lg)) & (c <= r)
    tri = jnp.where(causal, 1.0, 0.0).astype(BF16)
    n_pieces = 3
    for h in range(heads):
        pieces = []
        for gi in range(n_groups):
            rest = g_ref[0, gi * LANES:(gi + 1) * LANES, hs[h]]
            for _ in range(n_pieces):
                piece = rest.astype(BF16)
                rest = rest - piece.astype(F32)
                pieces.append(piece)
        sums = jnp.dot(tri, jnp.concatenate(pieces, axis=1), preferred_element_type=F32)
        for gi in range(n_groups):
            b = jnp.zeros((LANES, LIN_HEAD_DIM), F32)
            for pi in range(n_pieces):
                col = (gi * n_pieces + pi) * LIN_HEAD_DIM
                b = b + sums[:, col:col + LIN_HEAD_DIM]
            b_scr[h, gi * LANES:(gi + 1) * LANES, :] = b

    def intra_mxu():
        atts = {}
        for h in range(heads):
            for gi in range(n_groups):
                rows = slice(gi * LANES, (gi + 1) * LANES)
                b = b_scr[h, rows, :]
                qe = (q_ref[0, rows, hs[h]] * jnp.exp(b)).astype(BF16)
                ke = (kk_ref[0, rows, hs[h]] * jnp.exp(-b)).astype(BF16)
                att = lax.dot_general(qe, ke, NT_DIMS, preferred_element_type=F32)
                atts[h, gi] = jnp.where(causal, att, 0.0).astype(BF16)
        for h in range(heads):
            for gi in range(n_groups):
                rows = slice(gi * LANES, (gi + 1) * LANES)
                oi_scr[h, rows, :] = jnp.dot(atts[h, gi], v_ref[0, rows, hs[h]].astype(BF16),
                                             preferred_element_type=F32)

    def intra_pairs():
        zeros = jnp.zeros((chunk, LIN_HEAD_DIM), F32)
        tmod = lax.broadcasted_iota(jnp.int32, (t_tile, 1), 0) & (chunk - 1)
        for h in range(heads):
            for dst, src in ((kkp_scr, kk_ref[0, :, hs[h]]), (bp_scr, b_scr[h]), (vp_scr, v_ref[0, :, hs[h]])):
                dst[0:chunk, :] = zeros
                dst[chunk:, :] = src
            oi_scr[h] = jnp.zeros((t_tile, LIN_HEAD_DIM), F32)
            q = q_ref[0, :, hs[h]]
            b = b_scr[h]

            def lag(d, carry, h=h, q=q, b=b):
                start = chunk - d
                w = q * kkp_scr[pl.ds(start, t_tile), :] * jnp.exp(
                    jnp.minimum(b - bp_scr[pl.ds(start, t_tile), :], 0.0))
                a = jnp.sum(w, axis=-1, keepdims=True)
                oi_scr[h] += jnp.where(tmod >= d, a, 0.0) * vp_scr[pl.ds(start, t_tile), :]
                return carry

            lax.fori_loop(0, chunk, lag, 0)

    lax.cond(jnp.min(b_scr[...]) >= -SAFE_DECAY, intra_mxu, intra_pairs)

    for h in range(heads):
        for n in range(n_chunks):
            rows = slice(n * chunk, (n + 1) * chunk)
            b_last = b_scr[h, (n + 1) * chunk - 1:(n + 1) * chunk, :]
            kt = (kk_ref[0, rows, hs[h]] * jnp.exp(b_last - b_scr[h, rows, :])).astype(BF16)
            u_scr[h, n] = lax.dot_general(v_ref[0, rows, hs[h]].astype(BF16), kt, TN_DIMS,
                                          preferred_element_type=F32)

    st = [st_scr[h] for h in range(heads)]
    for n in range(n_chunks):
        rows = slice(n * chunk, (n + 1) * chunk)
        for h in range(heads):
            b_last = b_scr[h, (n + 1) * chunk - 1:(n + 1) * chunk, :]
            o = lax.dot_general((q_ref[0, rows, hs[h]] * jnp.exp(b_scr[h, rows, :])).astype(BF16),
                                st[h].astype(BF16), NT_DIMS, preferred_element_type=F32) + oi_scr[h, rows, :]
            st[h] = st[h] * jnp.exp(b_last) + u_scr[h, n]
            o_ref[0, rows, hs[h]] = (_rms(o, glin_ref[...]) * gl_ref[0, rows, hs[h]]).astype(o_ref.dtype)
    for h in range(heads):
        st_scr[h] = st[h]

    @pl.when(it == pl.num_programs(1) - 1)
    def _():
        for h in range(heads):
            sout_ref[0, h] = st[h].T


def _hgrn_tiles(q, g, kk, v, gl, s0, glin, *, chunk, t_tile):
    nb, t, d_lin = q.shape
    heads = d_lin // LIN_HEAD_DIM
    assert LANES % chunk == 0 and t_tile % LANES == 0 and t % t_tile == 0
    seq_spec = pl.BlockSpec((1, t_tile, d_lin), lambda b, i: (b, i, 0))
    st_spec = pl.BlockSpec((1, heads, LIN_HEAD_DIM, LIN_HEAD_DIM), lambda b, i: (b, 0, 0, 0))
    tile = pltpu.VMEM((heads, t_tile, LIN_HEAD_DIM), F32)
    halo = pltpu.VMEM((chunk + t_tile, LIN_HEAD_DIM), F32)
    return pl.pallas_call(
        functools.partial(_hgrn_tile_body, chunk=chunk, t_tile=t_tile, heads=heads),
        grid=(nb, t // t_tile),
        in_specs=[seq_spec] * 5 + [st_spec, pl.BlockSpec((1, LIN_HEAD_DIM), lambda b, i: (0, 0))],
        out_specs=[seq_spec, st_spec],
        out_shape=[jax.ShapeDtypeStruct((nb, t, d_lin), BF16),
                   jax.ShapeDtypeStruct(s0.shape, F32)],
        scratch_shapes=[pltpu.VMEM((heads, LIN_HEAD_DIM, LIN_HEAD_DIM), F32), tile, tile,
                        pltpu.VMEM((heads, t_tile // chunk, LIN_HEAD_DIM, LIN_HEAD_DIM), F32),
                        halo, halo, halo],
        compiler_params=pltpu.CompilerParams(
            dimension_semantics=("arbitrary", "arbitrary"), vmem_limit_bytes=VMEM_LIMIT),
        name="hgrn_tiles",
    )(q, g, kk, v, gl, s0, glin)


def _t5_first_distances():
    max_exact = N_BUCKETS // 2
    n = np.arange(0, MAX_DISTANCE + 1, dtype=np.int32)
    ratio = np.maximum(n, 1).astype(np.float32) / np.float32(max_exact)
    large = max_exact + (np.log(ratio) / np.float32(math.log(MAX_DISTANCE / max_exact))
                         * np.float32(N_BUCKETS - max_exact)).astype(np.int32)
    bucket = np.where(n < max_exact, n, np.minimum(large, N_BUCKETS - 1))
    assert np.all(np.diff(bucket) >= 0) and bucket[-1] == N_BUCKETS - 1
    return [int(np.argmax(bucket >= bk)) for bk in range(N_BUCKETS)]


_T5_FIRST_DISTANCE = _t5_first_distances()


def _t5_bias(dist, rb_ref, head):
    n = jnp.maximum(dist, 0)
    out = jnp.full(dist.shape, rb_ref[0, head], F32)
    for bk in range(1, N_BUCKETS):
        out = jnp.where(n >= _T5_FIRST_DISTANCE[bk], rb_ref[bk, head], out)
    return out


def _prompt_bias_body(rb_ref, o_ref):
    h = pl.program_id(0)
    d = pl.program_id(1)
    x = lax.broadcasted_iota(jnp.int32, (SUBLANES, 2 * MOBA_BLOCK), 1)
    dist = (d - 1) * MOBA_BLOCK + x
    by_dist = jnp.where(dist >= 0, _t5_bias(dist, rb_ref, h) * LOG2E, NEG)
    wide = jnp.broadcast_to(by_dist[0:1, :], (MOBA_BLOCK, 2 * MOBA_BLOCK))
    o_ref[0, 0] = pltpu.roll(wide, 0, 1, stride=1, stride_axis=0)[:, MOBA_BLOCK:]


def _prompt_bias(rel_bias, n_tiles):
    heads = rel_bias.shape[1]
    return pl.pallas_call(
        _prompt_bias_body,
        grid=(heads, n_tiles),
        in_specs=[pl.BlockSpec(memory_space=pltpu.SMEM)],
        out_specs=pl.BlockSpec((1, 1, MOBA_BLOCK, MOBA_BLOCK), lambda h, d: (h, d, 0, 0)),
        out_shape=jax.ShapeDtypeStruct((heads, n_tiles, MOBA_BLOCK, MOBA_BLOCK), F32),
        name="prompt_bias",
    )(rel_bias)


def _sample_bias_body(rb_ref, o_ref, *, past_len, n_valid, rows):
    h = pl.program_id(0)
    width = past_len + LANES
    r = lax.broadcasted_iota(jnp.int32, (rows, width), 0)
    kpos = lax.broadcasted_iota(jnp.int32, (rows, width), 1)
    dist = past_len + r - kpos
    ok = (dist >= 0) & (kpos < past_len + n_valid)
    o_ref[...] = jnp.where(ok, _t5_bias(dist, rb_ref, h), NEG)


def _sample_bias(rel_bias, *, past_len, n_valid, rows):
    heads = rel_bias.shape[1]
    width = past_len + LANES
    return pl.pallas_call(
        functools.partial(_sample_bias_body, past_len=past_len, n_valid=n_valid, rows=rows),
        grid=(heads,),
        in_specs=[pl.BlockSpec(memory_space=pltpu.SMEM)],
        out_specs=pl.BlockSpec((rows, width), lambda h: (h, 0)),
        out_shape=jax.ShapeDtypeStruct((heads * rows, width), F32),
        name="sample_bias",
    )(rel_bias)


def _topk_keep(gate, idx, own, axis, n):
    past = idx < own
    gm = jnp.where(past, gate, -jnp.inf)
    rank = jnp.zeros(gate.shape, jnp.int32)
    for jp in range(n):
        gj = lax.slice_in_dim(gm, jp, jp + 1, axis=axis)
        beats = (gj > gm) | ((gj == gm) & (jp < idx))
        rank = rank + beats.astype(jnp.int32)
    keep = ((rank < MOBA_TOPK) & past) | (idx == own)
    return jnp.where(keep, 0.0, NEG)


def _moba_gate_body(qt_ref, k_ref, a_ref, km_scr, *, n_blocks):
    t = n_blocks * MOBA_BLOCK
    qt = qt_ref[0]
    lane = lax.broadcasted_iota(jnp.int32, (1, LANES), 1)
    for n in range(n_blocks):
        km_scr[pl.ds(n, 1), :] = jnp.sum(
            k_ref[0, n * MOBA_BLOCK:(n + 1) * MOBA_BLOCK, :], axis=0, keepdims=True) * (1.0 / MOBA_BLOCK)
    km = km_scr[...]
    blk = lax.broadcasted_iota(jnp.int32, (n_blocks, t), 0)
    own = lax.broadcasted_iota(jnp.int32, (n_blocks, t), 1) >> int(math.log2(MOBA_BLOCK))
    for hh in range(HEADS_PER_TILE):
        kmh = jnp.where(_in_head(lane, hh), km, 0.0)
        gate_t = jnp.dot(kmh, qt, precision=lax.Precision.HIGHEST, preferred_element_type=F32)
        a_ref[0, 0, hh * n_blocks:(hh + 1) * n_blocks, :] = _topk_keep(gate_t, blk, own, axis=0, n=n_blocks)


def _moba_gate(aqt, ak):
    b, d_att, t = aqt.shape
    tiles = d_att // LANES
    n_blocks = t // MOBA_BLOCK
    return pl.pallas_call(
        functools.partial(_moba_gate_body, n_blocks=n_blocks),
        grid=(b, tiles),
        in_specs=[pl.BlockSpec((1, LANES, t), lambda i, p: (i, p, 0)),
                  pl.BlockSpec((1, t, LANES), lambda i, p: (i, 0, p))],
        out_specs=pl.BlockSpec((1, 1, HEADS_PER_TILE * n_blocks, t), lambda i, p: (i, p, 0, 0)),
        out_shape=jax.ShapeDtypeStruct((b, tiles, HEADS_PER_TILE * n_blocks, t), F32),
        scratch_shapes=[pltpu.VMEM((n_blocks, LANES), F32)],
        compiler_params=pltpu.CompilerParams(
            dimension_semantics=("arbitrary", "arbitrary"), vmem_limit_bytes=VMEM_LIMIT),
        name="moba_gate",
    )(aqt, ak)


def _moba_prompt_body(qt_ref, k_ref, vt_ref, keep_ref, bias_ref, ga_ref, o_ref, kaug_scr, vtaug_scr, s_scr,
                      *, n_bias, n_blocks, group):
    i = pl.program_id(2)
    rows = group * MOBA_BLOCK
    t = n_blocks * MOBA_BLOCK
    lg_block = int(math.log2(MOBA_BLOCK))
    feat = lax.broadcasted_iota(jnp.int32, (LANES, 1), 0)
    ones_row = [((hh + 1) % HEADS_PER_TILE) * ATT_HEAD_DIM for hh in range(HEADS_PER_TILE)]

    @pl.when(i == 0)
    def _():
        blk = lax.broadcasted_iota(jnp.int32, (rows, LANES), 0) >> lg_block
        lane = lax.broadcasted_iota(jnp.int32, (rows, LANES), 1)
        for c in range(t // rows):
            sl = slice(c * rows, (c + 1) * rows)
            for hh in range(HEADS_PER_TILE):
                onehot = jnp.where(lane == hh * n_blocks + c * group + blk, 1.0, 0.0)
                kaug_scr[hh, sl, :] = jnp.concatenate([k_ref[0, sl, :], onehot], axis=1).astype(BF16)
                vtaug_scr[hh, :, sl] = jnp.where(feat == ones_row[hh], 1.0, vt_ref[0, :, sl]).astype(BF16)

    qt = qt_ref[0] * (ATT_HEAD_DIM ** -0.5 * LOG2E)
    keep = keep_ref[0, 0]
    pad = jnp.zeros((LANES - keep.shape[0], MOBA_BLOCK), F32)
    q_aug = [jnp.concatenate([jnp.where(_in_head(feat, hh), qt, 0.0), keep, pad], axis=0).astype(BF16)
             for hh in range(HEADS_PER_TILE)]
    n_groups = (i >> int(math.log2(group))) + 1

    def scores(gi, m):
        r0 = pl.multiple_of(gi * rows, rows)
        m_new = []
        for hh in range(HEADS_PER_TILE):
            s = jnp.dot(kaug_scr[hh, pl.ds(r0, rows), :], q_aug[hh], preferred_element_type=F32)
            mh = m[hh]
            for u in range(group):
                d = jnp.clip(i - (gi * group + u), 0, n_bias - 1)
                su = s[u * MOBA_BLOCK:(u + 1) * MOBA_BLOCK] + bias_ref[hh, d]
                s_scr[hh, pl.ds(r0 + u * MOBA_BLOCK, MOBA_BLOCK), :] = su
                mh = jnp.maximum(mh, jnp.max(su, axis=0, keepdims=True))
            m_new.append(mh)
        return tuple(m_new)

    def values(gi, m, m_before, acc):
        r0 = pl.multiple_of(gi * rows, rows)
        out = []
        for hh in range(HEADS_PER_TILE):
            p = jnp.exp2(s_scr[hh, pl.ds(r0, rows), :] - m[hh]).astype(BF16)
            out.append(acc[hh] * jnp.exp2(m_before[hh] - m[hh])
                       + jnp.dot(vtaug_scr[hh, :, pl.ds(r0, rows)], p, preferred_element_type=F32))
        return tuple(out)

    m_init = tuple(jnp.full((1, MOBA_BLOCK), M_INIT, F32) for _ in range(HEADS_PER_TILE))
    acc0 = tuple(jnp.zeros((LANES, MOBA_BLOCK), F32) for _ in range(HEADS_PER_TILE))

    def stage(gi, carry):
        m, m_before, acc = carry
        acc = values(gi - 1, m, m_before, acc)
        return scores(gi, m), m, acc

    m, m_before, acc = lax.fori_loop(1, n_groups, stage, (scores(0, m_init), m_init, acc0))
    acc = values(n_groups - 1, m, m_before, acc)
    out = jnp.zeros((LANES, MOBA_BLOCK), F32)
    for hh in range(HEADS_PER_TILE):
        out = jnp.where(_in_head(feat, hh), acc[hh] / acc[hh][ones_row[hh]:ones_row[hh] + 1, :], out)
    o_ref[0] = (out.T * ga_ref[0]).astype(o_ref.dtype)


def _moba_prompt(aqt, ak, avt, keep, bias, ga, *, group=4):
    b, d_att, t = aqt.shape
    tiles = d_att // LANES
    n_blocks = t // MOBA_BLOCK
    n_bias = bias.shape[1]
    group = min(group, n_blocks)
    assert n_blocks % group == 0 and group & (group - 1) == 0
    return pl.pallas_call(
        functools.partial(_moba_prompt_body, n_bias=n_bias, n_blocks=n_blocks, group=group),
        grid=(b, tiles, n_blocks),
        in_specs=[pl.BlockSpec((1, LANES, MOBA_BLOCK), lambda ib, p, i: (ib, p, i)),
                  pl.BlockSpec((1, t, LANES), lambda ib, p, i: (ib, 0, p)),
                  pl.BlockSpec((1, LANES, t), lambda ib, p, i: (ib, p, 0)),
                  pl.BlockSpec((1, 1, keep.shape[2], MOBA_BLOCK), lambda ib, p, i: (ib, p, 0, i)),
                  pl.BlockSpec((HEADS_PER_TILE, n_bias, MOBA_BLOCK, MOBA_BLOCK), lambda ib, p, i: (p, 0, 0, 0)),
                  pl.BlockSpec((1, MOBA_BLOCK, LANES), lambda ib, p, i: (ib, i, p))],
        out_specs=pl.BlockSpec((1, MOBA_BLOCK, LANES), lambda ib, p, i: (ib, i, p)),
        out_shape=jax.ShapeDtypeStruct((b, t, d_att), BF16),
        scratch_shapes=[pltpu.VMEM((HEADS_PER_TILE, t, 2 * LANES), BF16),
                        pltpu.VMEM((HEADS_PER_TILE, LANES, t), BF16),
                        pltpu.VMEM((HEADS_PER_TILE, t, MOBA_BLOCK), F32)],
        compiler_params=pltpu.CompilerParams(
            dimension_semantics=("arbitrary", "arbitrary", "arbitrary"), vmem_limit_bytes=VMEM_LIMIT),
        name="moba_prompt",
    )(aqt, ak, avt, keep, bias, ga.reshape(b, t, d_att))


def _moba_sample_body(pt_ref, q_ref, kn_ref, vn_ref, bias_ref, *rest, n_pages, heads, rows):
    k_pages = rest[:n_pages]
    v_pages = rest[n_pages:2 * n_pages]
    o_ref = rest[2 * n_pages]
    s_scr = rest[2 * n_pages + 1]
    d_att = heads * ATT_HEAD_DIM
    page = k_pages[0].shape[-1]
    pages_per_block = MOBA_BLOCK // page
    n_blocks = n_pages // pages_per_block
    n_q = heads * rows
    feat = lax.broadcasted_iota(jnp.int32, (1, d_att), 1)
    lane = lax.broadcasted_iota(jnp.int32, (n_q, LANES), 1)

    q = q_ref[0] * (ATT_HEAD_DIM ** -0.5)
    q_bd = jnp.concatenate([jnp.where(_in_head(feat, h), q, 0.0) for h in range(heads)], axis=0).astype(BF16)

    gate = jnp.zeros((n_q, LANES), F32)
    for n in range(n_blocks):
        tot = jnp.zeros((n_q, page), F32)
        for p in range(n * pages_per_block, (n + 1) * pages_per_block):
            kt = k_pages[p][...].astype(BF16)
            s = jnp.dot(q_bd, kt, preferred_element_type=F32)
            s_scr[:, p * page:(p + 1) * page] = s
            tot = tot + s
        gate = jnp.where(lane == n, jnp.sum(tot, axis=-1, keepdims=True) * (1.0 / MOBA_BLOCK), gate)
    keep = _topk_keep(gate, lane, jnp.full(gate.shape, n_blocks, jnp.int32), axis=1, n=n_blocks)

    def new_rows(ref):
        return jnp.concatenate([ref[0], jnp.zeros((LANES - rows, d_att), F32)], axis=0).astype(BF16)

    own = n_pages * page
    s_own = lax.dot_general(q_bd, new_rows(kn_ref), NT_DIMS, preferred_element_type=F32)
    s_own = s_own + bias_ref[:, own:own + LANES]
    s_scr[:, own:own + LANES] = s_own
    m_wide = s_own
    for p in range(n_pages):
        n = p // pages_per_block
        s = s_scr[:, p * page:(p + 1) * page] + bias_ref[:, p * page:(p + 1) * page] + keep[:, n:n + 1]
        s_scr[:, p * page:(p + 1) * page] = s
        m_wide = jnp.maximum(m_wide, s)
    m = jnp.max(m_wide, axis=-1, keepdims=True)

    pr = jnp.exp(s_scr[:, own:own + LANES] - m)
    l_wide = pr
    acc = jnp.dot(pr.astype(BF16), new_rows(vn_ref), preferred_element_type=F32)
    for p in range(n_pages):
        pr = jnp.exp(s_scr[:, p * page:(p + 1) * page] - m)
        l_wide = l_wide + pr
        vt = v_pages[p][...].astype(BF16)
        acc = acc + lax.dot_general(pr.astype(BF16), vt, NT_DIMS, preferred_element_type=F32)
    acc = acc / jnp.sum(l_wide, axis=-1, keepdims=True)
    out = jnp.zeros((rows, d_att), F32)
    for h in range(heads):
        out = jnp.where(_in_head(feat, h), acc[h * rows:(h + 1) * rows, :], out)
    o_ref[0] = out


def _moba_sample(aq, ak, av, cache_kt, cache_vt, page_table, bias, *, heads):
    nb, rows, d_att = aq.shape
    n_pages = page_table.shape[1]
    page = cache_kt.shape[-1]
    assert cache_kt.shape[1:] == (d_att, page) and MOBA_BLOCK % page == 0 and page == LANES
    tok = pl.BlockSpec((1, rows, d_att), lambda b, pt: (b, 0, 0))

    def page_spec(p):
        return pl.BlockSpec((None, d_att, page), lambda b, pt, p=p: (pt[b, p], 0, 0))

    grid_spec = pltpu.PrefetchScalarGridSpec(
        num_scalar_prefetch=1,
        grid=(nb,),
        in_specs=[tok, tok, tok, pl.BlockSpec(bias.shape, lambda b, pt: (0, 0))]
                 + [page_spec(p) for p in range(n_pages)] * 2,
        out_specs=tok,
        scratch_shapes=[pltpu.VMEM(bias.shape, F32)],
    )
    return pl.pallas_call(
        functools.partial(_moba_sample_body, n_pages=n_pages, heads=heads, rows=rows),
        grid_spec=grid_spec,
        out_shape=jax.ShapeDtypeStruct((nb, rows, d_att), F32),
        compiler_params=pltpu.CompilerParams(
            dimension_semantics=("arbitrary",), vmem_limit_bytes=VMEM_LIMIT),
        name="moba_sample",
    )(page_table, aq, ak, av, bias, *([cache_kt] * n_pages), *([cache_vt] * n_pages))


def _out_proj_body(*refs, d_lin, gated):
    if gated:
        ol_ref, oa_ref, x_ref, w_ref, gpost_ref, y_ref = refs
        oa = oa_ref[...]
    else:
        ol_ref, oa_ref, ga_ref, x_ref, w_ref, gpost_ref, y_ref = refs
        oa = oa_ref[...] * ga_ref[...]
    o = jnp.dot(ol_ref[...].astype(BF16), w_ref[0:d_lin, :], preferred_element_type=F32)
    o = o + jnp.dot(oa.astype(BF16), w_ref[d_lin:, :], preferred_element_type=F32)
    y_ref[...] = x_ref[...] + _rms(o, gpost_ref[...])


def _out_proj(o_lin, o_att, ga, x, w_bf, gpost, *, tm=256):
    rows, d_model = x.shape
    tm = min(tm, rows)
    assert rows % tm == 0
    d_lin = o_lin.shape[1]
    row_spec = lambda w: pl.BlockSpec((tm, w), lambda i: (i, 0))
    acts = [o_lin, o_att] + ([] if ga is None else [ga])
    return pl.pallas_call(
        functools.partial(_out_proj_body, d_lin=d_lin, gated=ga is None),
        grid=(rows // tm,),
        in_specs=[row_spec(a.shape[1]) for a in acts] + [
            row_spec(d_model),
            pl.BlockSpec(w_bf.shape, lambda i: (0, 0)),
            pl.BlockSpec((1, d_model), lambda i: (0, 0)),
        ],
        out_specs=row_spec(d_model),
        out_shape=jax.ShapeDtypeStruct((rows, d_model), F32),
        compiler_params=pltpu.CompilerParams(dimension_semantics=("arbitrary",)),
        name="out_proj",
    )(*acts, x, w_bf, gpost)


def kernel(x_prompt, x_sample, cache_k, cache_v, state_hgrn, page_table, w_in, w_out,
           norm_pre, norm_post, norm_lin_out, lin_lower_bound, rel_bias):
    depth = w_in.shape[0]
    assert depth == 1 and lin_lower_bound.shape[0] == depth + 1
    b, t, d_model = x_prompt.shape
    nb, ts, _ = x_sample.shape
    d_lin = lin_lower_bound.shape[1]
    d_att = w_out.shape[1] - d_lin
    lin_heads = d_lin // LIN_HEAD_DIM
    att_heads = rel_bias.shape[1]
    n_pages = page_table.shape[1]
    page = cache_k.shape[2]
    past_len = n_pages * page
    assert d_att == att_heads * ATT_HEAD_DIM and t % MOBA_BLOCK == 0
    assert past_len % MOBA_BLOCK == 0 and ts <= SUBLANES

    w_in_bf = w_in[0].astype(BF16)
    w_out_bf = w_out[0].astype(BF16)
    gpre, gpost, glin = norm_pre, norm_post, norm_lin_out
    proj = functools.partial(_in_proj, gpre=gpre, w_bf=w_in_bf, llb=lin_lower_bound,
                             d_lin=d_lin, d_att=d_att)

    xp = x_prompt.reshape(b * t, d_model)
    q, g, kk, v, gl, ga, ak, aqt, akt, avt = proj(xp, seq_len=t)
    q, g, kk, v, gl, ak = [a.reshape(b, t, -1) for a in (q, g, kk, v, gl, ak)]
    s0 = jnp.zeros((b, lin_heads, LIN_HEAD_DIM, LIN_HEAD_DIM), F32)
    o_lin, s_prompt = _hgrn_tiles(q, g, kk, v, gl, s0, glin, chunk=32, t_tile=512)
    keep = _moba_gate(aqt, ak)
    n_bias = 6
    assert (n_bias - 1) * MOBA_BLOCK - (MOBA_BLOCK - 1) >= MAX_DISTANCE
    bias = _prompt_bias(rel_bias, n_bias)
    o_att = _moba_prompt(aqt, ak, avt, keep, bias, ga)
    y_prompt = _out_proj(o_lin.reshape(b * t, d_lin), o_att.reshape(b * t, d_att),
                         None, xp, w_out_bf, gpost, tm=512).reshape(b, t, d_model)

    rows = SUBLANES
    xs = jnp.pad(x_sample, ((0, 0), (0, rows - ts), (0, 0))).reshape(nb * rows, d_model)
    qs, gs, kks, vs, gls, gas, aqs, aks, avs = [a.reshape(nb, rows, -1) for a in proj(xs)]
    o_lin_s, s_sample = _hgrn_step(qs, gs, kks, vs, gls, state_hgrn[0], glin, valid=ts, n_seq=16)
    bias_s = _sample_bias(rel_bias, past_len=past_len, n_valid=ts, rows=rows)
    cache_kt = cache_k[0].transpose(0, 2, 3, 1).reshape(-1, d_att, page)
    cache_vt = cache_v[0].transpose(0, 2, 3, 1).reshape(-1, d_att, page)
    o_att_s = _moba_sample(aqs, aks, avs, cache_kt, cache_vt, page_table, bias_s, heads=att_heads)
    y_s = _out_proj(o_lin_s.reshape(nb * rows, d_lin), o_att_s.reshape(nb * rows, d_att),
                    gas.reshape(nb * rows, d_att), xs, w_out_bf, gpost)
    y_sample = y_s.reshape(nb, rows, d_model)[:, :ts]

    def prompt_kv(a):
        return a.reshape(1, b, att_heads, ATT_HEAD_DIM, t).transpose(0, 1, 4, 2, 3)

    kvs_shape = (1, nb, ts, att_heads, ATT_HEAD_DIM)
    return (y_prompt, y_sample, prompt_kv(akt), prompt_kv(avt), s_prompt[None],
            aks[:, :ts].reshape(kvs_shape), avs[:, :ts].reshape(kvs_shape), s_sample[None])
```

```python
import functools
import math

import jax
import jax.numpy as jnp
import numpy as np
from jax import lax
from jax.experimental import pallas as pl
from jax.experimental.pallas import tpu as pltpu

F32 = jnp.float32
BF16 = jnp.bfloat16

EPS = 1e-6
LIN_HEAD_DIM = 128
ATT_HEAD_DIM = 64
MOBA_BLOCK = 256
MOBA_TOPK = 3
N_BUCKETS = 32
MAX_DISTANCE = 1024
NEG = -2e30
M_INIT = -1e30
LOG2E = 1.4426950408889634
LANES = 128
SUBLANES = 8
VMEM_LIMIT = 56 * 1024 * 1024
HEADS_PER_TILE = LANES // ATT_HEAD_DIM

NT_DIMS = (((1,), (1,)), ((), ()))
TN_DIMS = (((0,), (0,)), ((), ()))


def _sigmoid(x):
    return 1.0 / (1.0 + jnp.exp(-x))


def _silu(x):
    return x * _sigmoid(x)


def _rms(x, g):
    return x * lax.rsqrt(jnp.mean(x * x, axis=-1, keepdims=True) + EPS) * g


def _in_head(idx, head):
    return (idx >= head * ATT_HEAD_DIM) & (idx < (head + 1) * ATT_HEAD_DIM)


def _in_proj_body(x_ref, gpre_ref, w_ref, llb_ref, *rest, d_lin, d_att, feat_major):
    if feat_major:
        wt_ref, q_o, g_o, kk_o, v_o, gl_o, ga_o, ak_o, aqt_o, akt_o, avt_o = rest
    else:
        q_o, g_o, kk_o, v_o, gl_o, ga_o, aq_o, ak_o, av_o = rest
    h = _rms(x_ref[...], gpre_ref[...]).astype(BF16)

    def proj(col, width):
        return jnp.dot(h, w_ref[:, col:col + width], preferred_element_type=F32)

    def proj_t(row, width):
        return lax.dot_general(wt_ref[row:row + width, :], h, NT_DIMS, preferred_element_type=F32)

    llb = llb_ref[...]
    e = jnp.exp(llb - jnp.max(llb, axis=0, keepdims=True))
    lb = e[0:1, :] / jnp.sum(e, axis=0, keepdims=True)

    q_o[...] = _silu(proj(0, d_lin))
    f = lb + (1.0 - lb) * _sigmoid(proj(d_lin, d_lin))
    g_o[...] = jnp.log(f)
    kk_o[...] = 1.0 - f
    v_o[...] = proj(2 * d_lin, d_lin)
    gl_o[...] = _silu(proj(3 * d_lin, d_lin))
    a0 = 4 * d_lin
    ga_o[...] = _silu(proj(a0 + 3 * d_att, d_att))
    if feat_major:
        aqt_o[0] = proj_t(0, d_att)
        akt = proj_t(d_att, d_att)
        akt_o[0] = akt
        ak_o[...] = akt.T
        avt_o[0] = proj_t(2 * d_att, d_att)
    else:
        aq_o[...] = proj(a0, d_att)
        ak_o[...] = proj(a0 + d_att, d_att)
        av_o[...] = proj(a0 + 2 * d_att, d_att)


def _in_proj(x, gpre, w_bf, llb, *, d_lin, d_att, seq_len=None, tm=256):
    rows, d_model = x.shape
    tm = min(tm, rows)
    assert rows % tm == 0
    d_in = w_bf.shape[1]
    feat_major = seq_len is not None
    row_spec = lambda w: pl.BlockSpec((tm, w), lambda i: (i, 0))
    in_specs = [
        pl.BlockSpec((tm, d_model), lambda i: (i, 0)),
        pl.BlockSpec((1, d_model), lambda i: (0, 0)),
        pl.BlockSpec((d_model, d_in), lambda i: (0, 0)),
        pl.BlockSpec(llb.shape, lambda i: (0, 0)),
    ]
    args = [x, gpre, w_bf, llb]
    out_specs = [row_spec(d_lin)] * 5 + [row_spec(d_att)]
    out_shape = [jax.ShapeDtypeStruct((rows, d_lin), F32)] * 5 + [jax.ShapeDtypeStruct((rows, d_att), F32)]
    if feat_major:
        assert seq_len % tm == 0
        tiles = seq_len // tm
        a0 = 4 * d_lin
        wt_bf = w_bf[:, a0:a0 + 3 * d_att].T
        in_specs.append(pl.BlockSpec(wt_bf.shape, lambda i: (0, 0)))
        args.append(wt_bf)
        t_spec = pl.BlockSpec((1, d_att, tm), lambda i: (i // tiles, 0, i % tiles))
        t_shape = jax.ShapeDtypeStruct((rows // seq_len, d_att, seq_len), F32)
        out_specs += [row_spec(d_att)] + [t_spec] * 3
        out_shape += [jax.ShapeDtypeStruct((rows, d_att), F32)] + [t_shape] * 3
    else:
        out_specs += [row_spec(d_att)] * 3
        out_shape += [jax.ShapeDtypeStruct((rows, d_att), F32)] * 3
    return pl.pallas_call(
        functools.partial(_in_proj_body, d_lin=d_lin, d_att=d_att, feat_major=feat_major),
        grid=(rows // tm,),
        in_specs=in_specs,
        out_specs=out_specs,
        out_shape=out_shape,
        compiler_params=pltpu.CompilerParams(
            dimension_semantics=("arbitrary",), vmem_limit_bytes=VMEM_LIMIT),
        name="in_proj",
    )(*args)


def _hgrn_step_body(q_ref, g_ref, kk_ref, v_ref, gl_ref, s0_ref, glin_ref, o_ref, sout_ref, b_scr,
                    *, rows, valid, n_seq):
    row = lax.broadcasted_iota(jnp.int32, (rows, 1), 0)
    for s in range(n_seq):
        acc = jnp.zeros((1, LIN_HEAD_DIM), F32)
        for r in range(rows):
            if r < valid:
                acc = acc + g_ref[s, r:r + 1, :]
            b_scr[s, r:r + 1, :] = acc
    seqs = range(n_seq)
    b = [b_scr[s] for s in seqs]
    b_last = [b_scr[s, rows - 1:rows, :] for s in seqs]
    q = [q_ref[s] for s in seqs]
    kk = [jnp.where(row < valid, kk_ref[s], 0.0) for s in seqs]

    o = [jnp.dot((q[s] * jnp.exp(b[s])).astype(BF16), s0_ref[s, 0].astype(BF16), preferred_element_type=F32)
         for s in seqs]
    for s in seqs:
        for j in range(valid):
            w = q[s] * kk_ref[s, j:j + 1, :] * jnp.exp(jnp.minimum(b[s] - b_scr[s, j:j + 1, :], 0.0))
            a = jnp.sum(w, axis=-1, keepdims=True)
            o[s] = o[s] + jnp.where(row >= j, a, 0.0) * v_ref[s, j:j + 1, :]
    u = [lax.dot_general((kk[s] * jnp.exp(b_last[s] - b[s])).astype(BF16), v_ref[s].astype(BF16), TN_DIMS,
                         preferred_element_type=F32) for s in seqs]
    ones = jnp.ones((rows, LIN_HEAD_DIM), BF16)
    decay = []
    for s in seqs:
        rest = jnp.exp(b_last[s])
        pieces = jnp.zeros((rows, LIN_HEAD_DIM), F32)
        for pi in range(3):
            piece = rest.astype(BF16).astype(F32)
            rest = rest - piece
            pieces = jnp.where(row == pi, piece, pieces)
        decay.append(lax.dot_general(pieces.astype(BF16), ones, TN_DIMS, preferred_element_type=F32))
    for s in seqs:
        sout_ref[s, 0] = s0_ref[s, 0] * decay[s] + u[s]
        o_ref[s] = _rms(o[s], glin_ref[...]) * gl_ref[s]


def _hgrn_step(q, g, kk, v, gl, s0, glin, *, valid, n_seq):
    nb, rows, d_lin = q.shape
    heads = d_lin // LIN_HEAD_DIM
    assert rows >= 3 and nb % n_seq == 0
    seq_spec = pl.BlockSpec((n_seq, rows, LIN_HEAD_DIM), lambda b, h: (b, 0, h))
    st_spec = pl.BlockSpec((n_seq, 1, LIN_HEAD_DIM, LIN_HEAD_DIM), lambda b, h: (b, h, 0, 0))
    return pl.pallas_call(
        functools.partial(_hgrn_step_body, rows=rows, valid=valid, n_seq=n_seq),
        grid=(nb // n_seq, heads),
        in_specs=[seq_spec] * 5 + [st_spec, pl.BlockSpec((1, LIN_HEAD_DIM), lambda b, h: (0, 0))],
        out_specs=[seq_spec, st_spec],
        out_shape=[jax.ShapeDtypeStruct((nb, rows, d_lin), F32),
                   jax.ShapeDtypeStruct(s0.shape, F32)],
        scratch_shapes=[pltpu.VMEM((n_seq, rows, LIN_HEAD_DIM), F32)],
        compiler_params=pltpu.CompilerParams(dimension_semantics=("arbitrary", "arbitrary")),
        name="hgrn_step",
    )(q, g, kk, v, gl, s0, glin)


SAFE_DECAY = 60.0


def _hgrn_tile_body(q_ref, g_ref, kk_ref, v_ref, gl_ref, s0_ref, glin_ref, o_ref, sout_ref,
                    st_scr, b_scr, oi_scr, u_scr, kkp_scr, bp_scr, vp_scr, *, chunk, t_tile, heads):
    it = pl.program_id(1)
    hs = [slice(h * LIN_HEAD_DIM, (h + 1) * LIN_HEAD_DIM) for h in range(heads)]

    @pl.when(it == 0)
    def _():
        for h in range(heads):
            st_scr[h] = s0_ref[0, h].T

    lg = int(math.log2(chunk))
    n_groups = t_tile // LANES
    n_chunks = t_tile // chunk
    r = lax.broadcasted_iota(jnp.int32, (LANES, LANES), 0)
    c = lax.broadcasted_iota(jnp.int32, (LANES, LANES), 1)
    causal = ((r >> lg) == (c >> lg)) & (c <= r)
    tri = jnp.where(causal, 1.0, 0.0).astype(BF16)
    n_pieces = 3
    for h in range(heads):
        pieces = []
        for gi in range(n_groups):
            rest = g_ref[0, gi * LANES:(gi + 1) * LANES, hs[h]]
            for _ in range(n_pieces):
                piece = rest.astype(BF16)
                rest = rest - piece.astype(F32)
                pieces.append(piece)
        sums = jnp.dot(tri, jnp.concatenate(pieces, axis=1), preferred_element_type=F32)
        for gi in range(n_groups):
            b = jnp.zeros((LANES, LIN_HEAD_DIM), F32)
            for pi in range(n_pieces):
                col = (gi * n_pieces + pi) * LIN_HEAD_DIM
                b = b + sums[:, col:col + LIN_HEAD_DIM]
            b_scr[h, gi * LANES:(gi + 1) * LANES, :] = b

    def intra_mxu():
        atts = {}
        for h in range(heads):
            for gi in range(n_groups):
                rows = slice(gi * LANES, (gi + 1) * LANES)
                b = b_scr[h, rows, :]
                qe = (q_ref[0, rows, hs[h]] * jnp.exp(b)).astype(BF16)
                ke = (kk_ref[0, rows, hs[h]] * jnp.exp(-b)).astype(BF16)
                att = lax.dot_general(qe, ke, NT_DIMS, preferred_element_type=F32)
                atts[h, gi] = jnp.where(causal, att, 0.0).astype(BF16)
        for h in range(heads):
            for gi in range(n_groups):
                rows = slice(gi * LANES, (gi + 1) * LANES)
                oi_scr[h, rows, :] = jnp.dot(atts[h, gi], v_ref[0, rows, hs[h]].astype(BF16),
                                             preferred_element_type=F32)

    def intra_pairs():
        zeros = jnp.zeros((chunk, LIN_HEAD_DIM), F32)
        tmod = lax.broadcasted_iota(jnp.int32, (t_tile, 1), 0) & (chunk - 1)
        for h in range(heads):
            for dst, src in ((kkp_scr, kk_ref[0, :, hs[h]]), (bp_scr, b_scr[h]), (vp_scr, v_ref[0, :, hs[h]])):
                dst[0:chunk, :] = zeros
                dst[chunk:, :] = src
            oi_scr[h] = jnp.zeros((t_tile, LIN_HEAD_DIM), F32)
            q = q_ref[0, :, hs[h]]
            b = b_scr[h]

            def lag(d, carry, h=h, q=q, b=b):
                start = chunk - d
                w = q * kkp_scr[pl.ds(start, t_tile), :] * jnp.exp(
                    jnp.minimum(b - bp_scr[pl.ds(start, t_tile), :], 0.0))
                a = jnp.sum(w, axis=-1, keepdims=True)
                oi_scr[h] += jnp.where(tmod >= d, a, 0.0) * vp_scr[pl.ds(start, t_tile), :]
                return carry

            lax.fori_loop(0, chunk, lag, 0)

    lax.cond(jnp.min(b_scr[...]) >= -SAFE_DECAY, intra_mxu, intra_pairs)

    for h in range(heads):
        for n in range(n_chunks):
            rows = slice(n * chunk, (n + 1) * chunk)
            b_last = b_scr[h, (n + 1) * chunk - 1:(n + 1) * chunk, :]
            kt = (kk_ref[0, rows, hs[h]] * jnp.exp(b_last - b_scr[h, rows, :])).astype(BF16)
            u_scr[h, n] = lax.dot_general(v_ref[0, rows, hs[h]].astype(BF16), kt, TN_DIMS,
                                          preferred_element_type=F32)

    st = [st_scr[h] for h in range(heads)]
    for n in range(n_chunks):
        rows = slice(n * chunk, (n + 1) * chunk)
        for h in range(heads):
            b_last = b_scr[h, (n + 1) * chunk - 1:(n + 1) * chunk, :]
            o = lax.dot_general((q_ref[0, rows, hs[h]] * jnp.exp(b_scr[h, rows, :])).astype(BF16),
                                st[h].astype(BF16), NT_DIMS, preferred_element_type=F32) + oi_scr[h, rows, :]
            st[h] = st[h] * jnp.exp(b_last) + u_scr[h, n]
            o_ref[0, rows, hs[h]] = (_rms(o, glin_ref[...]) * gl_ref[0, rows, hs[h]]).astype(o_ref.dtype)
    for h in range(heads):
        st_scr[h] = st[h]

    @pl.when(it == pl.num_programs(1) - 1)
    def _():
        for h in range(heads):
            sout_ref[0, h] = st[h].T


def _hgrn_tiles(q, g, kk, v, gl, s0, glin, *, chunk, t_tile):
    nb, t, d_lin = q.shape
    heads = d_lin // LIN_HEAD_DIM
    assert LANES % chunk == 0 and t_tile % LANES == 0 and t % t_tile == 0
    seq_spec = pl.BlockSpec((1, t_tile, d_lin), lambda b, i: (b, i, 0))
    st_spec = pl.BlockSpec((1, heads, LIN_HEAD_DIM, LIN_HEAD_DIM), lambda b, i: (b, 0, 0, 0))
    tile = pltpu.VMEM((heads, t_tile, LIN_HEAD_DIM), F32)
    halo = pltpu.VMEM((chunk + t_tile, LIN_HEAD_DIM), F32)
    return pl.pallas_call(
        functools.partial(_hgrn_tile_body, chunk=chunk, t_tile=t_tile, heads=heads),
        grid=(nb, t // t_tile),
        in_specs=[seq_spec] * 5 + [st_spec, pl.BlockSpec((1, LIN_HEAD_DIM), lambda b, i: (0, 0))],
        out_specs=[seq_spec, st_spec],
        out_shape=[jax.ShapeDtypeStruct((nb, t, d_lin), BF16),
                   jax.ShapeDtypeStruct(s0.shape, F32)],
        scratch_shapes=[pltpu.VMEM((heads, LIN_HEAD_DIM, LIN_HEAD_DIM), F32), tile, tile,
                        pltpu.VMEM((heads, t_tile // chunk, LIN_HEAD_DIM, LIN_HEAD_DIM), F32),
                        halo, halo, halo],
        compiler_params=pltpu.CompilerParams(
            dimension_semantics=("arbitrary", "arbitrary"), vmem_limit_bytes=VMEM_LIMIT),
        name="hgrn_tiles",
    )(q, g, kk, v, gl, s0, glin)


def _t5_first_distances():
    max_exact = N_BUCKETS // 2
    n = np.arange(0, MAX_DISTANCE + 1, dtype=np.int32)
    ratio = np.maximum(n, 1).astype(np.float32) / np.float32(max_exact)
    large = max_exact + (np.log(ratio) / np.float32(math.log(MAX_DISTANCE / max_exact))
                         * np.float32(N_BUCKETS - max_exact)).astype(np.int32)
    bucket = np.where(n < max_exact, n, np.minimum(large, N_BUCKETS - 1))
    assert np.all(np.diff(bucket) >= 0) and bucket[-1] == N_BUCKETS - 1
    return [int(np.argmax(bucket >= bk)) for bk in range(N_BUCKETS)]


_T5_FIRST_DISTANCE = _t5_first_distances()


def _t5_bias(dist, rb_ref, head):
    n = jnp.maximum(dist, 0)
    out = jnp.full(dist.shape, rb_ref[0, head], F32)
    for bk in range(1, N_BUCKETS):
        out = jnp.where(n >= _T5_FIRST_DISTANCE[bk], rb_ref[bk, head], out)
    return out


def _prompt_bias_body(rb_ref, o_ref):
    h = pl.program_id(0)
    d = pl.program_id(1)
    x = lax.broadcasted_iota(jnp.int32, (SUBLANES, 2 * MOBA_BLOCK), 1)
    dist = (d - 1) * MOBA_BLOCK + x
    by_dist = jnp.where(dist >= 0, _t5_bias(dist, rb_ref, h) * LOG2E, NEG)
    wide = jnp.broadcast_to(by_dist[0:1, :], (MOBA_BLOCK, 2 * MOBA_BLOCK))
    o_ref[0, 0] = pltpu.roll(wide, 0, 1, stride=1, stride_axis=0)[:, MOBA_BLOCK:]


def _prompt_bias(rel_bias, n_tiles):
    heads = rel_bias.shape[1]
    return pl.pallas_call(
        _prompt_bias_body,
        grid=(heads, n_tiles),
        in_specs=[pl.BlockSpec(memory_space=pltpu.SMEM)],
        out_specs=pl.BlockSpec((1, 1, MOBA_BLOCK, MOBA_BLOCK), lambda h, d: (h, d, 0, 0)),
        out_shape=jax.ShapeDtypeStruct((heads, n_tiles, MOBA_BLOCK, MOBA_BLOCK), F32),
        name="prompt_bias",
    )(rel_bias)


def _sample_bias_body(rb_ref, o_ref, *, past_len, n_valid, rows):
    h = pl.program_id(0)
    width = past_len + LANES
    r = lax.broadcasted_iota(jnp.int32, (rows, width), 0)
    kpos = lax.broadcasted_iota(jnp.int32, (rows, width), 1)
    dist = past_len + r - kpos
    ok = (dist >= 0) & (kpos < past_len + n_valid)
    o_ref[...] = jnp.where(ok, _t5_bias(dist, rb_ref, h), NEG)


def _sample_bias(rel_bias, *, past_len, n_valid, rows):
    heads = rel_bias.shape[1]
    width = past_len + LANES
    return pl.pallas_call(
        functools.partial(_sample_bias_body, past_len=past_len, n_valid=n_valid, rows=rows),
        grid=(heads,),
        in_specs=[pl.BlockSpec(memory_space=pltpu.SMEM)],
        out_specs=pl.BlockSpec((rows, width), lambda h: (h, 0)),
        out_shape=jax.ShapeDtypeStruct((heads * rows, width), F32),
        name="sample_bias",
    )(rel_bias)


def _topk_keep(gate, idx, own, axis, n):
    past = idx < own
    gm = jnp.where(past, gate, -jnp.inf)
    rank = jnp.zeros(gate.shape, jnp.int32)
    for jp in range(n):
        gj = lax.slice_in_dim(gm, jp, jp + 1, axis=axis)
        beats = (gj > gm) | ((gj == gm) & (jp < idx))
        rank = rank + beats.astype(jnp.int32)
    keep = ((rank < MOBA_TOPK) & past) | (idx == own)
    return jnp.where(keep, 0.0, NEG)


def _moba_gate_body(qt_ref, k_ref, a_ref, km_scr, *, n_blocks):
    t = n_blocks * MOBA_BLOCK
    qt = qt_ref[0]
    lane = lax.broadcasted_iota(jnp.int32, (1, LANES), 1)
    for n in range(n_blocks):
        km_scr[pl.ds(n, 1), :] = jnp.sum(
            k_ref[0, n * MOBA_BLOCK:(n + 1) * MOBA_BLOCK, :], axis=0, keepdims=True) * (1.0 / MOBA_BLOCK)
    km = km_scr[...]
    blk = lax.broadcasted_iota(jnp.int32, (n_blocks, t), 0)
    own = lax.broadcasted_iota(jnp.int32, (n_blocks, t), 1) >> int(math.log2(MOBA_BLOCK))
    for hh in range(HEADS_PER_TILE):
        kmh = jnp.where(_in_head(lane, hh), km, 0.0)
        gate_t = jnp.dot(kmh, qt, precision=lax.Precision.HIGHEST, preferred_element_type=F32)
        a_ref[0, 0, hh * n_blocks:(hh + 1) * n_blocks, :] = _topk_keep(gate_t, blk, own, axis=0, n=n_blocks)


def _moba_gate(aqt, ak):
    b, d_att, t = aqt.shape
    tiles = d_att // LANES
    n_blocks = t // MOBA_BLOCK
    return pl.pallas_call(
        functools.partial(_moba_gate_body, n_blocks=n_blocks),
        grid=(b, tiles),
        in_specs=[pl.BlockSpec((1, LANES, t), lambda i, p: (i, p, 0)),
                  pl.BlockSpec((1, t, LANES), lambda i, p: (i, 0, p))],
        out_specs=pl.BlockSpec((1, 1, HEADS_PER_TILE * n_blocks, t), lambda i, p: (i, p, 0, 0)),
        out_shape=jax.ShapeDtypeStruct((b, tiles, HEADS_PER_TILE * n_blocks, t), F32),
        scratch_shapes=[pltpu.VMEM((n_blocks, LANES), F32)],
        compiler_params=pltpu.CompilerParams(
            dimension_semantics=("arbitrary", "arbitrary"), vmem_limit_bytes=VMEM_LIMIT),
        name="moba_gate",
    )(aqt, ak)


def _moba_prompt_body(qt_ref, k_ref, vt_ref, keep_ref, bias_ref, ga_ref, o_ref, kaug_scr, vtaug_scr, s_scr,
                      *, n_bias, n_blocks, group, tps):
    i = pl.program_id(2)
    rows = group * MOBA_BLOCK
    t = n_blocks * MOBA_BLOCK
    lg_block = int(math.log2(MOBA_BLOCK))
    feat = lax.broadcasted_iota(jnp.int32, (LANES, 1), 0)
    heads = [(tt, hh) for tt in range(tps) for hh in range(HEADS_PER_TILE)]
    tile = [slice(tt * LANES, (tt + 1) * LANES) for tt in range(tps)]
    ones_row = [((hh + 1) % HEADS_PER_TILE) * ATT_HEAD_DIM for hh in range(HEADS_PER_TILE)]

    @pl.when(i == 0)
    def _():
        blk = lax.broadcasted_iota(jnp.int32, (rows, LANES), 0) >> lg_block
        lane = lax.broadcasted_iota(jnp.int32, (rows, LANES), 1)
        for c in range(t // rows):
            sl = slice(c * rows, (c + 1) * rows)
            for hd, (tt, hh) in enumerate(heads):
                onehot = jnp.where(lane == hh * n_blocks + c * group + blk, 1.0, 0.0)
                kaug_scr[hd, sl, :] = jnp.concatenate([k_ref[0, sl, tile[tt]], onehot], axis=1).astype(BF16)
                vtaug_scr[hd, :, sl] = jnp.where(feat == ones_row[hh], 1.0, vt_ref[0, tile[tt], sl]).astype(BF16)

    q_aug = []
    for tt, hh in heads:
        qt = qt_ref[0, tile[tt], :] * (ATT_HEAD_DIM ** -0.5 * LOG2E)
        keep = keep_ref[0, tt]
        pad = jnp.zeros((LANES - keep.shape[0], MOBA_BLOCK), F32)
        q_aug.append(jnp.concatenate([jnp.where(_in_head(feat, hh), qt, 0.0), keep, pad], axis=0).astype(BF16))
    n_groups = (i >> int(math.log2(group))) + 1
    n_heads = len(heads)

    def scores(gi, m):
        r0 = pl.multiple_of(gi * rows, rows)
        m_new = []
        for hd in range(n_heads):
            s = jnp.dot(kaug_scr[hd, pl.ds(r0, rows), :], q_aug[hd], preferred_element_type=F32)
            mh = m[hd]
            for u in range(group):
                d = jnp.clip(i - (gi * group + u), 0, n_bias - 1)
                su = s[u * MOBA_BLOCK:(u + 1) * MOBA_BLOCK] + bias_ref[hd, d]
                s_scr[hd, pl.ds(r0 + u * MOBA_BLOCK, MOBA_BLOCK), :] = su
                mh = jnp.maximum(mh, jnp.max(su, axis=0, keepdims=True))
            m_new.append(mh)
        return tuple(m_new)

    def values(gi, m, m_before, acc):
        r0 = pl.multiple_of(gi * rows, rows)
        out = []
        for hd in range(n_heads):
            p = jnp.exp2(s_scr[hd, pl.ds(r0, rows), :] - m[hd]).astype(BF16)
            out.append(acc[hd] * jnp.exp2(m_before[hd] - m[hd])
                       + jnp.dot(vtaug_scr[hd, :, pl.ds(r0, rows)], p, preferred_element_type=F32))
        return tuple(out)

    m_init = tuple(jnp.full((1, MOBA_BLOCK), M_INIT, F32) for _ in range(n_heads))
    acc0 = tuple(jnp.zeros((LANES, MOBA_BLOCK), F32) for _ in range(n_heads))

    def stage(gi, carry):
        m, m_before, acc = carry
        acc = values(gi - 1, m, m_before, acc)
        return scores(gi, m), m, acc

    m, m_before, acc = lax.fori_loop(1, n_groups, stage, (scores(0, m_init), m_init, acc0))
    acc = values(n_groups - 1, m, m_before, acc)
    for tt in range(tps):
        out = jnp.zeros((LANES, MOBA_BLOCK), F32)
        for hh in range(HEADS_PER_TILE):
            a = acc[tt * HEADS_PER_TILE + hh]
            out = jnp.where(_in_head(feat, hh), a / a[ones_row[hh]:ones_row[hh] + 1, :], out)
        o_ref[0, :, tile[tt]] = (out.T * ga_ref[0, :, tile[tt]]).astype(o_ref.dtype)


def _moba_prompt(aqt, ak, avt, keep, bias, ga, *, group=4, tps=2):
    b, d_att, t = aqt.shape
    tiles = d_att // LANES
    n_blocks = t // MOBA_BLOCK
    n_bias = bias.shape[1]
    group = min(group, n_blocks)
    assert n_blocks % group == 0 and group & (group - 1) == 0 and tiles % tps == 0
    width = tps * LANES
    n_heads = tps * HEADS_PER_TILE
    once = dict(pipeline_mode=pl.Buffered(1))
    return pl.pallas_call(
        functools.partial(_moba_prompt_body, n_bias=n_bias, n_blocks=n_blocks, group=group, tps=tps),
        grid=(b, tiles // tps, n_blocks),
        in_specs=[pl.BlockSpec((1, width, MOBA_BLOCK), lambda ib, p, i: (ib, p, i)),
                  pl.BlockSpec((1, t, width), lambda ib, p, i: (ib, 0, p), **once),
                  pl.BlockSpec((1, width, t), lambda ib, p, i: (ib, p, 0), **once),
                  pl.BlockSpec((1, tps, keep.shape[2], MOBA_BLOCK), lambda ib, p, i: (ib, p, 0, i)),
                  pl.BlockSpec((n_heads, n_bias, MOBA_BLOCK, MOBA_BLOCK), lambda ib, p, i: (p, 0, 0, 0), **once),
                  pl.BlockSpec((1, MOBA_BLOCK, width), lambda ib, p, i: (ib, i, p))],
        out_specs=pl.BlockSpec((1, MOBA_BLOCK, width), lambda ib, p, i: (ib, i, p)),
        out_shape=jax.ShapeDtypeStruct((b, t, d_att), BF16),
        scratch_shapes=[pltpu.VMEM((n_heads, t, 2 * LANES), BF16),
                        pltpu.VMEM((n_heads, LANES, t), BF16),
                        pltpu.VMEM((n_heads, t, MOBA_BLOCK), F32)],
        compiler_params=pltpu.CompilerParams(
            dimension_semantics=("arbitrary", "arbitrary", "arbitrary"), vmem_limit_bytes=VMEM_LIMIT),
        name="moba_prompt",
    )(aqt, ak, avt, keep, bias, ga.reshape(b, t, d_att))


def _moba_sample_body(pt_ref, q_ref, kn_ref, vn_ref, bias_ref, *rest, n_pages, heads, rows):
    k_pages = rest[:n_pages]
    v_pages = rest[n_pages:2 * n_pages]
    o_ref = rest[2 * n_pages]
    s_scr = rest[2 * n_pages + 1]
    d_att = heads * ATT_HEAD_DIM
    page = k_pages[0].shape[-1]
    pages_per_block = MOBA_BLOCK // page
    n_blocks = n_pages // pages_per_block
    n_q = heads * rows
    feat = lax.broadcasted_iota(jnp.int32, (1, d_att), 1)
    lane = lax.broadcasted_iota(jnp.int32, (n_q, LANES), 1)

    q = q_ref[0] * (ATT_HEAD_DIM ** -0.5)
    q_bd = jnp.concatenate([jnp.where(_in_head(feat, h), q, 0.0) for h in range(heads)], axis=0).astype(BF16)

    gate = jnp.zeros((n_q, LANES), F32)
    for n in range(n_blocks):
        tot = jnp.zeros((n_q, page), F32)
        for p in range(n * pages_per_block, (n + 1) * pages_per_block):
            kt = k_pages[p][...].astype(BF16)
            s = jnp.dot(q_bd, kt, preferred_element_type=F32)
            s_scr[:, p * page:(p + 1) * page] = s
            tot = tot + s
        gate = jnp.where(lane == n, jnp.sum(tot, axis=-1, keepdims=True) * (1.0 / MOBA_BLOCK), gate)
    keep = _topk_keep(gate, lane, jnp.full(gate.shape, n_blocks, jnp.int32), axis=1, n=n_blocks)

    def new_rows(ref):
        return jnp.concatenate([ref[0], jnp.zeros((LANES - rows, d_att), F32)], axis=0).astype(BF16)

    own = n_pages * page
    s_own = lax.dot_general(q_bd, new_rows(kn_ref), NT_DIMS, preferred_element_type=F32)
    s_own = s_own + bias_ref[:, own:own + LANES]
    s_scr[:, own:own + LANES] = s_own
    m_wide = s_own
    for p in range(n_pages):
        n = p // pages_per_block
        s = s_scr[:, p * page:(p + 1) * page] + bias_ref[:, p * page:(p + 1) * page] + keep[:, n:n + 1]
        s_scr[:, p * page:(p + 1) * page] = s
        m_wide = jnp.maximum(m_wide, s)
    m = jnp.max(m_wide, axis=-1, keepdims=True)

    pr = jnp.exp(s_scr[:, own:own + LANES] - m)
    l_wide = pr
    acc = jnp.dot(pr.astype(BF16), new_rows(vn_ref), preferred_element_type=F32)
    for p in range(n_pages):
        pr = jnp.exp(s_scr[:, p * page:(p + 1) * page] - m)
        l_wide = l_wide + pr
        vt = v_pages[p][...].astype(BF16)
        acc = acc + lax.dot_general(pr.astype(BF16), vt, NT_DIMS, preferred_element_type=F32)
    acc = acc / jnp.sum(l_wide, axis=-1, keepdims=True)
    out = jnp.zeros((rows, d_att), F32)
    for h in range(heads):
        out = jnp.where(_in_head(feat, h), acc[h * rows:(h + 1) * rows, :], out)
    o_ref[0] = out


def _moba_sample(aq, ak, av, cache_kt, cache_vt, page_table, bias, *, heads):
    nb, rows, d_att = aq.shape
    n_pages = page_table.shape[1]
    page = cache_kt.shape[-1]
    assert cache_kt.shape[1:] == (d_att, page) and MOBA_BLOCK % page == 0 and page == LANES
    tok = pl.BlockSpec((1, rows, d_att), lambda b, pt: (b, 0, 0))

    def page_spec(p):
        return pl.BlockSpec((None, d_att, page), lambda b, pt, p=p: (pt[b, p], 0, 0))

    grid_spec = pltpu.PrefetchScalarGridSpec(
        num_scalar_prefetch=1,
        grid=(nb,),
        in_specs=[tok, tok, tok, pl.BlockSpec(bias.shape, lambda b, pt: (0, 0))]
                 + [page_spec(p) for p in range(n_pages)] * 2,
        out_specs=tok,
        scratch_shapes=[pltpu.VMEM(bias.shape, F32)],
    )
    return pl.pallas_call(
        functools.partial(_moba_sample_body, n_pages=n_pages, heads=heads, rows=rows),
        grid_spec=grid_spec,
        out_shape=jax.ShapeDtypeStruct((nb, rows, d_att), F32),
        compiler_params=pltpu.CompilerParams(
            dimension_semantics=("arbitrary",), vmem_limit_bytes=VMEM_LIMIT),
        name="moba_sample",
    )(page_table, aq, ak, av, bias, *([cache_kt] * n_pages), *([cache_vt] * n_pages))


def _out_proj_body(*refs, d_lin, gated):
    if gated:
        ol_ref, oa_ref, x_ref, w_ref, gpost_ref, y_ref = refs
        oa = oa_ref[...]
    else:
        ol_ref, oa_ref, ga_ref, x_ref, w_ref, gpost_ref, y_ref = refs
        oa = oa_ref[...] * ga_ref[...]
    o = jnp.dot(ol_ref[...].astype(BF16), w_ref[0:d_lin, :], preferred_element_type=F32)
    o = o + jnp.dot(oa.astype(BF16), w_ref[d_lin:, :], preferred_element_type=F32)
    y_ref[...] = x_ref[...] + _rms(o, gpost_ref[...])


def _out_proj(o_lin, o_att, ga, x, w_bf, gpost, *, tm=256):
    rows, d_model = x.shape
    tm = min(tm, rows)
    assert rows % tm == 0
    d_lin = o_lin.shape[1]
    row_spec = lambda w: pl.BlockSpec((tm, w), lambda i: (i, 0))
    acts = [o_lin, o_att] + ([] if ga is None else [ga])
    return pl.pallas_call(
        functools.partial(_out_proj_body, d_lin=d_lin, gated=ga is None),
        grid=(rows // tm,),
        in_specs=[row_spec(a.shape[1]) for a in acts] + [
            row_spec(d_model),
            pl.BlockSpec(w_bf.shape, lambda i: (0, 0)),
            pl.BlockSpec((1, d_model), lambda i: (0, 0)),
        ],
        out_specs=row_spec(d_model),
        out_shape=jax.ShapeDtypeStruct((rows, d_model), F32),
        compiler_params=pltpu.CompilerParams(dimension_semantics=("arbitrary",)),
        name="out_proj",
    )(*acts, x, w_bf, gpost)


def kernel(x_prompt, x_sample, cache_k, cache_v, state_hgrn, page_table, w_in, w_out,
           norm_pre, norm_post, norm_lin_out, lin_lower_bound, rel_bias):
    depth = w_in.shape[0]
    assert depth == 1 and lin_lower_bound.shape[0] == depth + 1
    b, t, d_model = x_prompt.shape
    nb, ts, _ = x_sample.shape
    d_lin = lin_lower_bound.shape[1]
    d_att = w_out.shape[1] - d_lin
    lin_heads = d_lin // LIN_HEAD_DIM
    att_heads = rel_bias.shape[1]
    n_pages = page_table.shape[1]
    page = cache_k.shape[2]
    past_len = n_pages * page
    assert d_att == att_heads * ATT_HEAD_DIM and t % MOBA_BLOCK == 0
    assert past_len % MOBA_BLOCK == 0 and ts <= SUBLANES

    w_in_bf = w_in[0].astype(BF16)
    w_out_bf = w_out[0].astype(BF16)
    gpre, gpost, glin = norm_pre, norm_post, norm_lin_out
    proj = functools.partial(_in_proj, gpre=gpre, w_bf=w_in_bf, llb=lin_lower_bound,
                             d_lin=d_lin, d_att=d_att)

    xp = x_prompt.reshape(b * t, d_model)
    q, g, kk, v, gl, ga, ak, aqt, akt, avt = proj(xp, seq_len=t)
    q, g, kk, v, gl, ak = [a.reshape(b, t, -1) for a in (q, g, kk, v, gl, ak)]
    s0 = jnp.zeros((b, lin_heads, LIN_HEAD_DIM, LIN_HEAD_DIM), F32)
    o_lin, s_prompt = _hgrn_tiles(q, g, kk, v, gl, s0, glin, chunk=32, t_tile=512)
    keep = _moba_gate(aqt, ak)
    n_bias = 6
    assert (n_bias - 1) * MOBA_BLOCK - (MOBA_BLOCK - 1) >= MAX_DISTANCE
    bias = _prompt_bias(rel_bias, n_bias)
    o_att = _moba_prompt(aqt, ak, avt, keep, bias, ga)
    y_prompt = _out_proj(o_lin.reshape(b * t, d_lin), o_att.reshape(b * t, d_att),
                         None, xp, w_out_bf, gpost, tm=512).reshape(b, t, d_model)

    rows = SUBLANES
    xs = jnp.pad(x_sample, ((0, 0), (0, rows - ts), (0, 0))).reshape(nb * rows, d_model)
    qs, gs, kks, vs, gls, gas, aqs, aks, avs = [a.reshape(nb, rows, -1) for a in proj(xs)]
    o_lin_s, s_sample = _hgrn_step(qs, gs, kks, vs, gls, state_hgrn[0], glin, valid=ts, n_seq=16)
    bias_s = _sample_bias(rel_bias, past_len=past_len, n_valid=ts, rows=rows)
    cache_kt = cache_k[0].transpose(0, 2, 3, 1).reshape(-1, d_att, page)
    cache_vt = cache_v[0].transpose(0, 2, 3, 1).reshape(-1, d_att, page)
    o_att_s = _moba_sample(aqs, aks, avs, cache_kt, cache_vt, page_table, bias_s, heads=att_heads)
    y_s = _out_proj(o_lin_s.reshape(nb * rows, d_lin), o_att_s.reshape(nb * rows, d_att),
                    gas.reshape(nb * rows, d_att), xs, w_out_bf, gpost)
    y_sample = y_s.reshape(nb, rows, d_model)[:, :ts]

    def prompt_kv(a):
        return a.reshape(1, b, att_heads, ATT_HEAD_DIM, t).transpose(0, 1, 4, 2, 3)

    kvs_shape = (1, nb, ts, att_heads, ATT_HEAD_DIM)
    return (y_prompt, y_sample, prompt_kv(akt), prompt_kv(avt), s_prompt[None],
            aks[:, :ts].reshape(kvs_shape), avs[:, :ts].reshape(kvs_shape), s_sample[None])
```

```python
import functools
import math

import jax
import jax.numpy as jnp
import numpy as np
from jax import lax
from jax.experimental import pallas as pl
from jax.experimental.pallas import tpu as pltpu

F32 = jnp.float32
BF16 = jnp.bfloat16

EPS = 1e-6
LIN_HEAD_DIM = 128
ATT_HEAD_DIM = 64
MOBA_BLOCK = 256
MOBA_TOPK = 3
N_BUCKETS = 32
MAX_DISTANCE = 1024
NEG = -2e30
M_INIT = -1e30
LOG2E = 1.4426950408889634
LANES = 128
SUBLANES = 8
VMEM_LIMIT = 56 * 1024 * 1024
HEADS_PER_TILE = LANES // ATT_HEAD_DIM

NT_DIMS = (((1,), (1,)), ((), ()))
TN_DIMS = (((0,), (0,)), ((), ()))


def _sigmoid(x):
    return 1.0 / (1.0 + jnp.exp(-x))


def _silu(x):
    return x * _sigmoid(x)


def _rms(x, g):
    return x * lax.rsqrt(jnp.mean(x * x, axis=-1, keepdims=True) + EPS) * g


def _in_head(idx, head):
    return (idx >= head * ATT_HEAD_DIM) & (idx < (head + 1) * ATT_HEAD_DIM)


def _in_proj_body(x_ref, gpre_ref, w_ref, llb_ref, *rest, d_lin, d_att, feat_major):
    if feat_major:
        wt_ref, q_o, g_o, kk_o, v_o, gl_o, ga_o, ak_o, aqt_o, akt_o, avt_o = rest
    else:
        q_o, g_o, kk_o, v_o, gl_o, ga_o, aq_o, ak_o, av_o = rest
    h = _rms(x_ref[...], gpre_ref[...]).astype(BF16)

    def proj(col, width):
        return jnp.dot(h, w_ref[:, col:col + width], preferred_element_type=F32)

    def proj_t(row, width):
        return lax.dot_general(wt_ref[row:row + width, :], h, NT_DIMS, preferred_element_type=F32)

    llb = llb_ref[...]
    e = jnp.exp(llb - jnp.max(llb, axis=0, keepdims=True))
    lb = e[0:1, :] / jnp.sum(e, axis=0, keepdims=True)

    q_o[...] = _silu(proj(0, d_lin))
    f = lb + (1.0 - lb) * _sigmoid(proj(d_lin, d_lin))
    g_o[...] = jnp.log(f)
    kk_o[...] = 1.0 - f
    v_o[...] = proj(2 * d_lin, d_lin)
    gl_o[...] = _silu(proj(3 * d_lin, d_lin))
    a0 = 4 * d_lin
    ga_o[...] = _silu(proj(a0 + 3 * d_att, d_att))
    if feat_major:
        aqt_o[0] = proj_t(0, d_att)
        akt = proj_t(d_att, d_att)
        akt_o[0] = akt
        ak_o[...] = akt.T
        avt_o[0] = proj_t(2 * d_att, d_att)
    else:
        aq_o[...] = proj(a0, d_att)
        ak_o[...] = proj(a0 + d_att, d_att)
        av_o[...] = proj(a0 + 2 * d_att, d_att)


def _in_proj(x, gpre, w_bf, llb, *, d_lin, d_att, seq_len=None, tm=256):
    rows, d_model = x.shape
    tm = min(tm, rows)
    assert rows % tm == 0
    d_in = w_bf.shape[1]
    feat_major = seq_len is not None
    row_spec = lambda w: pl.BlockSpec((tm, w), lambda i: (i, 0))
    in_specs = [
        pl.BlockSpec((tm, d_model), lambda i: (i, 0)),
        pl.BlockSpec((1, d_model), lambda i: (0, 0)),
        pl.BlockSpec((d_model, d_in), lambda i: (0, 0)),
        pl.BlockSpec(llb.shape, lambda i: (0, 0)),
    ]
    args = [x, gpre, w_bf, llb]
    out_specs = [row_spec(d_lin)] * 5 + [row_spec(d_att)]
    out_shape = [jax.ShapeDtypeStruct((rows, d_lin), F32)] * 5 + [jax.ShapeDtypeStruct((rows, d_att), F32)]
    if feat_major:
        assert seq_len % tm == 0
        tiles = seq_len // tm
        a0 = 4 * d_lin
        wt_bf = w_bf[:, a0:a0 + 3 * d_att].T
        in_specs.append(pl.BlockSpec(wt_bf.shape, lambda i: (0, 0)))
        args.append(wt_bf)
        t_spec = pl.BlockSpec((1, d_att, tm), lambda i: (i // tiles, 0, i % tiles))
        t_shape = jax.ShapeDtypeStruct((rows // seq_len, d_att, seq_len), F32)
        out_specs += [row_spec(d_att)] + [t_spec] * 3
        out_shape += [jax.ShapeDtypeStruct((rows, d_att), F32)] + [t_shape] * 3
    else:
        out_specs += [row_spec(d_att)] * 3
        out_shape += [jax.ShapeDtypeStruct((rows, d_att), F32)] * 3
    return pl.pallas_call(
        functools.partial(_in_proj_body, d_lin=d_lin, d_att=d_att, feat_major=feat_major),
        grid=(rows // tm,),
        in_specs=in_specs,
        out_specs=out_specs,
        out_shape=out_shape,
        compiler_params=pltpu.CompilerParams(
            dimension_semantics=("arbitrary",), vmem_limit_bytes=VMEM_LIMIT),
        name="in_proj",
    )(*args)


def _hgrn_step_body(q_ref, g_ref, kk_ref, v_ref, gl_ref, s0_ref, glin_ref, o_ref, sout_ref, b_scr,
                    *, rows, valid, n_seq):
    row = lax.broadcasted_iota(jnp.int32, (rows, 1), 0)
    for s in range(n_seq):
        acc = jnp.zeros((1, LIN_HEAD_DIM), F32)
        for r in range(rows):
            if r < valid:
                acc = acc + g_ref[s, r:r + 1, :]
            b_scr[s, r:r + 1, :] = acc
    seqs = range(n_seq)
    b = [b_scr[s] for s in seqs]
    b_last = [b_scr[s, rows - 1:rows, :] for s in seqs]
    q = [q_ref[s] for s in seqs]
    kk = [jnp.where(row < valid, kk_ref[s], 0.0) for s in seqs]

    o = [jnp.dot((q[s] * jnp.exp(b[s])).astype(BF16), s0_ref[s, 0].astype(BF16), preferred_element_type=F32)
         for s in seqs]
    for s in seqs:
        for j in range(valid):
            w = q[s] * kk_ref[s, j:j + 1, :] * jnp.exp(jnp.minimum(b[s] - b_scr[s, j:j + 1, :], 0.0))
            a = jnp.sum(w, axis=-1, keepdims=True)
            o[s] = o[s] + jnp.where(row >= j, a, 0.0) * v_ref[s, j:j + 1, :]
    u = [lax.dot_general((kk[s] * jnp.exp(b_last[s] - b[s])).astype(BF16), v_ref[s].astype(BF16), TN_DIMS,
                         preferred_element_type=F32) for s in seqs]
    ones = jnp.ones((rows, LIN_HEAD_DIM), BF16)
    decay = []
    for s in seqs:
        rest = jnp.exp(b_last[s])
        pieces = jnp.zeros((rows, LIN_HEAD_DIM), F32)
        for pi in range(3):
            piece = rest.astype(BF16).astype(F32)
            rest = rest - piece
            pieces = jnp.where(row == pi, piece, pieces)
        decay.append(lax.dot_general(pieces.astype(BF16), ones, TN_DIMS, preferred_element_type=F32))
    for s in seqs:
        sout_ref[s, 0] = s0_ref[s, 0] * decay[s] + u[s]
        o_ref[s] = _rms(o[s], glin_ref[...]) * gl_ref[s]


def _hgrn_step(q, g, kk, v, gl, s0, glin, *, valid, n_seq):
    nb, rows, d_lin = q.shape
    heads = d_lin // LIN_HEAD_DIM
    assert rows >= 3 and nb % n_seq == 0
    seq_spec = pl.BlockSpec((n_seq, rows, LIN_HEAD_DIM), lambda b, h: (b, 0, h))
    st_spec = pl.BlockSpec((n_seq, 1, LIN_HEAD_DIM, LIN_HEAD_DIM), lambda b, h: (b, h, 0, 0))
    return pl.pallas_call(
        functools.partial(_hgrn_step_body, rows=rows, valid=valid, n_seq=n_seq),
        grid=(nb // n_seq, heads),
        in_specs=[seq_spec] * 5 + [st_spec, pl.BlockSpec((1, LIN_HEAD_DIM), lambda b, h: (0, 0))],
        out_specs=[seq_spec, st_spec],
        out_shape=[jax.ShapeDtypeStruct((nb, rows, d_lin), F32),
                   jax.ShapeDtypeStruct(s0.shape, F32)],
        scratch_shapes=[pltpu.VMEM((n_seq, rows, LIN_HEAD_DIM), F32)],
        compiler_params=pltpu.CompilerParams(dimension_semantics=("arbitrary", "arbitrary")),
        name="hgrn_step",
    )(q, g, kk, v, gl, s0, glin)


SAFE_DECAY = 60.0


def _hgrn_tile_body(q_ref, g_ref, kk_ref, v_ref, gl_ref, s0_ref, glin_ref, o_ref, sout_ref,
                    st_scr, b_scr, oi_scr, u_scr, kkp_scr, bp_scr, vp_scr, *, chunk, t_tile, heads):
    it = pl.program_id(1)
    hs = [slice(h * LIN_HEAD_DIM, (h + 1) * LIN_HEAD_DIM) for h in range(heads)]

    @pl.when(it == 0)
    def _():
        for h in range(heads):
            st_scr[h] = s0_ref[0, h].T

    lg = int(math.log2(chunk))
    n_groups = t_tile // LANES
    n_chunks = t_tile // chunk
    r = lax.broadcasted_iota(jnp.int32, (LANES, LANES), 0)
    c = lax.broadcasted_iota(jnp.int32, (LANES, LANES), 1)
    causal = ((r >> lg) == (c >> lg)) & (c <= r)
    tri = jnp.where(causal, 1.0, 0.0).astype(BF16)
    n_pieces = 3
    for h in range(heads):
        pieces = []
        for gi in range(n_groups):
            rest = g_ref[0, gi * LANES:(gi + 1) * LANES, hs[h]]
            for _ in range(n_pieces):
                piece = rest.astype(BF16)
                rest = rest - piece.astype(F32)
                pieces.append(piece)
        sums = jnp.dot(tri, jnp.concatenate(pieces, axis=1), preferred_element_type=F32)
        for gi in range(n_groups):
            b = jnp.zeros((LANES, LIN_HEAD_DIM), F32)
            for pi in range(n_pieces):
                col = (gi * n_pieces + pi) * LIN_HEAD_DIM
                b = b + sums[:, col:col + LIN_HEAD_DIM]
            b_scr[h, gi * LANES:(gi + 1) * LANES, :] = b

    def intra_mxu():
        atts = {}
        for h in range(heads):
            for gi in range(n_groups):
                rows = slice(gi * LANES, (gi + 1) * LANES)
                b = b_scr[h, rows, :]
                qe = (q_ref[0, rows, hs[h]] * jnp.exp(b)).astype(BF16)
                ke = (kk_ref[0, rows, hs[h]] * jnp.exp(-b)).astype(BF16)
                att = lax.dot_general(qe, ke, NT_DIMS, preferred_element_type=F32)
                atts[h, gi] = jnp.where(causal, att, 0.0).astype(BF16)
        for h in range(heads):
            for gi in range(n_groups):
                rows = slice(gi * LANES, (gi + 1) * LANES)
                oi_scr[h, rows, :] = jnp.dot(atts[h, gi], v_ref[0, rows, hs[h]].astype(BF16),
                                             preferred_element_type=F32)

    def intra_pairs():
        zeros = jnp.zeros((chunk, LIN_HEAD_DIM), F32)
        tmod = lax.broadcasted_iota(jnp.int32, (t_tile, 1), 0) & (chunk - 1)
        for h in range(heads):
            for dst, src in ((kkp_scr, kk_ref[0, :, hs[h]]), (bp_scr, b_scr[h]), (vp_scr, v_ref[0, :, hs[h]])):
                dst[0:chunk, :] = zeros
                dst[chunk:, :] = src
            oi_scr[h] = jnp.zeros((t_tile, LIN_HEAD_DIM), F32)
            q = q_ref[0, :, hs[h]]
            b = b_scr[h]

            def lag(d, carry, h=h, q=q, b=b):
                start = chunk - d
                w = q * kkp_scr[pl.ds(start, t_tile), :] * jnp.exp(
                    jnp.minimum(b - bp_scr[pl.ds(start, t_tile), :], 0.0))
                a = jnp.sum(w, axis=-1, keepdims=True)
                oi_scr[h] += jnp.where(tmod >= d, a, 0.0) * vp_scr[pl.ds(start, t_tile), :]
                return carry

            lax.fori_loop(0, chunk, lag, 0)

    lax.cond(jnp.min(b_scr[...]) >= -SAFE_DECAY, intra_mxu, intra_pairs)

    for h in range(heads):
        for n in range(n_chunks):
            rows = slice(n * chunk, (n + 1) * chunk)
            b_last = b_scr[h, (n + 1) * chunk - 1:(n + 1) * chunk, :]
            kt = (kk_ref[0, rows, hs[h]] * jnp.exp(b_last - b_scr[h, rows, :])).astype(BF16)
            u_scr[h, n] = lax.dot_general(v_ref[0, rows, hs[h]].astype(BF16), kt, TN_DIMS,
                                          preferred_element_type=F32)

    st = [st_scr[h] for h in range(heads)]
    for n in range(n_chunks):
        rows = slice(n * chunk, (n + 1) * chunk)
        for h in range(heads):
            b_last = b_scr[h, (n + 1) * chunk - 1:(n + 1) * chunk, :]
            o = lax.dot_general((q_ref[0, rows, hs[h]] * jnp.exp(b_scr[h, rows, :])).astype(BF16),
                                st[h].astype(BF16), NT_DIMS, preferred_element_type=F32) + oi_scr[h, rows, :]
            st[h] = st[h] * jnp.exp(b_last) + u_scr[h, n]
            o_ref[0, rows, hs[h]] = (_rms(o, glin_ref[...]) * gl_ref[0, rows, hs[h]]).astype(o_ref.dtype)
    for h in range(heads):
        st_scr[h] = st[h]

    @pl.when(it == pl.num_programs(1) - 1)
    def _():
        for h in range(heads):
            sout_ref[0, h] = st[h].T


def _hgrn_tiles(q, g, kk, v, gl, s0, glin, *, chunk, t_tile):
    nb, t, d_lin = q.shape
    heads = d_lin // LIN_HEAD_DIM
    assert LANES % chunk == 0 and t_tile % LANES == 0 and t % t_tile == 0
    seq_spec = pl.BlockSpec((1, t_tile, d_lin), lambda b, i: (b, i, 0))
    st_spec = pl.BlockSpec((1, heads, LIN_HEAD_DIM, LIN_HEAD_DIM), lambda b, i: (b, 0, 0, 0))
    tile = pltpu.VMEM((heads, t_tile, LIN_HEAD_DIM), F32)
    halo = pltpu.VMEM((chunk + t_tile, LIN_HEAD_DIM), F32)
    return pl.pallas_call(
        functools.partial(_hgrn_tile_body, chunk=chunk, t_tile=t_tile, heads=heads),
        grid=(nb, t // t_tile),
        in_specs=[seq_spec] * 5 + [st_spec, pl.BlockSpec((1, LIN_HEAD_DIM), lambda b, i: (0, 0))],
        out_specs=[seq_spec, st_spec],
        out_shape=[jax.ShapeDtypeStruct((nb, t, d_lin), BF16),
                   jax.ShapeDtypeStruct(s0.shape, F32)],
        scratch_shapes=[pltpu.VMEM((heads, LIN_HEAD_DIM, LIN_HEAD_DIM), F32), tile, tile,
                        pltpu.VMEM((heads, t_tile // chunk, LIN_HEAD_DIM, LIN_HEAD_DIM), F32),
                        halo, halo, halo],
        compiler_params=pltpu.CompilerParams(
            dimension_semantics=("arbitrary", "arbitrary"), vmem_limit_bytes=VMEM_LIMIT),
        name="hgrn_tiles",
    )(q, g, kk, v, gl, s0, glin)


def _t5_first_distances():
    max_exact = N_BUCKETS // 2
    n = np.arange(0, MAX_DISTANCE + 1, dtype=np.int32)
    ratio = np.maximum(n, 1).astype(np.float32) / np.float32(max_exact)
    large = max_exact + (np.log(ratio) / np.float32(math.log(MAX_DISTANCE / max_exact))
                         * np.float32(N_BUCKETS - max_exact)).astype(np.int32)
    bucket = np.where(n < max_exact, n, np.minimum(large, N_BUCKETS - 1))
    assert np.all(np.diff(bucket) >= 0) and bucket[-1] == N_BUCKETS - 1
    return [int(np.argmax(bucket >= bk)) for bk in range(N_BUCKETS)]


_T5_FIRST_DISTANCE = _t5_first_distances()


def _t5_bias(dist, rb_ref, head):
    n = jnp.maximum(dist, 0)
    out = jnp.full(dist.shape, rb_ref[0, head], F32)
    for bk in range(1, N_BUCKETS):
        out = jnp.where(n >= _T5_FIRST_DISTANCE[bk], rb_ref[bk, head], out)
    return out


def _prompt_bias_body(rb_ref, o_ref):
    h = pl.program_id(0)
    d = pl.program_id(1)
    x = lax.broadcasted_iota(jnp.int32, (SUBLANES, 2 * MOBA_BLOCK), 1)
    dist = (d - 1) * MOBA_BLOCK + x
    by_dist = jnp.where(dist >= 0, _t5_bias(dist, rb_ref, h) * LOG2E, NEG)
    wide = jnp.broadcast_to(by_dist[0:1, :], (MOBA_BLOCK, 2 * MOBA_BLOCK))
    o_ref[0, 0] = pltpu.roll(wide, 0, 1, stride=1, stride_axis=0)[:, MOBA_BLOCK:]


def _prompt_bias(rel_bias, n_tiles):
    heads = rel_bias.shape[1]
    return pl.pallas_call(
        _prompt_bias_body,
        grid=(heads, n_tiles),
        in_specs=[pl.BlockSpec(memory_space=pltpu.SMEM)],
        out_specs=pl.BlockSpec((1, 1, MOBA_BLOCK, MOBA_BLOCK), lambda h, d: (h, d, 0, 0)),
        out_shape=jax.ShapeDtypeStruct((heads, n_tiles, MOBA_BLOCK, MOBA_BLOCK), F32),
        name="prompt_bias",
    )(rel_bias)


def _sample_bias_body(rb_ref, o_ref, *, past_len, n_valid, rows):
    h = pl.program_id(0)
    width = past_len + LANES
    r = lax.broadcasted_iota(jnp.int32, (rows, width), 0)
    kpos = lax.broadcasted_iota(jnp.int32, (rows, width), 1)
    dist = past_len + r - kpos
    ok = (dist >= 0) & (kpos < past_len + n_valid)
    o_ref[...] = jnp.where(ok, _t5_bias(dist, rb_ref, h), NEG)


def _sample_bias(rel_bias, *, past_len, n_valid, rows):
    heads = rel_bias.shape[1]
    width = past_len + LANES
    return pl.pallas_call(
        functools.partial(_sample_bias_body, past_len=past_len, n_valid=n_valid, rows=rows),
        grid=(heads,),
        in_specs=[pl.BlockSpec(memory_space=pltpu.SMEM)],
        out_specs=pl.BlockSpec((rows, width), lambda h: (h, 0)),
        out_shape=jax.ShapeDtypeStruct((heads * rows, width), F32),
        name="sample_bias",
    )(rel_bias)


def _topk_keep(gate, idx, own, axis, n):
    past = idx < own
    gm = jnp.where(past, gate, -jnp.inf)
    rank = jnp.zeros(gate.shape, jnp.int32)
    for jp in range(n):
        gj = lax.slice_in_dim(gm, jp, jp + 1, axis=axis)
        beats = (gj > gm) | ((gj == gm) & (jp < idx))
        rank = rank + beats.astype(jnp.int32)
    keep = ((rank < MOBA_TOPK) & past) | (idx == own)
    return jnp.where(keep, 0.0, NEG)


def _moba_gate_body(qt_ref, k_ref, a_ref, km_scr, *, n_blocks):
    t = n_blocks * MOBA_BLOCK
    qt = qt_ref[0]
    lane = lax.broadcasted_iota(jnp.int32, (1, LANES), 1)
    for n in range(n_blocks):
        km_scr[pl.ds(n, 1), :] = jnp.sum(
            k_ref[0, n * MOBA_BLOCK:(n + 1) * MOBA_BLOCK, :], axis=0, keepdims=True) * (1.0 / MOBA_BLOCK)
    km = km_scr[...]
    blk = lax.broadcasted_iota(jnp.int32, (n_blocks, t), 0)
    own = lax.broadcasted_iota(jnp.int32, (n_blocks, t), 1) >> int(math.log2(MOBA_BLOCK))
    for hh in range(HEADS_PER_TILE):
        kmh = jnp.where(_in_head(lane, hh), km, 0.0)
        gate_t = jnp.dot(kmh, qt, precision=lax.Precision.HIGHEST, preferred_element_type=F32)
        a_ref[0, 0, hh * n_blocks:(hh + 1) * n_blocks, :] = _topk_keep(gate_t, blk, own, axis=0, n=n_blocks)


def _moba_gate(aqt, ak):
    b, d_att, t = aqt.shape
    tiles = d_att // LANES
    n_blocks = t // MOBA_BLOCK
    return pl.pallas_call(
        functools.partial(_moba_gate_body, n_blocks=n_blocks),
        grid=(b, tiles),
        in_specs=[pl.BlockSpec((1, LANES, t), lambda i, p: (i, p, 0)),
                  pl.BlockSpec((1, t, LANES), lambda i, p: (i, 0, p))],
        out_specs=pl.BlockSpec((1, 1, HEADS_PER_TILE * n_blocks, t), lambda i, p: (i, p, 0, 0)),
        out_shape=jax.ShapeDtypeStruct((b, tiles, HEADS_PER_TILE * n_blocks, t), F32),
        scratch_shapes=[pltpu.VMEM((n_blocks, LANES), F32)],
        compiler_params=pltpu.CompilerParams(
            dimension_semantics=("arbitrary", "arbitrary"), vmem_limit_bytes=VMEM_LIMIT),
        name="moba_gate",
    )(aqt, ak)


def _moba_prompt_body(qt_ref, k_ref, vt_ref, keep_ref, bias_ref, ga_ref, o_ref, kaug_scr, vtaug_scr, s_scr,
                      *, n_bias, n_blocks, group, tps):
    i = pl.program_id(2)
    rows = group * MOBA_BLOCK
    t = n_blocks * MOBA_BLOCK
    lg_block = int(math.log2(MOBA_BLOCK))
    feat = lax.broadcasted_iota(jnp.int32, (LANES, 1), 0)
    heads = [(tt, hh) for tt in range(tps) for hh in range(HEADS_PER_TILE)]
    tile = [slice(tt * LANES, (tt + 1) * LANES) for tt in range(tps)]
    ones_row = [((hh + 1) % HEADS_PER_TILE) * ATT_HEAD_DIM for hh in range(HEADS_PER_TILE)]

    @pl.when(i == 0)
    def _():
        blk = lax.broadcasted_iota(jnp.int32, (rows, LANES), 0) >> lg_block
        lane = lax.broadcasted_iota(jnp.int32, (rows, LANES), 1)
        for c in range(t // rows):
            sl = slice(c * rows, (c + 1) * rows)
            for hd, (tt, hh) in enumerate(heads):
                onehot = jnp.where(lane == hh * n_blocks + c * group + blk, 1.0, 0.0)
                kaug_scr[hd, sl, :] = jnp.concatenate([k_ref[0, sl, tile[tt]], onehot], axis=1).astype(BF16)
                vtaug_scr[hd, :, sl] = jnp.where(feat == ones_row[hh], 1.0, vt_ref[0, tile[tt], sl]).astype(BF16)

    q_aug = []
    for tt, hh in heads:
        qt = qt_ref[0, tile[tt], :] * (ATT_HEAD_DIM ** -0.5 * LOG2E)
        keep = keep_ref[0, tt]
        pad = jnp.zeros((LANES - keep.shape[0], MOBA_BLOCK), F32)
        q_aug.append(jnp.concatenate([jnp.where(_in_head(feat, hh), qt, 0.0), keep, pad], axis=0).astype(BF16))
    n_groups = (i >> int(math.log2(group))) + 1
    n_heads = len(heads)

    def scores(gi, m):
        r0 = pl.multiple_of(gi * rows, rows)
        m_new = []
        for hd in range(n_heads):
            s = jnp.dot(kaug_scr[hd, pl.ds(r0, rows), :], q_aug[hd], preferred_element_type=F32)
            mh = m[hd]
            for u in range(group):
                d = jnp.clip(i - (gi * group + u), 0, n_bias - 1)
                su = s[u * MOBA_BLOCK:(u + 1) * MOBA_BLOCK] + bias_ref[hd, d]
                s_scr[hd, pl.ds(r0 + u * MOBA_BLOCK, MOBA_BLOCK), :] = su
                mh = jnp.maximum(mh, jnp.max(su, axis=0, keepdims=True))
            m_new.append(mh)
        return tuple(m_new)

    def values(gi, m, m_before, acc):
        r0 = pl.multiple_of(gi * rows, rows)
        out = []
        for hd in range(n_heads):
            p = jnp.exp2(s_scr[hd, pl.ds(r0, rows), :] - m[hd]).astype(BF16)
            out.append(acc[hd] * jnp.exp2(m_before[hd] - m[hd])
                       + jnp.dot(vtaug_scr[hd, :, pl.ds(r0, rows)], p, preferred_element_type=F32))
        return tuple(out)

    m_init = tuple(jnp.full((1, MOBA_BLOCK), M_INIT, F32) for _ in range(n_heads))
    acc0 = tuple(jnp.zeros((LANES, MOBA_BLOCK), F32) for _ in range(n_heads))

    def stage(gi, carry):
        m, m_before, acc = carry
        acc = values(gi - 1, m, m_before, acc)
        return scores(gi, m), m, acc

    m, m_before, acc = lax.fori_loop(1, n_groups, stage, (scores(0, m_init), m_init, acc0))
    acc = values(n_groups - 1, m, m_before, acc)
    for tt in range(tps):
        out = jnp.zeros((LANES, MOBA_BLOCK), F32)
        for hh in range(HEADS_PER_TILE):
            a = acc[tt * HEADS_PER_TILE + hh]
            out = jnp.where(_in_head(feat, hh), a / a[ones_row[hh]:ones_row[hh] + 1, :], out)
        o_ref[0, :, tile[tt]] = (out.T * ga_ref[0, :, tile[tt]]).astype(o_ref.dtype)


def _moba_prompt(aqt, ak, avt, keep, bias, ga, *, group=4, tps=2):
    b, d_att, t = aqt.shape
    tiles = d_att // LANES
    n_blocks = t // MOBA_BLOCK
    n_bias = bias.shape[1]
    group = min(group, n_blocks)
    assert n_blocks % group == 0 and group & (group - 1) == 0 and tiles % tps == 0
    width = tps * LANES
    n_heads = tps * HEADS_PER_TILE
    once = dict(pipeline_mode=pl.Buffered(1))
    return pl.pallas_call(
        functools.partial(_moba_prompt_body, n_bias=n_bias, n_blocks=n_blocks, group=group, tps=tps),
        grid=(b, tiles // tps, n_blocks),
        in_specs=[pl.BlockSpec((1, width, MOBA_BLOCK), lambda ib, p, i: (ib, p, i)),
                  pl.BlockSpec((1, t, width), lambda ib, p, i: (ib, 0, p), **once),
                  pl.BlockSpec((1, width, t), lambda ib, p, i: (ib, p, 0), **once),
                  pl.BlockSpec((1, tps, keep.shape[2], MOBA_BLOCK), lambda ib, p, i: (ib, p, 0, i)),
                  pl.BlockSpec((n_heads, n_bias, MOBA_BLOCK, MOBA_BLOCK), lambda ib, p, i: (p, 0, 0, 0), **once),
                  pl.BlockSpec((1, MOBA_BLOCK, width), lambda ib, p, i: (ib, i, p))],
        out_specs=pl.BlockSpec((1, MOBA_BLOCK, width), lambda ib, p, i: (ib, i, p)),
        out_shape=jax.ShapeDtypeStruct((b, t, d_att), BF16),
        scratch_shapes=[pltpu.VMEM((n_heads, t, 2 * LANES), BF16),
                        pltpu.VMEM((n_heads, LANES, t), BF16),
                        pltpu.VMEM((n_heads, t, MOBA_BLOCK), F32)],
        compiler_params=pltpu.CompilerParams(
            dimension_semantics=("arbitrary", "arbitrary", "arbitrary"), vmem_limit_bytes=VMEM_LIMIT),
        name="moba_prompt",
    )(aqt, ak, avt, keep, bias, ga.reshape(b, t, d_att))


def _moba_sample_body(pt_ref, q_ref, kn_ref, vn_ref, bias_ref, ck_hbm, cv_hbm, o_ref,
                      kbuf, vbuf, sems, s_scr, *, n_pages, heads, rows):
    b = pl.program_id(0)
    slot = b & 1

    def page_copies(seq, sl):
        out = []
        for p in range(n_pages):
            phys = pt_ref[seq, p]
            out.append(pltpu.make_async_copy(ck_hbm.at[phys], kbuf.at[sl, p], sems.at[0, sl]))
            out.append(pltpu.make_async_copy(cv_hbm.at[phys], vbuf.at[sl, p], sems.at[1, sl]))
        return out

    @pl.when(b == 0)
    def _():
        for cp in page_copies(0, 0):
            cp.start()

    @pl.when(b + 1 < pl.num_programs(0))
    def _():
        for cp in page_copies(b + 1, 1 - slot):
            cp.start()

    for cp in page_copies(b, slot):
        cp.wait()
    k_pages = [kbuf.at[slot, p] for p in range(n_pages)]
    v_pages = [vbuf.at[slot, p] for p in range(n_pages)]
    d_att = heads * ATT_HEAD_DIM
    page = kbuf.shape[-1]
    pages_per_block = MOBA_BLOCK // page
    n_blocks = n_pages // pages_per_block
    n_q = heads * rows
    feat = lax.broadcasted_iota(jnp.int32, (1, d_att), 1)
    lane = lax.broadcasted_iota(jnp.int32, (n_q, LANES), 1)

    q = q_ref[0] * (ATT_HEAD_DIM ** -0.5)
    q_bd = jnp.concatenate([jnp.where(_in_head(feat, h), q, 0.0) for h in range(heads)], axis=0).astype(BF16)

    gate = jnp.zeros((n_q, LANES), F32)
    for n in range(n_blocks):
        tot = jnp.zeros((n_q, page), F32)
        for p in range(n * pages_per_block, (n + 1) * pages_per_block):
            kt = k_pages[p][...].astype(BF16)
            s = jnp.dot(q_bd, kt, preferred_element_type=F32)
            s_scr[:, p * page:(p + 1) * page] = s
            tot = tot + s
        gate = jnp.where(lane == n, jnp.sum(tot, axis=-1, keepdims=True) * (1.0 / MOBA_BLOCK), gate)
    keep = _topk_keep(gate, lane, jnp.full(gate.shape, n_blocks, jnp.int32), axis=1, n=n_blocks)

    def new_rows(ref):
        return jnp.concatenate([ref[0], jnp.zeros((LANES - rows, d_att), F32)], axis=0).astype(BF16)

    own = n_pages * page
    s_own = lax.dot_general(q_bd, new_rows(kn_ref), NT_DIMS, preferred_element_type=F32)
    s_own = s_own + bias_ref[:, own:own + LANES]
    s_scr[:, own:own + LANES] = s_own
    m_wide = s_own
    for p in range(n_pages):
        n = p // pages_per_block
        s = s_scr[:, p * page:(p + 1) * page] + bias_ref[:, p * page:(p + 1) * page] + keep[:, n:n + 1]
        s_scr[:, p * page:(p + 1) * page] = s
        m_wide = jnp.maximum(m_wide, s)
    m = jnp.max(m_wide, axis=-1, keepdims=True)

    pr = jnp.exp(s_scr[:, own:own + LANES] - m)
    l_wide = pr
    acc = jnp.dot(pr.astype(BF16), new_rows(vn_ref), preferred_element_type=F32)
    for p in range(n_pages):
        pr = jnp.exp(s_scr[:, p * page:(p + 1) * page] - m)
        l_wide = l_wide + pr
        vt = v_pages[p][...].astype(BF16)
        acc = acc + lax.dot_general(pr.astype(BF16), vt, NT_DIMS, preferred_element_type=F32)
    acc = acc / jnp.sum(l_wide, axis=-1, keepdims=True)
    out = jnp.zeros((rows, d_att), F32)
    for h in range(heads):
        out = jnp.where(_in_head(feat, h), acc[h * rows:(h + 1) * rows, :], out)
    o_ref[0] = out


def _moba_sample(aq, ak, av, cache_kt, cache_vt, page_table, bias, *, heads):
    nb, rows, d_att = aq.shape
    n_pages = page_table.shape[1]
    page = cache_kt.shape[-1]
    assert cache_kt.shape[1:] == (d_att, page) and MOBA_BLOCK % page == 0 and page == LANES
    tok = pl.BlockSpec((1, rows, d_att), lambda b, pt: (b, 0, 0))
    in_hbm = pl.BlockSpec(memory_space=pl.ANY)
    page_slots = pltpu.VMEM((2, n_pages, d_att, page), F32)
    grid_spec = pltpu.PrefetchScalarGridSpec(
        num_scalar_prefetch=1,
        grid=(nb,),
        in_specs=[tok, tok, tok, pl.BlockSpec(bias.shape, lambda b, pt: (0, 0)), in_hbm, in_hbm],
        out_specs=tok,
        scratch_shapes=[page_slots, page_slots, pltpu.SemaphoreType.DMA((2, 2)), pltpu.VMEM(bias.shape, F32)],
    )
    return pl.pallas_call(
        functools.partial(_moba_sample_body, n_pages=n_pages, heads=heads, rows=rows),
        grid_spec=grid_spec,
        out_shape=jax.ShapeDtypeStruct((nb, rows, d_att), F32),
        compiler_params=pltpu.CompilerParams(
            dimension_semantics=("arbitrary",), vmem_limit_bytes=VMEM_LIMIT),
        name="moba_sample",
    )(page_table, aq, ak, av, bias, cache_kt, cache_vt)


def _out_proj_body(*refs, d_lin, gated):
    if gated:
        ol_ref, oa_ref, x_ref, w_ref, gpost_ref, y_ref = refs
        oa = oa_ref[...]
    else:
        ol_ref, oa_ref, ga_ref, x_ref, w_ref, gpost_ref, y_ref = refs
        oa = oa_ref[...] * ga_ref[...]
    o = jnp.dot(ol_ref[...].astype(BF16), w_ref[0:d_lin, :], preferred_element_type=F32)
    o = o + jnp.dot(oa.astype(BF16), w_ref[d_lin:, :], preferred_element_type=F32)
    y_ref[...] = x_ref[...] + _rms(o, gpost_ref[...])


def _out_proj(o_lin, o_att, ga, x, w_bf, gpost, *, tm=256):
    rows, d_model = x.shape
    tm = min(tm, rows)
    assert rows % tm == 0
    d_lin = o_lin.shape[1]
    row_spec = lambda w: pl.BlockSpec((tm, w), lambda i: (i, 0))
    acts = [o_lin, o_att] + ([] if ga is None else [ga])
    return pl.pallas_call(
        functools.partial(_out_proj_body, d_lin=d_lin, gated=ga is None),
        grid=(rows // tm,),
        in_specs=[row_spec(a.shape[1]) for a in acts] + [
            row_spec(d_model),
            pl.BlockSpec(w_bf.shape, lambda i: (0, 0)),
            pl.BlockSpec((1, d_model), lambda i: (0, 0)),
        ],
        out_specs=row_spec(d_model),
        out_shape=jax.ShapeDtypeStruct((rows, d_model), F32),
        compiler_params=pltpu.CompilerParams(dimension_semantics=("arbitrary",)),
        name="out_proj",
    )(*acts, x, w_bf, gpost)


def kernel(x_prompt, x_sample, cache_k, cache_v, state_hgrn, page_table, w_in, w_out,
           norm_pre, norm_post, norm_lin_out, lin_lower_bound, rel_bias):
    depth = w_in.shape[0]
    assert depth == 1 and lin_lower_bound.shape[0] == depth + 1
    b, t, d_model = x_prompt.shape
    nb, ts, _ = x_sample.shape
    d_lin = lin_lower_bound.shape[1]
    d_att = w_out.shape[1] - d_lin
    lin_heads = d_lin // LIN_HEAD_DIM
    att_heads = rel_bias.shape[1]
    n_pages = page_table.shape[1]
    page = cache_k.shape[2]
    past_len = n_pages * page
    assert d_att == att_heads * ATT_HEAD_DIM and t % MOBA_BLOCK == 0
    assert past_len % MOBA_BLOCK == 0 and ts <= SUBLANES

    w_in_bf = w_in[0].astype(BF16)
    w_out_bf = w_out[0].astype(BF16)
    gpre, gpost, glin = norm_pre, norm_post, norm_lin_out
    proj = functools.partial(_in_proj, gpre=gpre, w_bf=w_in_bf, llb=lin_lower_bound,
                             d_lin=d_lin, d_att=d_att)

    xp = x_prompt.reshape(b * t, d_model)
    q, g, kk, v, gl, ga, ak, aqt, akt, avt = proj(xp, seq_len=t)
    q, g, kk, v, gl, ak = [a.reshape(b, t, -1) for a in (q, g, kk, v, gl, ak)]
    s0 = jnp.zeros((b, lin_heads, LIN_HEAD_DIM, LIN_HEAD_DIM), F32)
    o_lin, s_prompt = _hgrn_tiles(q, g, kk, v, gl, s0, glin, chunk=32, t_tile=512)
    keep = _moba_gate(aqt, ak)
    n_bias = 6
    assert (n_bias - 1) * MOBA_BLOCK - (MOBA_BLOCK - 1) >= MAX_DISTANCE
    bias = _prompt_bias(rel_bias, n_bias)
    o_att = _moba_prompt(aqt, ak, avt, keep, bias, ga)
    y_prompt = _out_proj(o_lin.reshape(b * t, d_lin), o_att.reshape(b * t, d_att),
                         None, xp, w_out_bf, gpost, tm=512).reshape(b, t, d_model)

    rows = SUBLANES
    xs = jnp.pad(x_sample, ((0, 0), (0, rows - ts), (0, 0))).reshape(nb * rows, d_model)
    qs, gs, kks, vs, gls, gas, aqs, aks, avs = [a.reshape(nb, rows, -1) for a in proj(xs)]
    o_lin_s, s_sample = _hgrn_step(qs, gs, kks, vs, gls, state_hgrn[0], glin, valid=ts, n_seq=16)
    bias_s = _sample_bias(rel_bias, past_len=past_len, n_valid=ts, rows=rows)
    cache_kt = cache_k[0].transpose(0, 2, 3, 1).reshape(-1, d_att, page)
    cache_vt = cache_v[0].transpose(0, 2, 3, 1).reshape(-1, d_att, page)
    o_att_s = _moba_sample(aqs, aks, avs, cache_kt, cache_vt, page_table, bias_s, heads=att_heads)
    y_s = _out_proj(o_lin_s.reshape(nb * rows, d_lin), o_att_s.reshape(nb * rows, d_att),
                    gas.reshape(nb * rows, d_att), xs, w_out_bf, gpost)
    y_sample = y_s.reshape(nb, rows, d_model)[:, :ts]

    def prompt_kv(a):
        return a.reshape(1, b, att_heads, ATT_HEAD_DIM, t).transpose(0, 1, 4, 2, 3)

    kvs_shape = (1, nb, ts, att_heads, ATT_HEAD_DIM)
    return (y_prompt, y_sample, prompt_kv(akt), prompt_kv(avt), s_prompt[None],
            aks[:, :ts].reshape(kvs_shape), avs[:, :ts].reshape(kvs_shape), s_sample[None])
```

```python
import functools
import math

import jax
import jax.numpy as jnp
import numpy as np
from jax import lax
from jax.experimental import pallas as pl
from jax.experimental.pallas import tpu as pltpu

F32 = jnp.float32
BF16 = jnp.bfloat16

EPS = 1e-6
LIN_HEAD_DIM = 128
ATT_HEAD_DIM = 64
MOBA_BLOCK = 256
MOBA_TOPK = 3
N_BUCKETS = 32
MAX_DISTANCE = 1024
NEG = -2e30
M_INIT = -1e30
LOG2E = 1.4426950408889634
LANES = 128
SUBLANES = 8
VMEM_LIMIT = 56 * 1024 * 1024
HEADS_PER_TILE = LANES // ATT_HEAD_DIM

NT_DIMS = (((1,), (1,)), ((), ()))
TN_DIMS = (((0,), (0,)), ((), ()))


def _sigmoid(x):
    return 1.0 / (1.0 + jnp.exp(-x))


def _silu(x):
    return x * _sigmoid(x)


def _rms(x, g):
    return x * lax.rsqrt(jnp.mean(x * x, axis=-1, keepdims=True) + EPS) * g


def _in_head(idx, head):
    return (idx >= head * ATT_HEAD_DIM) & (idx < (head + 1) * ATT_HEAD_DIM)


def _in_proj_body(x_ref, gpre_ref, w_ref, llb_ref, *rest, d_lin, d_att, feat_major):
    if feat_major:
        wt_ref, q_o, g_o, kk_o, v_o, gl_o, ga_o, ak_o, aqt_o, akt_o, avt_o = rest
    else:
        q_o, g_o, kk_o, v_o, gl_o, ga_o, aq_o, ak_o, av_o = rest
    h = _rms(x_ref[...], gpre_ref[...]).astype(BF16)

    def proj(col, width):
        return jnp.dot(h, w_ref[:, col:col + width], preferred_element_type=F32)

    def proj_t(row, width):
        return lax.dot_general(wt_ref[row:row + width, :], h, NT_DIMS, preferred_element_type=F32)

    llb = llb_ref[...]
    e = jnp.exp(llb - jnp.max(llb, axis=0, keepdims=True))
    lb = e[0:1, :] / jnp.sum(e, axis=0, keepdims=True)

    q_o[...] = _silu(proj(0, d_lin))
    f = lb + (1.0 - lb) * _sigmoid(proj(d_lin, d_lin))
    g_o[...] = jnp.log(f)
    kk_o[...] = 1.0 - f
    v_o[...] = proj(2 * d_lin, d_lin)
    gl_o[...] = _silu(proj(3 * d_lin, d_lin))
    a0 = 4 * d_lin
    ga_o[...] = _silu(proj(a0 + 3 * d_att, d_att))
    if feat_major:
        aqt_o[0] = proj_t(0, d_att)
        akt = proj_t(d_att, d_att)
        akt_o[0] = akt
        ak_o[...] = akt.T
        avt_o[0] = proj_t(2 * d_att, d_att)
    else:
        aq_o[...] = proj(a0, d_att)
        ak_o[...] = proj(a0 + d_att, d_att)
        av_o[...] = proj(a0 + 2 * d_att, d_att)


def _in_proj(x, gpre, w_bf, llb, *, d_lin, d_att, seq_len=None, tm=256):
    rows, d_model = x.shape
    tm = min(tm, rows)
    assert rows % tm == 0
    d_in = w_bf.shape[1]
    feat_major = seq_len is not None
    row_spec = lambda w: pl.BlockSpec((tm, w), lambda i: (i, 0))
    in_specs = [
        pl.BlockSpec((tm, d_model), lambda i: (i, 0)),
        pl.BlockSpec((1, d_model), lambda i: (0, 0)),
        pl.BlockSpec((d_model, d_in), lambda i: (0, 0)),
        pl.BlockSpec(llb.shape, lambda i: (0, 0)),
    ]
    args = [x, gpre, w_bf, llb]
    out_specs = [row_spec(d_lin)] * 5 + [row_spec(d_att)]
    out_shape = [jax.ShapeDtypeStruct((rows, d_lin), F32)] * 5 + [jax.ShapeDtypeStruct((rows, d_att), F32)]
    if feat_major:
        assert seq_len % tm == 0
        tiles = seq_len // tm
        a0 = 4 * d_lin
        wt_bf = w_bf[:, a0:a0 + 3 * d_att].T
        in_specs.append(pl.BlockSpec(wt_bf.shape, lambda i: (0, 0)))
        args.append(wt_bf)
        t_spec = pl.BlockSpec((1, d_att, tm), lambda i: (i // tiles, 0, i % tiles))
        t_shape = jax.ShapeDtypeStruct((rows // seq_len, d_att, seq_len), F32)
        out_specs += [row_spec(d_att)] + [t_spec] * 3
        out_shape += [jax.ShapeDtypeStruct((rows, d_att), F32)] + [t_shape] * 3
    else:
        out_specs += [row_spec(d_att)] * 3
        out_shape += [jax.ShapeDtypeStruct((rows, d_att), F32)] * 3
    return pl.pallas_call(
        functools.partial(_in_proj_body, d_lin=d_lin, d_att=d_att, feat_major=feat_major),
        grid=(rows // tm,),
        in_specs=in_specs,
        out_specs=out_specs,
        out_shape=out_shape,
        compiler_params=pltpu.CompilerParams(
            dimension_semantics=("arbitrary",), vmem_limit_bytes=VMEM_LIMIT),
        name="in_proj",
    )(*args)


def _hgrn_step_body(q_ref, g_ref, kk_ref, v_ref, gl_ref, s0_ref, glin_ref, o_ref, sout_ref, b_scr,
                    *, rows, valid, n_seq):
    row = lax.broadcasted_iota(jnp.int32, (rows, 1), 0)
    for s in range(n_seq):
        acc = jnp.zeros((1, LIN_HEAD_DIM), F32)
        for r in range(rows):
            if r < valid:
                acc = acc + g_ref[s, r:r + 1, :]
            b_scr[s, r:r + 1, :] = acc
    seqs = range(n_seq)
    b = [b_scr[s] for s in seqs]
    b_last = [b_scr[s, rows - 1:rows, :] for s in seqs]
    q = [q_ref[s] for s in seqs]
    kk = [jnp.where(row < valid, kk_ref[s], 0.0) for s in seqs]

    o = [jnp.dot((q[s] * jnp.exp(b[s])).astype(BF16), s0_ref[s, 0].astype(BF16), preferred_element_type=F32)
         for s in seqs]
    for s in seqs:
        for j in range(valid):
            w = q[s] * kk_ref[s, j:j + 1, :] * jnp.exp(jnp.minimum(b[s] - b_scr[s, j:j + 1, :], 0.0))
            a = jnp.sum(w, axis=-1, keepdims=True)
            o[s] = o[s] + jnp.where(row >= j, a, 0.0) * v_ref[s, j:j + 1, :]
    u = [lax.dot_general((kk[s] * jnp.exp(b_last[s] - b[s])).astype(BF16), v_ref[s].astype(BF16), TN_DIMS,
                         preferred_element_type=F32) for s in seqs]
    ones = jnp.ones((rows, LIN_HEAD_DIM), BF16)
    decay = []
    for s in seqs:
        rest = jnp.exp(b_last[s])
        pieces = jnp.zeros((rows, LIN_HEAD_DIM), F32)
        for pi in range(3):
            piece = rest.astype(BF16).astype(F32)
            rest = rest - piece
            pieces = jnp.where(row == pi, piece, pieces)
        decay.append(lax.dot_general(pieces.astype(BF16), ones, TN_DIMS, preferred_element_type=F32))
    for s in seqs:
        sout_ref[s, 0] = s0_ref[s, 0] * decay[s] + u[s]
        o_ref[s] = _rms(o[s], glin_ref[...]) * gl_ref[s]


def _hgrn_step(q, g, kk, v, gl, s0, glin, *, valid, n_seq):
    nb, rows, d_lin = q.shape
    heads = d_lin // LIN_HEAD_DIM
    assert rows >= 3 and nb % n_seq == 0
    seq_spec = pl.BlockSpec((n_seq, rows, LIN_HEAD_DIM), lambda b, h: (b, 0, h))
    st_spec = pl.BlockSpec((n_seq, 1, LIN_HEAD_DIM, LIN_HEAD_DIM), lambda b, h: (b, h, 0, 0))
    return pl.pallas_call(
        functools.partial(_hgrn_step_body, rows=rows, valid=valid, n_seq=n_seq),
        grid=(nb // n_seq, heads),
        in_specs=[seq_spec] * 5 + [st_spec, pl.BlockSpec((1, LIN_HEAD_DIM), lambda b, h: (0, 0))],
        out_specs=[seq_spec, st_spec],
        out_shape=[jax.ShapeDtypeStruct((nb, rows, d_lin), F32),
                   jax.ShapeDtypeStruct(s0.shape, F32)],
        scratch_shapes=[pltpu.VMEM((n_seq, rows, LIN_HEAD_DIM), F32)],
        compiler_params=pltpu.CompilerParams(dimension_semantics=("arbitrary", "arbitrary")),
        name="hgrn_step",
    )(q, g, kk, v, gl, s0, glin)


SAFE_DECAY = 60.0


def _hgrn_tile_body(q_ref, g_ref, kk_ref, v_ref, gl_ref, s0_ref, glin_ref, o_ref, sout_ref,
                    st_scr, b_scr, oi_scr, u_scr, kkp_scr, bp_scr, vp_scr, *, chunk, t_tile, heads):
    it = pl.program_id(1)
    hs = [slice(h * LIN_HEAD_DIM, (h + 1) * LIN_HEAD_DIM) for h in range(heads)]

    @pl.when(it == 0)
    def _():
        for h in range(heads):
            st_scr[h] = s0_ref[0, h].T

    lg = int(math.log2(chunk))
    n_groups = t_tile // LANES
    n_chunks = t_tile // chunk
    r = lax.broadcasted_iota(jnp.int32, (LANES, LANES), 0)
    c = lax.broadcasted_iota(jnp.int32, (LANES, LANES), 1)
    causal = ((r >> lg) == (c >> lg)) & (c <= r)
    tri = jnp.where(causal, 1.0, 0.0).astype(BF16)
    n_pieces = 3
    for h in range(heads):
        pieces = []
        for gi in range(n_groups):
            rest = g_ref[0, gi * LANES:(gi + 1) * LANES, hs[h]]
            for _ in range(n_pieces):
                piece = rest.astype(BF16)
                rest = rest - piece.astype(F32)
                pieces.append(piece)
        sums = jnp.dot(tri, jnp.concatenate(pieces, axis=1), preferred_element_type=F32)
        for gi in range(n_groups):
            b = jnp.zeros((LANES, LIN_HEAD_DIM), F32)
            for pi in range(n_pieces):
                col = (gi * n_pieces + pi) * LIN_HEAD_DIM
                b = b + sums[:, col:col + LIN_HEAD_DIM]
            b_scr[h, gi * LANES:(gi + 1) * LANES, :] = b

    def intra_mxu():
        atts = {}
        for h in range(heads):
            for gi in range(n_groups):
                rows = slice(gi * LANES, (gi + 1) * LANES)
                b = b_scr[h, rows, :]
                qe = (q_ref[0, rows, hs[h]] * jnp.exp(b)).astype(BF16)
                ke = (kk_ref[0, rows, hs[h]] * jnp.exp(-b)).astype(BF16)
                att = lax.dot_general(qe, ke, NT_DIMS, preferred_element_type=F32)
                atts[h, gi] = jnp.where(causal, att, 0.0).astype(BF16)
        for h in range(heads):
            for gi in range(n_groups):
                rows = slice(gi * LANES, (gi + 1) * LANES)
                oi_scr[h, rows, :] = jnp.dot(atts[h, gi], v_ref[0, rows, hs[h]].astype(BF16),
                                             preferred_element_type=F32)

    def intra_pairs():
        zeros = jnp.zeros((chunk, LIN_HEAD_DIM), F32)
        tmod = lax.broadcasted_iota(jnp.int32, (t_tile, 1), 0) & (chunk - 1)
        for h in range(heads):
            for dst, src in ((kkp_scr, kk_ref[0, :, hs[h]]), (bp_scr, b_scr[h]), (vp_scr, v_ref[0, :, hs[h]])):
                dst[0:chunk, :] = zeros
                dst[chunk:, :] = src
            oi_scr[h] = jnp.zeros((t_tile, LIN_HEAD_DIM), F32)
            q = q_ref[0, :, hs[h]]
            b = b_scr[h]

            def lag(d, carry, h=h, q=q, b=b):
                start = chunk - d
                w = q * kkp_scr[pl.ds(start, t_tile), :] * jnp.exp(
                    jnp.minimum(b - bp_scr[pl.ds(start, t_tile), :], 0.0))
                a = jnp.sum(w, axis=-1, keepdims=True)
                oi_scr[h] += jnp.where(tmod >= d, a, 0.0) * vp_scr[pl.ds(start, t_tile), :]
                return carry

            lax.fori_loop(0, chunk, lag, 0)

    lax.cond(jnp.min(b_scr[...]) >= -SAFE_DECAY, intra_mxu, intra_pairs)

    for h in range(heads):
        for n in range(n_chunks):
            rows = slice(n * chunk, (n + 1) * chunk)
            b_last = b_scr[h, (n + 1) * chunk - 1:(n + 1) * chunk, :]
            kt = (kk_ref[0, rows, hs[h]] * jnp.exp(b_last - b_scr[h, rows, :])).astype(BF16)
            u_scr[h, n] = lax.dot_general(v_ref[0, rows, hs[h]].astype(BF16), kt, TN_DIMS,
                                          preferred_element_type=F32)

    st = [st_scr[h] for h in range(heads)]
    for n in range(n_chunks):
        rows = slice(n * chunk, (n + 1) * chunk)
        for h in range(heads):
            b_last = b_scr[h, (n + 1) * chunk - 1:(n + 1) * chunk, :]
            o = lax.dot_general((q_ref[0, rows, hs[h]] * jnp.exp(b_scr[h, rows, :])).astype(BF16),
                                st[h].astype(BF16), NT_DIMS, preferred_element_type=F32) + oi_scr[h, rows, :]
            st[h] = st[h] * jnp.exp(b_last) + u_scr[h, n]
            o_ref[0, rows, hs[h]] = (_rms(o, glin_ref[...]) * gl_ref[0, rows, hs[h]]).astype(o_ref.dtype)
    for h in range(heads):
        st_scr[h] = st[h]

    @pl.when(it == pl.num_programs(1) - 1)
    def _():
        for h in range(heads):
            sout_ref[0, h] = st[h].T


def _hgrn_tiles(q, g, kk, v, gl, s0, glin, *, chunk, t_tile):
    nb, t, d_lin = q.shape
    heads = d_lin // LIN_HEAD_DIM
    assert LANES % chunk == 0 and t_tile % LANES == 0 and t % t_tile == 0
    seq_spec = pl.BlockSpec((1, t_tile, d_lin), lambda b, i: (b, i, 0))
    st_spec = pl.BlockSpec((1, heads, LIN_HEAD_DIM, LIN_HEAD_DIM), lambda b, i: (b, 0, 0, 0))
    tile = pltpu.VMEM((heads, t_tile, LIN_HEAD_DIM), F32)
    halo = pltpu.VMEM((chunk + t_tile, LIN_HEAD_DIM), F32)
    return pl.pallas_call(
        functools.partial(_hgrn_tile_body, chunk=chunk, t_tile=t_tile, heads=heads),
        grid=(nb, t // t_tile),
        in_specs=[seq_spec] * 5 + [st_spec, pl.BlockSpec((1, LIN_HEAD_DIM), lambda b, i: (0, 0))],
        out_specs=[seq_spec, st_spec],
        out_shape=[jax.ShapeDtypeStruct((nb, t, d_lin), BF16),
                   jax.ShapeDtypeStruct(s0.shape, F32)],
        scratch_shapes=[pltpu.VMEM((heads, LIN_HEAD_DIM, LIN_HEAD_DIM), F32), tile, tile,
                        pltpu.VMEM((heads, t_tile // chunk, LIN_HEAD_DIM, LIN_HEAD_DIM), F32),
                        halo, halo, halo],
        compiler_params=pltpu.CompilerParams(
            dimension_semantics=("arbitrary", "arbitrary"), vmem_limit_bytes=VMEM_LIMIT),
        name="hgrn_tiles",
    )(q, g, kk, v, gl, s0, glin)


def _t5_first_distances():
    max_exact = N_BUCKETS // 2
    n = np.arange(0, MAX_DISTANCE + 1, dtype=np.int32)
    ratio = np.maximum(n, 1).astype(np.float32) / np.float32(max_exact)
    large = max_exact + (np.log(ratio) / np.float32(math.log(MAX_DISTANCE / max_exact))
                         * np.float32(N_BUCKETS - max_exact)).astype(np.int32)
    bucket = np.where(n < max_exact, n, np.minimum(large, N_BUCKETS - 1))
    assert np.all(np.diff(bucket) >= 0) and bucket[-1] == N_BUCKETS - 1
    return [int(np.argmax(bucket >= bk)) for bk in range(N_BUCKETS)]


_T5_FIRST_DISTANCE = _t5_first_distances()


def _t5_bias(dist, rb_ref, head):
    n = jnp.maximum(dist, 0)
    out = jnp.full(dist.shape, rb_ref[0, head], F32)
    for bk in range(1, N_BUCKETS):
        out = jnp.where(n >= _T5_FIRST_DISTANCE[bk], rb_ref[bk, head], out)
    return out


def _prompt_bias_body(rb_ref, o_ref):
    h = pl.program_id(0)
    d = pl.program_id(1)
    x = lax.broadcasted_iota(jnp.int32, (SUBLANES, 2 * MOBA_BLOCK), 1)
    dist = (d - 1) * MOBA_BLOCK + x
    by_dist = jnp.where(dist >= 0, _t5_bias(dist, rb_ref, h) * LOG2E, NEG)
    wide = jnp.broadcast_to(by_dist[0:1, :], (MOBA_BLOCK, 2 * MOBA_BLOCK))
    o_ref[0, 0] = pltpu.roll(wide, 0, 1, stride=1, stride_axis=0)[:, MOBA_BLOCK:]


def _prompt_bias(rel_bias, n_tiles):
    heads = rel_bias.shape[1]
    return pl.pallas_call(
        _prompt_bias_body,
        grid=(heads, n_tiles),
        in_specs=[pl.BlockSpec(memory_space=pltpu.SMEM)],
        out_specs=pl.BlockSpec((1, 1, MOBA_BLOCK, MOBA_BLOCK), lambda h, d: (h, d, 0, 0)),
        out_shape=jax.ShapeDtypeStruct((heads, n_tiles, MOBA_BLOCK, MOBA_BLOCK), F32),
        name="prompt_bias",
    )(rel_bias)


def _sample_bias_body(rb_ref, o_ref, *, past_len, n_valid, rows):
    h = pl.program_id(0)
    width = past_len + LANES
    r = lax.broadcasted_iota(jnp.int32, (rows, width), 0)
    kpos = lax.broadcasted_iota(jnp.int32, (rows, width), 1)
    dist = past_len + r - kpos
    ok = (dist >= 0) & (kpos < past_len + n_valid)
    o_ref[...] = jnp.where(ok, _t5_bias(dist, rb_ref, h), NEG)


def _sample_bias(rel_bias, *, past_len, n_valid, rows):
    heads = rel_bias.shape[1]
    width = past_len + LANES
    return pl.pallas_call(
        functools.partial(_sample_bias_body, past_len=past_len, n_valid=n_valid, rows=rows),
        grid=(heads,),
        in_specs=[pl.BlockSpec(memory_space=pltpu.SMEM)],
        out_specs=pl.BlockSpec((rows, width), lambda h: (h, 0)),
        out_shape=jax.ShapeDtypeStruct((heads * rows, width), F32),
        name="sample_bias",
    )(rel_bias)


def _topk_keep(gate, idx, own, axis, n):
    past = idx < own
    gm = jnp.where(past, gate, -jnp.inf)
    rank = jnp.zeros(gate.shape, jnp.int32)
    for jp in range(n):
        gj = lax.slice_in_dim(gm, jp, jp + 1, axis=axis)
        beats = (gj > gm) | ((gj == gm) & (jp < idx))
        rank = rank + beats.astype(jnp.int32)
    keep = ((rank < MOBA_TOPK) & past) | (idx == own)
    return jnp.where(keep, 0.0, NEG)


def _moba_gate_body(qt_ref, k_ref, a_ref, km_scr, *, n_blocks):
    t = n_blocks * MOBA_BLOCK
    qt = qt_ref[0]
    lane = lax.broadcasted_iota(jnp.int32, (1, LANES), 1)
    for n in range(n_blocks):
        km_scr[pl.ds(n, 1), :] = jnp.sum(
            k_ref[0, n * MOBA_BLOCK:(n + 1) * MOBA_BLOCK, :], axis=0, keepdims=True) * (1.0 / MOBA_BLOCK)
    km = km_scr[...]
    blk = lax.broadcasted_iota(jnp.int32, (n_blocks, t), 0)
    own = lax.broadcasted_iota(jnp.int32, (n_blocks, t), 1) >> int(math.log2(MOBA_BLOCK))
    for hh in range(HEADS_PER_TILE):
        kmh = jnp.where(_in_head(lane, hh), km, 0.0)
        gate_t = jnp.dot(kmh, qt, precision=lax.Precision.HIGHEST, preferred_element_type=F32)
        a_ref[0, 0, hh * n_blocks:(hh + 1) * n_blocks, :] = _topk_keep(gate_t, blk, own, axis=0, n=n_blocks)


def _moba_gate(aqt, ak):
    b, d_att, t = aqt.shape
    tiles = d_att // LANES
    n_blocks = t // MOBA_BLOCK
    return pl.pallas_call(
        functools.partial(_moba_gate_body, n_blocks=n_blocks),
        grid=(b, tiles),
        in_specs=[pl.BlockSpec((1, LANES, t), lambda i, p: (i, p, 0)),
                  pl.BlockSpec((1, t, LANES), lambda i, p: (i, 0, p))],
        out_specs=pl.BlockSpec((1, 1, HEADS_PER_TILE * n_blocks, t), lambda i, p: (i, p, 0, 0)),
        out_shape=jax.ShapeDtypeStruct((b, tiles, HEADS_PER_TILE * n_blocks, t), F32),
        scratch_shapes=[pltpu.VMEM((n_blocks, LANES), F32)],
        compiler_params=pltpu.CompilerParams(
            dimension_semantics=("arbitrary", "arbitrary"), vmem_limit_bytes=VMEM_LIMIT),
        name="moba_gate",
    )(aqt, ak)


def _moba_prompt_body(qt_ref, k_ref, vt_ref, keep_ref, bias_ref, ga_ref, o_ref, kaug_scr, vtaug_scr, s_scr,
                      *, n_bias, n_blocks, group, tps):
    i = pl.program_id(2)
    rows = group * MOBA_BLOCK
    t = n_blocks * MOBA_BLOCK
    lg_block = int(math.log2(MOBA_BLOCK))
    feat = lax.broadcasted_iota(jnp.int32, (LANES, 1), 0)
    heads = [(tt, hh) for tt in range(tps) for hh in range(HEADS_PER_TILE)]
    tile = [slice(tt * LANES, (tt + 1) * LANES) for tt in range(tps)]
    ones_row = [((hh + 1) % HEADS_PER_TILE) * ATT_HEAD_DIM for hh in range(HEADS_PER_TILE)]

    @pl.when(i == 0)
    def _():
        blk = lax.broadcasted_iota(jnp.int32, (rows, LANES), 0) >> lg_block
        lane = lax.broadcasted_iota(jnp.int32, (rows, LANES), 1)
        for c in range(t // rows):
            sl = slice(c * rows, (c + 1) * rows)
            for hd, (tt, hh) in enumerate(heads):
                onehot = jnp.where(lane == hh * n_blocks + c * group + blk, 1.0, 0.0)
                kaug_scr[hd, sl, :] = jnp.concatenate([k_ref[0, sl, tile[tt]], onehot], axis=1).astype(BF16)
                vtaug_scr[hd, :, sl] = jnp.where(feat == ones_row[hh], 1.0, vt_ref[0, tile[tt], sl]).astype(BF16)

    q_aug = []
    for tt, hh in heads:
        qt = qt_ref[0, tile[tt], :] * (ATT_HEAD_DIM ** -0.5 * LOG2E)
        keep = keep_ref[0, tt]
        pad = jnp.zeros((LANES - keep.shape[0], MOBA_BLOCK), F32)
        q_aug.append(jnp.concatenate([jnp.where(_in_head(feat, hh), qt, 0.0), keep, pad], axis=0).astype(BF16))
    n_groups = (i >> int(math.log2(group))) + 1
    n_heads = len(heads)

    def scores(gi, m):
        r0 = pl.multiple_of(gi * rows, rows)
        m_new = []
        for hd in range(n_heads):
            s = jnp.dot(kaug_scr[hd, pl.ds(r0, rows), :], q_aug[hd], preferred_element_type=F32)
            mh = m[hd]
            for u in range(group):
                d = jnp.clip(i - (gi * group + u), 0, n_bias - 1)
                su = s[u * MOBA_BLOCK:(u + 1) * MOBA_BLOCK] + bias_ref[hd, d]
                s_scr[hd, pl.ds(r0 + u * MOBA_BLOCK, MOBA_BLOCK), :] = su
                mh = jnp.maximum(mh, jnp.max(su, axis=0, keepdims=True))
            m_new.append(mh)
        return tuple(m_new)

    def values(gi, m, m_before, acc):
        r0 = pl.multiple_of(gi * rows, rows)
        out = []
        for hd in range(n_heads):
            p = jnp.exp2(s_scr[hd, pl.ds(r0, rows), :] - m[hd]).astype(BF16)
            out.append(acc[hd] * jnp.exp2(m_before[hd] - m[hd])
                       + jnp.dot(vtaug_scr[hd, :, pl.ds(r0, rows)], p, preferred_element_type=F32))
        return tuple(out)

    m_init = tuple(jnp.full((1, MOBA_BLOCK), M_INIT, F32) for _ in range(n_heads))
    acc0 = tuple(jnp.zeros((LANES, MOBA_BLOCK), F32) for _ in range(n_heads))

    def stage(gi, carry):
        m, m_before, acc = carry
        acc = values(gi - 1, m, m_before, acc)
        return scores(gi, m), m, acc

    m, m_before, acc = lax.fori_loop(1, n_groups, stage, (scores(0, m_init), m_init, acc0))
    acc = values(n_groups - 1, m, m_before, acc)
    for tt in range(tps):
        out = jnp.zeros((LANES, MOBA_BLOCK), F32)
        for hh in range(HEADS_PER_TILE):
            a = acc[tt * HEADS_PER_TILE + hh]
            out = jnp.where(_in_head(feat, hh), a / a[ones_row[hh]:ones_row[hh] + 1, :], out)
        o_ref[0, :, tile[tt]] = (out.T * ga_ref[0, :, tile[tt]]).astype(o_ref.dtype)


def _moba_prompt(aqt, ak, avt, keep, bias, ga, *, group=4, tps=2):
    b, d_att, t = aqt.shape
    tiles = d_att // LANES
    n_blocks = t // MOBA_BLOCK
    n_bias = bias.shape[1]
    group = min(group, n_blocks)
    assert n_blocks % group == 0 and group & (group - 1) == 0 and tiles % tps == 0
    width = tps * LANES
    n_heads = tps * HEADS_PER_TILE
    once = dict(pipeline_mode=pl.Buffered(1))
    return pl.pallas_call(
        functools.partial(_moba_prompt_body, n_bias=n_bias, n_blocks=n_blocks, group=group, tps=tps),
        grid=(b, tiles // tps, n_blocks),
        in_specs=[pl.BlockSpec((1, width, MOBA_BLOCK), lambda ib, p, i: (ib, p, i)),
                  pl.BlockSpec((1, t, width), lambda ib, p, i: (ib, 0, p), **once),
                  pl.BlockSpec((1, width, t), lambda ib, p, i: (ib, p, 0), **once),
                  pl.BlockSpec((1, tps, keep.shape[2], MOBA_BLOCK), lambda ib, p, i: (ib, p, 0, i)),
                  pl.BlockSpec((n_heads, n_bias, MOBA_BLOCK, MOBA_BLOCK), lambda ib, p, i: (p, 0, 0, 0), **once),
                  pl.BlockSpec((1, MOBA_BLOCK, width), lambda ib, p, i: (ib, i, p))],
        out_specs=pl.BlockSpec((1, MOBA_BLOCK, width), lambda ib, p, i: (ib, i, p)),
        out_shape=jax.ShapeDtypeStruct((b, t, d_att), BF16),
        scratch_shapes=[pltpu.VMEM((n_heads, t, 2 * LANES), BF16),
                        pltpu.VMEM((n_heads, LANES, t), BF16),
                        pltpu.VMEM((n_heads, t, MOBA_BLOCK), F32)],
        compiler_params=pltpu.CompilerParams(
            dimension_semantics=("arbitrary", "arbitrary", "arbitrary"), vmem_limit_bytes=VMEM_LIMIT),
        name="moba_prompt",
    )(aqt, ak, avt, keep, bias, ga.reshape(b, t, d_att))


def _moba_sample_body(pt_ref, q_ref, kn_ref, vn_ref, bias_ref, ck_hbm, cv_hbm, o_ref,
                      kbuf, vbuf, sems, s_scr, *, n_pages, heads, rows):
    b = pl.program_id(0)
    n_slots = kbuf.shape[0]
    ahead = n_slots - 1
    slot = lax.rem(b, n_slots)

    def page_copies(seq, sl):
        out = []
        for p in range(n_pages):
            phys = pt_ref[seq, p]
            out.append(pltpu.make_async_copy(ck_hbm.at[phys], kbuf.at[sl, p], sems.at[0, sl]))
            out.append(pltpu.make_async_copy(cv_hbm.at[phys], vbuf.at[sl, p], sems.at[1, sl]))
        return out

    @pl.when(b == 0)
    def _():
        for seq in range(ahead):
            for cp in page_copies(seq, seq):
                cp.start()

    @pl.when(b + ahead < pl.num_programs(0))
    def _():
        for cp in page_copies(b + ahead, lax.rem(b + ahead, n_slots)):
            cp.start()

    for cp in page_copies(b, slot):
        cp.wait()
    k_pages = [kbuf.at[slot, p] for p in range(n_pages)]
    v_pages = [vbuf.at[slot, p] for p in range(n_pages)]
    d_att = heads * ATT_HEAD_DIM
    page = kbuf.shape[-1]
    pages_per_block = MOBA_BLOCK // page
    n_blocks = n_pages // pages_per_block
    n_q = heads * rows
    feat = lax.broadcasted_iota(jnp.int32, (1, d_att), 1)
    lane = lax.broadcasted_iota(jnp.int32, (n_q, LANES), 1)

    q = q_ref[0] * (ATT_HEAD_DIM ** -0.5)
    q_bd = jnp.concatenate([jnp.where(_in_head(feat, h), q, 0.0) for h in range(heads)], axis=0).astype(BF16)

    gate = jnp.zeros((n_q, LANES), F32)
    for n in range(n_blocks):
        tot = jnp.zeros((n_q, page), F32)
        for p in range(n * pages_per_block, (n + 1) * pages_per_block):
            kt = k_pages[p][...].astype(BF16)
            s = jnp.dot(q_bd, kt, preferred_element_type=F32)
            s_scr[:, p * page:(p + 1) * page] = s
            tot = tot + s
        gate = jnp.where(lane == n, jnp.sum(tot, axis=-1, keepdims=True) * (1.0 / MOBA_BLOCK), gate)
    keep = _topk_keep(gate, lane, jnp.full(gate.shape, n_blocks, jnp.int32), axis=1, n=n_blocks)

    def new_rows(ref):
        return jnp.concatenate([ref[0], jnp.zeros((LANES - rows, d_att), F32)], axis=0).astype(BF16)

    own = n_pages * page
    s_own = lax.dot_general(q_bd, new_rows(kn_ref), NT_DIMS, preferred_element_type=F32)
    s_own = s_own + bias_ref[:, own:own + LANES]
    s_scr[:, own:own + LANES] = s_own
    m_wide = s_own
    for p in range(n_pages):
        n = p // pages_per_block
        s = s_scr[:, p * page:(p + 1) * page] + bias_ref[:, p * page:(p + 1) * page] + keep[:, n:n + 1]
        s_scr[:, p * page:(p + 1) * page] = s
        m_wide = jnp.maximum(m_wide, s)
    m = jnp.max(m_wide, axis=-1, keepdims=True)

    pr = jnp.exp(s_scr[:, own:own + LANES] - m)
    l_wide = pr
    acc = jnp.dot(pr.astype(BF16), new_rows(vn_ref), preferred_element_type=F32)
    for p in range(n_pages):
        pr = jnp.exp(s_scr[:, p * page:(p + 1) * page] - m)
        l_wide = l_wide + pr
        vt = v_pages[p][...].astype(BF16)
        acc = acc + lax.dot_general(pr.astype(BF16), vt, NT_DIMS, preferred_element_type=F32)
    acc = acc / jnp.sum(l_wide, axis=-1, keepdims=True)
    out = jnp.zeros((rows, d_att), F32)
    for h in range(heads):
        out = jnp.where(_in_head(feat, h), acc[h * rows:(h + 1) * rows, :], out)
    o_ref[0] = out


def _moba_sample(aq, ak, av, cache_kt, cache_vt, page_table, bias, *, heads):
    nb, rows, d_att = aq.shape
    n_pages = page_table.shape[1]
    page = cache_kt.shape[-1]
    assert cache_kt.shape[1:] == (d_att, page) and MOBA_BLOCK % page == 0 and page == LANES
    tok = pl.BlockSpec((1, rows, d_att), lambda b, pt: (b, 0, 0))
    in_hbm = pl.BlockSpec(memory_space=pl.ANY)
    n_slots = 3
    assert nb >= n_slots
    page_slots = pltpu.VMEM((n_slots, n_pages, d_att, page), F32)
    grid_spec = pltpu.PrefetchScalarGridSpec(
        num_scalar_prefetch=1,
        grid=(nb,),
        in_specs=[tok, tok, tok, pl.BlockSpec(bias.shape, lambda b, pt: (0, 0)), in_hbm, in_hbm],
        out_specs=tok,
        scratch_shapes=[page_slots, page_slots, pltpu.SemaphoreType.DMA((2, n_slots)),
                        pltpu.VMEM(bias.shape, F32)],
    )
    return pl.pallas_call(
        functools.partial(_moba_sample_body, n_pages=n_pages, heads=heads, rows=rows),
        grid_spec=grid_spec,
        out_shape=jax.ShapeDtypeStruct((nb, rows, d_att), F32),
        compiler_params=pltpu.CompilerParams(
            dimension_semantics=("arbitrary",), vmem_limit_bytes=VMEM_LIMIT),
        name="moba_sample",
    )(page_table, aq, ak, av, bias, cache_kt, cache_vt)


def _out_proj_body(*refs, d_lin, gated):
    if gated:
        ol_ref, oa_ref, x_ref, w_ref, gpost_ref, y_ref = refs
        oa = oa_ref[...]
    else:
        ol_ref, oa_ref, ga_ref, x_ref, w_ref, gpost_ref, y_ref = refs
        oa = oa_ref[...] * ga_ref[...]
    o = jnp.dot(ol_ref[...].astype(BF16), w_ref[0:d_lin, :], preferred_element_type=F32)
    o = o + jnp.dot(oa.astype(BF16), w_ref[d_lin:, :], preferred_element_type=F32)
    y_ref[...] = x_ref[...] + _rms(o, gpost_ref[...])


def _out_proj(o_lin, o_att, ga, x, w_bf, gpost, *, tm=256):
    rows, d_model = x.shape
    tm = min(tm, rows)
    assert rows % tm == 0
    d_lin = o_lin.shape[1]
    row_spec = lambda w: pl.BlockSpec((tm, w), lambda i: (i, 0))
    acts = [o_lin, o_att] + ([] if ga is None else [ga])
    return pl.pallas_call(
        functools.partial(_out_proj_body, d_lin=d_lin, gated=ga is None),
        grid=(rows // tm,),
        in_specs=[row_spec(a.shape[1]) for a in acts] + [
            row_spec(d_model),
            pl.BlockSpec(w_bf.shape, lambda i: (0, 0)),
            pl.BlockSpec((1, d_model), lambda i: (0, 0)),
        ],
        out_specs=row_spec(d_model),
        out_shape=jax.ShapeDtypeStruct((rows, d_model), F32),
        compiler_params=pltpu.CompilerParams(dimension_semantics=("arbitrary",)),
        name="out_proj",
    )(*acts, x, w_bf, gpost)


def kernel(x_prompt, x_sample, cache_k, cache_v, state_hgrn, page_table, w_in, w_out,
           norm_pre, norm_post, norm_lin_out, lin_lower_bound, rel_bias):
    depth = w_in.shape[0]
    assert depth == 1 and lin_lower_bound.shape[0] == depth + 1
    b, t, d_model = x_prompt.shape
    nb, ts, _ = x_sample.shape
    d_lin = lin_lower_bound.shape[1]
    d_att = w_out.shape[1] - d_lin
    lin_heads = d_lin // LIN_HEAD_DIM
    att_heads = rel_bias.shape[1]
    n_pages = page_table.shape[1]
    page = cache_k.shape[2]
    past_len = n_pages * page
    assert d_att == att_heads * ATT_HEAD_DIM and t % MOBA_BLOCK == 0
    assert past_len % MOBA_BLOCK == 0 and ts <= SUBLANES

    w_in_bf = w_in[0].astype(BF16)
    w_out_bf = w_out[0].astype(BF16)
    gpre, gpost, glin = norm_pre, norm_post, norm_lin_out
    proj = functools.partial(_in_proj, gpre=gpre, w_bf=w_in_bf, llb=lin_lower_bound,
                             d_lin=d_lin, d_att=d_att)

    xp = x_prompt.reshape(b * t, d_model)
    q, g, kk, v, gl, ga, ak, aqt, akt, avt = proj(xp, seq_len=t)
    q, g, kk, v, gl, ak = [a.reshape(b, t, -1) for a in (q, g, kk, v, gl, ak)]
    s0 = jnp.zeros((b, lin_heads, LIN_HEAD_DIM, LIN_HEAD_DIM), F32)
    o_lin, s_prompt = _hgrn_tiles(q, g, kk, v, gl, s0, glin, chunk=32, t_tile=512)
    keep = _moba_gate(aqt, ak)
    n_bias = 6
    assert (n_bias - 1) * MOBA_BLOCK - (MOBA_BLOCK - 1) >= MAX_DISTANCE
    bias = _prompt_bias(rel_bias, n_bias)
    o_att = _moba_prompt(aqt, ak, avt, keep, bias, ga)
    y_prompt = _out_proj(o_lin.reshape(b * t, d_lin), o_att.reshape(b * t, d_att),
                         None, xp, w_out_bf, gpost, tm=512).reshape(b, t, d_model)

    rows = SUBLANES
    xs = jnp.pad(x_sample, ((0, 0), (0, rows - ts), (0, 0))).reshape(nb * rows, d_model)
    qs, gs, kks, vs, gls, gas, aqs, aks, avs = [a.reshape(nb, rows, -1) for a in proj(xs)]
    o_lin_s, s_sample = _hgrn_step(qs, gs, kks, vs, gls, state_hgrn[0], glin, valid=ts, n_seq=16)
    bias_s = _sample_bias(rel_bias, past_len=past_len, n_valid=ts, rows=rows)
    cache_kt = cache_k[0].transpose(0, 2, 3, 1).reshape(-1, d_att, page)
    cache_vt = cache_v[0].transpose(0, 2, 3, 1).reshape(-1, d_att, page)
    o_att_s = _moba_sample(aqs, aks, avs, cache_kt, cache_vt, page_table, bias_s, heads=att_heads)
    y_s = _out_proj(o_lin_s.reshape(nb * rows, d_lin), o_att_s.reshape(nb * rows, d_att),
                    gas.reshape(nb * rows, d_att), xs, w_out_bf, gpost)
    y_sample = y_s.reshape(nb, rows, d_model)[:, :ts]

    def prompt_kv(a):
        return a.reshape(1, b, att_heads, ATT_HEAD_DIM, t).transpose(0, 1, 4, 2, 3)

    kvs_shape = (1, nb, ts, att_heads, ATT_HEAD_DIM)
    return (y_prompt, y_sample, prompt_kv(akt), prompt_kv(avt), s_prompt[None],
            aks[:, :ts].reshape(kvs_shape), avs[:, :ts].reshape(kvs_shape), s_sample[None])
```

```python
import functools
import math

import jax
import jax.numpy as jnp
import numpy as np
from jax import lax
from jax.experimental import pallas as pl
from jax.experimental.pallas import tpu as pltpu

F32 = jnp.float32
BF16 = jnp.bfloat16

EPS = 1e-6
LIN_HEAD_DIM = 128
ATT_HEAD_DIM = 64
MOBA_BLOCK = 256
MOBA_TOPK = 3
N_BUCKETS = 32
MAX_DISTANCE = 1024
NEG = -2e30
M_INIT = -1e30
LOG2E = 1.4426950408889634
LANES = 128
SUBLANES = 8
VMEM_LIMIT = 56 * 1024 * 1024
HEADS_PER_TILE = LANES // ATT_HEAD_DIM

NT_DIMS = (((1,), (1,)), ((), ()))
TN_DIMS = (((0,), (0,)), ((), ()))


def _sigmoid(x):
    return 1.0 / (1.0 + jnp.exp(-x))


def _silu(x):
    return x * _sigmoid(x)


def _rms(x, g):
    return x * lax.rsqrt(jnp.mean(x * x, axis=-1, keepdims=True) + EPS) * g


def _in_head(idx, head):
    return (idx >= head * ATT_HEAD_DIM) & (idx < (head + 1) * ATT_HEAD_DIM)


def _in_proj_body(x_ref, gpre_ref, w_ref, llb_ref, *rest, d_lin, d_att, feat_major):
    if feat_major:
        wt_ref, q_o, g_o, kk_o, v_o, gl_o, ga_o, ak_o, aqt_o, akt_o, avt_o = rest
    else:
        q_o, g_o, kk_o, v_o, gl_o, ga_o, aq_o, ak_o, av_o = rest
    h = _rms(x_ref[...], gpre_ref[...]).astype(BF16)

    def proj(col, width):
        return jnp.dot(h, w_ref[:, col:col + width], preferred_element_type=F32)

    def proj_t(row, width):
        return lax.dot_general(wt_ref[row:row + width, :], h, NT_DIMS, preferred_element_type=F32)

    llb = llb_ref[...]
    e = jnp.exp(llb - jnp.max(llb, axis=0, keepdims=True))
    lb = e[0:1, :] / jnp.sum(e, axis=0, keepdims=True)

    q_o[...] = _silu(proj(0, d_lin))
    f = lb + (1.0 - lb) * _sigmoid(proj(d_lin, d_lin))
    g_o[...] = jnp.log(f)
    kk_o[...] = 1.0 - f
    v_o[...] = proj(2 * d_lin, d_lin)
    gl_o[...] = _silu(proj(3 * d_lin, d_lin))
    a0 = 4 * d_lin
    ga_o[...] = _silu(proj(a0 + 3 * d_att, d_att))
    if feat_major:
        aqt_o[0] = proj_t(0, d_att)
        akt = proj_t(d_att, d_att)
        akt_o[0] = akt
        ak_o[...] = akt.T
        avt_o[0] = proj_t(2 * d_att, d_att)
    else:
        aq_o[...] = proj(a0, d_att)
        ak_o[...] = proj(a0 + d_att, d_att)
        av_o[...] = proj(a0 + 2 * d_att, d_att)


def _in_proj(x, gpre, w_bf, llb, *, d_lin, d_att, seq_len=None, tm=256):
    rows, d_model = x.shape
    tm = min(tm, rows)
    assert rows % tm == 0
    d_in = w_bf.shape[1]
    feat_major = seq_len is not None
    row_spec = lambda w: pl.BlockSpec((tm, w), lambda i: (i, 0))
    in_specs = [
        pl.BlockSpec((tm, d_model), lambda i: (i, 0)),
        pl.BlockSpec((1, d_model), lambda i: (0, 0)),
        pl.BlockSpec((d_model, d_in), lambda i: (0, 0), pipeline_mode=pl.Buffered(1)),
        pl.BlockSpec(llb.shape, lambda i: (0, 0)),
    ]
    args = [x, gpre, w_bf, llb]
    out_specs = [row_spec(d_lin)] * 5 + [row_spec(d_att)]
    out_shape = [jax.ShapeDtypeStruct((rows, d_lin), F32)] * 5 + [jax.ShapeDtypeStruct((rows, d_att), F32)]
    if feat_major:
        assert seq_len % tm == 0
        tiles = seq_len // tm
        a0 = 4 * d_lin
        wt_bf = w_bf[:, a0:a0 + 3 * d_att].T
        in_specs.append(pl.BlockSpec(wt_bf.shape, lambda i: (0, 0), pipeline_mode=pl.Buffered(1)))
        args.append(wt_bf)
        t_spec = pl.BlockSpec((1, d_att, tm), lambda i: (i // tiles, 0, i % tiles))
        t_shape = jax.ShapeDtypeStruct((rows // seq_len, d_att, seq_len), F32)
        out_specs += [row_spec(d_att)] + [t_spec] * 3
        out_shape += [jax.ShapeDtypeStruct((rows, d_att), F32)] + [t_shape] * 3
    else:
        out_specs += [row_spec(d_att)] * 3
        out_shape += [jax.ShapeDtypeStruct((rows, d_att), F32)] * 3
    return pl.pallas_call(
        functools.partial(_in_proj_body, d_lin=d_lin, d_att=d_att, feat_major=feat_major),
        grid=(rows // tm,),
        in_specs=in_specs,
        out_specs=out_specs,
        out_shape=out_shape,
        compiler_params=pltpu.CompilerParams(
            dimension_semantics=("arbitrary",), vmem_limit_bytes=VMEM_LIMIT),
        name="in_proj",
    )(*args)


def _hgrn_step_body(q_ref, g_ref, kk_ref, v_ref, gl_ref, s0_ref, glin_ref, o_ref, sout_ref, b_scr,
                    *, rows, valid, n_seq):
    row = lax.broadcasted_iota(jnp.int32, (rows, 1), 0)
    for s in range(n_seq):
        acc = jnp.zeros((1, LIN_HEAD_DIM), F32)
        for r in range(rows):
            if r < valid:
                acc = acc + g_ref[s, r:r + 1, :]
            b_scr[s, r:r + 1, :] = acc
    seqs = range(n_seq)
    b = [b_scr[s] for s in seqs]
    b_last = [b_scr[s, rows - 1:rows, :] for s in seqs]
    q = [q_ref[s] for s in seqs]
    kk = [jnp.where(row < valid, kk_ref[s], 0.0) for s in seqs]

    o = [jnp.dot((q[s] * jnp.exp(b[s])).astype(BF16), s0_ref[s, 0].astype(BF16), preferred_element_type=F32)
         for s in seqs]
    for s in seqs:
        for j in range(valid):
            w = q[s] * kk_ref[s, j:j + 1, :] * jnp.exp(jnp.minimum(b[s] - b_scr[s, j:j + 1, :], 0.0))
            a = jnp.sum(w, axis=-1, keepdims=True)
            o[s] = o[s] + jnp.where(row >= j, a, 0.0) * v_ref[s, j:j + 1, :]
    u = [lax.dot_general((kk[s] * jnp.exp(b_last[s] - b[s])).astype(BF16), v_ref[s].astype(BF16), TN_DIMS,
                         preferred_element_type=F32) for s in seqs]
    ones = jnp.ones((rows, LIN_HEAD_DIM), BF16)
    decay = []
    for s in seqs:
        rest = jnp.exp(b_last[s])
        pieces = jnp.zeros((rows, LIN_HEAD_DIM), F32)
        for pi in range(3):
            piece = rest.astype(BF16).astype(F32)
            rest = rest - piece
            pieces = jnp.where(row == pi, piece, pieces)
        decay.append(lax.dot_general(pieces.astype(BF16), ones, TN_DIMS, preferred_element_type=F32))
    for s in seqs:
        sout_ref[s, 0] = s0_ref[s, 0] * decay[s] + u[s]
        o_ref[s] = _rms(o[s], glin_ref[...]) * gl_ref[s]


def _hgrn_step(q, g, kk, v, gl, s0, glin, *, valid, n_seq):
    nb, rows, d_lin = q.shape
    heads = d_lin // LIN_HEAD_DIM
    assert rows >= 3 and nb % n_seq == 0
    seq_spec = pl.BlockSpec((n_seq, rows, LIN_HEAD_DIM), lambda b, h: (b, 0, h))
    st_spec = pl.BlockSpec((n_seq, 1, LIN_HEAD_DIM, LIN_HEAD_DIM), lambda b, h: (b, h, 0, 0))
    return pl.pallas_call(
        functools.partial(_hgrn_step_body, rows=rows, valid=valid, n_seq=n_seq),
        grid=(nb // n_seq, heads),
        in_specs=[seq_spec] * 5 + [st_spec, pl.BlockSpec((1, LIN_HEAD_DIM), lambda b, h: (0, 0))],
        out_specs=[seq_spec, st_spec],
        out_shape=[jax.ShapeDtypeStruct((nb, rows, d_lin), F32),
                   jax.ShapeDtypeStruct(s0.shape, F32)],
        scratch_shapes=[pltpu.VMEM((n_seq, rows, LIN_HEAD_DIM), F32)],
        compiler_params=pltpu.CompilerParams(dimension_semantics=("arbitrary", "arbitrary")),
        name="hgrn_step",
    )(q, g, kk, v, gl, s0, glin)


SAFE_DECAY = 60.0


def _hgrn_tile_body(q_ref, g_ref, kk_ref, v_ref, gl_ref, s0_ref, glin_ref, o_ref, sout_ref,
                    st_scr, b_scr, oi_scr, u_scr, kkp_scr, bp_scr, vp_scr, *, chunk, t_tile, heads):
    it = pl.program_id(1)
    hs = [slice(h * LIN_HEAD_DIM, (h + 1) * LIN_HEAD_DIM) for h in range(heads)]

    @pl.when(it == 0)
    def _():
        for h in range(heads):
            st_scr[h] = s0_ref[0, h].T

    lg = int(math.log2(chunk))
    n_groups = t_tile // LANES
    n_chunks = t_tile // chunk
    r = lax.broadcasted_iota(jnp.int32, (LANES, LANES), 0)
    c = lax.broadcasted_iota(jnp.int32, (LANES, LANES), 1)
    causal = ((r >> lg) == (c >> lg)) & (c <= r)
    tri = jnp.where(causal, 1.0, 0.0).astype(BF16)
    n_pieces = 3
    for h in range(heads):
        pieces = []
        for gi in range(n_groups):
            rest = g_ref[0, gi * LANES:(gi + 1) * LANES, hs[h]]
            for _ in range(n_pieces):
                piece = rest.astype(BF16)
                rest = rest - piece.astype(F32)
                pieces.append(piece)
        sums = jnp.dot(tri, jnp.concatenate(pieces, axis=1), preferred_element_type=F32)
        for gi in range(n_groups):
            b = jnp.zeros((LANES, LIN_HEAD_DIM), F32)
            for pi in range(n_pieces):
                col = (gi * n_pieces + pi) * LIN_HEAD_DIM
                b = b + sums[:, col:col + LIN_HEAD_DIM]
            b_scr[h, gi * LANES:(gi + 1) * LANES, :] = b

    def intra_mxu():
        atts = {}
        for h in range(heads):
            for gi in range(n_groups):
                rows = slice(gi * LANES, (gi + 1) * LANES)
                b = b_scr[h, rows, :]
                qe = (q_ref[0, rows, hs[h]] * jnp.exp(b)).astype(BF16)
                ke = (kk_ref[0, rows, hs[h]] * jnp.exp(-b)).astype(BF16)
                att = lax.dot_general(qe, ke, NT_DIMS, preferred_element_type=F32)
                atts[h, gi] = jnp.where(causal, att, 0.0).astype(BF16)
        for h in range(heads):
            for gi in range(n_groups):
                rows = slice(gi * LANES, (gi + 1) * LANES)
                oi_scr[h, rows, :] = jnp.dot(atts[h, gi], v_ref[0, rows, hs[h]].astype(BF16),
                                             preferred_element_type=F32)

    def intra_pairs():
        zeros = jnp.zeros((chunk, LIN_HEAD_DIM), F32)
        tmod = lax.broadcasted_iota(jnp.int32, (t_tile, 1), 0) & (chunk - 1)
        for h in range(heads):
            for dst, src in ((kkp_scr, kk_ref[0, :, hs[h]]), (bp_scr, b_scr[h]), (vp_scr, v_ref[0, :, hs[h]])):
                dst[0:chunk, :] = zeros
                dst[chunk:, :] = src
            oi_scr[h] = jnp.zeros((t_tile, LIN_HEAD_DIM), F32)
            q = q_ref[0, :, hs[h]]
            b = b_scr[h]

            def lag(d, carry, h=h, q=q, b=b):
                start = chunk - d
                w = q * kkp_scr[pl.ds(start, t_tile), :] * jnp.exp(
                    jnp.minimum(b - bp_scr[pl.ds(start, t_tile), :], 0.0))
                a = jnp.sum(w, axis=-1, keepdims=True)
                oi_scr[h] += jnp.where(tmod >= d, a, 0.0) * vp_scr[pl.ds(start, t_tile), :]
                return carry

            lax.fori_loop(0, chunk, lag, 0)

    lax.cond(jnp.min(b_scr[...]) >= -SAFE_DECAY, intra_mxu, intra_pairs)

    for h in range(heads):
        for n in range(n_chunks):
            rows = slice(n * chunk, (n + 1) * chunk)
            b_last = b_scr[h, (n + 1) * chunk - 1:(n + 1) * chunk, :]
            kt = (kk_ref[0, rows, hs[h]] * jnp.exp(b_last - b_scr[h, rows, :])).astype(BF16)
            u_scr[h, n] = lax.dot_general(v_ref[0, rows, hs[h]].astype(BF16), kt, TN_DIMS,
                                          preferred_element_type=F32)

    st = [st_scr[h] for h in range(heads)]
    for n in range(n_chunks):
        rows = slice(n * chunk, (n + 1) * chunk)
        for h in range(heads):
            b_last = b_scr[h, (n + 1) * chunk - 1:(n + 1) * chunk, :]
            o = lax.dot_general((q_ref[0, rows, hs[h]] * jnp.exp(b_scr[h, rows, :])).astype(BF16),
                                st[h].astype(BF16), NT_DIMS, preferred_element_type=F32) + oi_scr[h, rows, :]
            st[h] = st[h] * jnp.exp(b_last) + u_scr[h, n]
            o_ref[0, rows, hs[h]] = (_rms(o, glin_ref[...]) * gl_ref[0, rows, hs[h]]).astype(o_ref.dtype)
    for h in range(heads):
        st_scr[h] = st[h]

    @pl.when(it == pl.num_programs(1) - 1)
    def _():
        for h in range(heads):
            sout_ref[0, h] = st[h].T


def _hgrn_tiles(q, g, kk, v, gl, s0, glin, *, chunk, t_tile):
    nb, t, d_lin = q.shape
    heads = d_lin // LIN_HEAD_DIM
    assert LANES % chunk == 0 and t_tile % LANES == 0 and t % t_tile == 0
    seq_spec = pl.BlockSpec((1, t_tile, d_lin), lambda b, i: (b, i, 0))
    st_spec = pl.BlockSpec((1, heads, LIN_HEAD_DIM, LIN_HEAD_DIM), lambda b, i: (b, 0, 0, 0))
    tile = pltpu.VMEM((heads, t_tile, LIN_HEAD_DIM), F32)
    halo = pltpu.VMEM((chunk + t_tile, LIN_HEAD_DIM), F32)
    return pl.pallas_call(
        functools.partial(_hgrn_tile_body, chunk=chunk, t_tile=t_tile, heads=heads),
        grid=(nb, t // t_tile),
        in_specs=[seq_spec] * 5 + [st_spec, pl.BlockSpec((1, LIN_HEAD_DIM), lambda b, i: (0, 0))],
        out_specs=[seq_spec, st_spec],
        out_shape=[jax.ShapeDtypeStruct((nb, t, d_lin), BF16),
                   jax.ShapeDtypeStruct(s0.shape, F32)],
        scratch_shapes=[pltpu.VMEM((heads, LIN_HEAD_DIM, LIN_HEAD_DIM), F32), tile, tile,
                        pltpu.VMEM((heads, t_tile // chunk, LIN_HEAD_DIM, LIN_HEAD_DIM), F32),
                        halo, halo, halo],
        compiler_params=pltpu.CompilerParams(
            dimension_semantics=("arbitrary", "arbitrary"), vmem_limit_bytes=VMEM_LIMIT),
        name="hgrn_tiles",
    )(q, g, kk, v, gl, s0, glin)


def _t5_first_distances():
    max_exact = N_BUCKETS // 2
    n = np.arange(0, MAX_DISTANCE + 1, dtype=np.int32)
    ratio = np.maximum(n, 1).astype(np.float32) / np.float32(max_exact)
    large = max_exact + (np.log(ratio) / np.float32(math.log(MAX_DISTANCE / max_exact))
                         * np.float32(N_BUCKETS - max_exact)).astype(np.int32)
    bucket = np.where(n < max_exact, n, np.minimum(large, N_BUCKETS - 1))
    assert np.all(np.diff(bucket) >= 0) and bucket[-1] == N_BUCKETS - 1
    return [int(np.argmax(bucket >= bk)) for bk in range(N_BUCKETS)]


_T5_FIRST_DISTANCE = _t5_first_distances()


def _t5_bias(dist, rb_ref, head):
    n = jnp.maximum(dist, 0)
    out = jnp.full(dist.shape, rb_ref[0, head], F32)
    for bk in range(1, N_BUCKETS):
        out = jnp.where(n >= _T5_FIRST_DISTANCE[bk], rb_ref[bk, head], out)
    return out


def _prompt_bias_body(rb_ref, o_ref, *, n_tiles):
    h = pl.program_id(0)
    x = lax.broadcasted_iota(jnp.int32, (SUBLANES, 2 * MOBA_BLOCK), 1)
    for d in range(n_tiles):
        dist = (d - 1) * MOBA_BLOCK + x
        by_dist = jnp.where(dist >= 0, _t5_bias(dist, rb_ref, h) * LOG2E, NEG)
        wide = jnp.broadcast_to(by_dist[0:1, :], (MOBA_BLOCK, 2 * MOBA_BLOCK))
        o_ref[0, d] = pltpu.roll(wide, 0, 1, stride=1, stride_axis=0)[:, MOBA_BLOCK:]


def _prompt_bias(rel_bias, n_tiles):
    heads = rel_bias.shape[1]
    return pl.pallas_call(
        functools.partial(_prompt_bias_body, n_tiles=n_tiles),
        grid=(heads,),
        in_specs=[pl.BlockSpec(memory_space=pltpu.SMEM)],
        out_specs=pl.BlockSpec((1, n_tiles, MOBA_BLOCK, MOBA_BLOCK), lambda h: (h, 0, 0, 0)),
        out_shape=jax.ShapeDtypeStruct((heads, n_tiles, MOBA_BLOCK, MOBA_BLOCK), F32),
        name="prompt_bias",
    )(rel_bias)


def _sample_bias_body(rb_ref, o_ref, *, past_len, n_valid, rows):
    h = pl.program_id(0)
    width = past_len + LANES
    r = lax.broadcasted_iota(jnp.int32, (rows, width), 0)
    kpos = lax.broadcasted_iota(jnp.int32, (rows, width), 1)
    dist = past_len + r - kpos
    ok = (dist >= 0) & (kpos < past_len + n_valid)
    o_ref[...] = jnp.where(ok, _t5_bias(dist, rb_ref, h), NEG)


def _sample_bias(rel_bias, *, past_len, n_valid, rows):
    heads = rel_bias.shape[1]
    width = past_len + LANES
    return pl.pallas_call(
        functools.partial(_sample_bias_body, past_len=past_len, n_valid=n_valid, rows=rows),
        grid=(heads,),
        in_specs=[pl.BlockSpec(memory_space=pltpu.SMEM)],
        out_specs=pl.BlockSpec((rows, width), lambda h: (h, 0)),
        out_shape=jax.ShapeDtypeStruct((heads * rows, width), F32),
        name="sample_bias",
    )(rel_bias)


def _topk_keep(gate, idx, own, axis, n):
    past = idx < own
    gm = jnp.where(past, gate, -jnp.inf)
    rank = jnp.zeros(gate.shape, jnp.int32)
    for jp in range(n):
        gj = lax.slice_in_dim(gm, jp, jp + 1, axis=axis)
        beats = (gj > gm) | ((gj == gm) & (jp < idx))
        rank = rank + beats.astype(jnp.int32)
    keep = ((rank < MOBA_TOPK) & past) | (idx == own)
    return jnp.where(keep, 0.0, NEG)


def _moba_gate_body(qt_ref, k_ref, a_ref, km_scr, *, n_blocks):
    t = n_blocks * MOBA_BLOCK
    qt = qt_ref[0]
    lane = lax.broadcasted_iota(jnp.int32, (1, LANES), 1)
    for n in range(n_blocks):
        km_scr[pl.ds(n, 1), :] = jnp.sum(
            k_ref[0, n * MOBA_BLOCK:(n + 1) * MOBA_BLOCK, :], axis=0, keepdims=True) * (1.0 / MOBA_BLOCK)
    km = km_scr[...]
    blk = lax.broadcasted_iota(jnp.int32, (n_blocks, t), 0)
    own = lax.broadcasted_iota(jnp.int32, (n_blocks, t), 1) >> int(math.log2(MOBA_BLOCK))
    for hh in range(HEADS_PER_TILE):
        kmh = jnp.where(_in_head(lane, hh), km, 0.0)
        gate_t = jnp.dot(kmh, qt, precision=lax.Precision.HIGHEST, preferred_element_type=F32)
        a_ref[0, 0, hh * n_blocks:(hh + 1) * n_blocks, :] = _topk_keep(gate_t, blk, own, axis=0, n=n_blocks)


def _moba_gate(aqt, ak):
    b, d_att, t = aqt.shape
    tiles = d_att // LANES
    n_blocks = t // MOBA_BLOCK
    return pl.pallas_call(
        functools.partial(_moba_gate_body, n_blocks=n_blocks),
        grid=(b, tiles),
        in_specs=[pl.BlockSpec((1, LANES, t), lambda i, p: (i, p, 0)),
                  pl.BlockSpec((1, t, LANES), lambda i, p: (i, 0, p))],
        out_specs=pl.BlockSpec((1, 1, HEADS_PER_TILE * n_blocks, t), lambda i, p: (i, p, 0, 0)),
        out_shape=jax.ShapeDtypeStruct((b, tiles, HEADS_PER_TILE * n_blocks, t), F32),
        scratch_shapes=[pltpu.VMEM((n_blocks, LANES), F32)],
        compiler_params=pltpu.CompilerParams(
            dimension_semantics=("arbitrary", "arbitrary"), vmem_limit_bytes=VMEM_LIMIT),
        name="moba_gate",
    )(aqt, ak)


def _moba_prompt_body(qt_ref, k_ref, vt_ref, keep_ref, bias_ref, ga_ref, o_ref, kaug_scr, vtaug_scr, s_scr,
                      *, n_bias, n_blocks, group, tps):
    i = pl.program_id(2)
    rows = group * MOBA_BLOCK
    t = n_blocks * MOBA_BLOCK
    lg_block = int(math.log2(MOBA_BLOCK))
    feat = lax.broadcasted_iota(jnp.int32, (LANES, 1), 0)
    heads = [(tt, hh) for tt in range(tps) for hh in range(HEADS_PER_TILE)]
    tile = [slice(tt * LANES, (tt + 1) * LANES) for tt in range(tps)]
    ones_row = [((hh + 1) % HEADS_PER_TILE) * ATT_HEAD_DIM for hh in range(HEADS_PER_TILE)]

    @pl.when(i == 0)
    def _():
        blk = lax.broadcasted_iota(jnp.int32, (rows, LANES), 0) >> lg_block
        lane = lax.broadcasted_iota(jnp.int32, (rows, LANES), 1)
        for c in range(t // rows):
            sl = slice(c * rows, (c + 1) * rows)
            for hd, (tt, hh) in enumerate(heads):
                onehot = jnp.where(lane == hh * n_blocks + c * group + blk, 1.0, 0.0)
                kaug_scr[hd, sl, :] = jnp.concatenate([k_ref[0, sl, tile[tt]], onehot], axis=1).astype(BF16)
                vtaug_scr[hd, :, sl] = jnp.where(feat == ones_row[hh], 1.0, vt_ref[0, tile[tt], sl]).astype(BF16)

    q_aug = []
    for tt, hh in heads:
        qt = qt_ref[0, tile[tt], :] * (ATT_HEAD_DIM ** -0.5 * LOG2E)
        keep = keep_ref[0, tt]
        pad = jnp.zeros((LANES - keep.shape[0], MOBA_BLOCK), F32)
        q_aug.append(jnp.concatenate([jnp.where(_in_head(feat, hh), qt, 0.0), keep, pad], axis=0).astype(BF16))
    n_groups = (i >> int(math.log2(group))) + 1
    n_heads = len(heads)

    def scores(gi, m):
        r0 = pl.multiple_of(gi * rows, rows)
        m_new = []
        for hd in range(n_heads):
            s = jnp.dot(kaug_scr[hd, pl.ds(r0, rows), :], q_aug[hd], preferred_element_type=F32)
            mh = m[hd]
            for u in range(group):
                d = jnp.clip(i - (gi * group + u), 0, n_bias - 1)
                su = s[u * MOBA_BLOCK:(u + 1) * MOBA_BLOCK] + bias_ref[hd, d]
                s_scr[hd, pl.ds(r0 + u * MOBA_BLOCK, MOBA_BLOCK), :] = su
                mh = jnp.maximum(mh, jnp.max(su, axis=0, keepdims=True))
            m_new.append(mh)
        return tuple(m_new)

    def values(gi, m, m_before, acc):
        r0 = pl.multiple_of(gi * rows, rows)
        out = []
        for hd in range(n_heads):
            p = jnp.exp2(s_scr[hd, pl.ds(r0, rows), :] - m[hd]).astype(BF16)
            out.append(acc[hd] * jnp.exp2(m_before[hd] - m[hd])
                       + jnp.dot(vtaug_scr[hd, :, pl.ds(r0, rows)], p, preferred_element_type=F32))
        return tuple(out)

    m_init = tuple(jnp.full((1, MOBA_BLOCK), M_INIT, F32) for _ in range(n_heads))
    acc0 = tuple(jnp.zeros((LANES, MOBA_BLOCK), F32) for _ in range(n_heads))

    def stage(gi, carry):
        m, m_before, acc = carry
        acc = values(gi - 1, m, m_before, acc)
        return scores(gi, m), m, acc

    m, m_before, acc = lax.fori_loop(1, n_groups, stage, (scores(0, m_init), m_init, acc0))
    acc = values(n_groups - 1, m, m_before, acc)
    for tt in range(tps):
        out = jnp.zeros((LANES, MOBA_BLOCK), F32)
        for hh in range(HEADS_PER_TILE):
            a = acc[tt * HEADS_PER_TILE + hh]
            out = jnp.where(_in_head(feat, hh), a / a[ones_row[hh]:ones_row[hh] + 1, :], out)
        o_ref[0, :, tile[tt]] = (out.T * ga_ref[0, :, tile[tt]]).astype(o_ref.dtype)


def _moba_prompt(aqt, ak, avt, keep, bias, ga, *, group=4, tps=2):
    b, d_att, t = aqt.shape
    tiles = d_att // LANES
    n_blocks = t // MOBA_BLOCK
    n_bias = bias.shape[1]
    group = min(group, n_blocks)
    assert n_blocks % group == 0 and group & (group - 1) == 0 and tiles % tps == 0
    width = tps * LANES
    n_heads = tps * HEADS_PER_TILE
    once = dict(pipeline_mode=pl.Buffered(1))
    return pl.pallas_call(
        functools.partial(_moba_prompt_body, n_bias=n_bias, n_blocks=n_blocks, group=group, tps=tps),
        grid=(b, tiles // tps, n_blocks),
        in_specs=[pl.BlockSpec((1, width, MOBA_BLOCK), lambda ib, p, i: (ib, p, i)),
                  pl.BlockSpec((1, t, width), lambda ib, p, i: (ib, 0, p), **once),
                  pl.BlockSpec((1, width, t), lambda ib, p, i: (ib, p, 0), **once),
                  pl.BlockSpec((1, tps, keep.shape[2], MOBA_BLOCK), lambda ib, p, i: (ib, p, 0, i)),
                  pl.BlockSpec((n_heads, n_bias, MOBA_BLOCK, MOBA_BLOCK), lambda ib, p, i: (p, 0, 0, 0), **once),
                  pl.BlockSpec((1, MOBA_BLOCK, width), lambda ib, p, i: (ib, i, p))],
        out_specs=pl.BlockSpec((1, MOBA_BLOCK, width), lambda ib, p, i: (ib, i, p)),
        out_shape=jax.ShapeDtypeStruct((b, t, d_att), BF16),
        scratch_shapes=[pltpu.VMEM((n_heads, t, 2 * LANES), BF16),
                        pltpu.VMEM((n_heads, LANES, t), BF16),
                        pltpu.VMEM((n_heads, t, MOBA_BLOCK), F32)],
        compiler_params=pltpu.CompilerParams(
            dimension_semantics=("arbitrary", "arbitrary", "arbitrary"), vmem_limit_bytes=VMEM_LIMIT),
        name="moba_prompt",
    )(aqt, ak, avt, keep, bias, ga.reshape(b, t, d_att))


def _moba_sample_body(pt_ref, q_ref, kn_ref, vn_ref, bias_ref, ck_hbm, cv_hbm, o_ref,
                      kbuf, vbuf, sems, s_scr, *, n_pages, heads, rows):
    b = pl.program_id(0)
    n_slots = kbuf.shape[0]
    ahead = n_slots - 1
    slot = lax.rem(b, n_slots)

    def page_copies(seq, sl):
        out = []
        for p in range(n_pages):
            phys = pt_ref[seq, p]
            out.append(pltpu.make_async_copy(ck_hbm.at[phys], kbuf.at[sl, p], sems.at[0, sl]))
            out.append(pltpu.make_async_copy(cv_hbm.at[phys], vbuf.at[sl, p], sems.at[1, sl]))
        return out

    @pl.when(b == 0)
    def _():
        for seq in range(ahead):
            for cp in page_copies(seq, seq):
                cp.start()

    @pl.when(b + ahead < pl.num_programs(0))
    def _():
        for cp in page_copies(b + ahead, lax.rem(b + ahead, n_slots)):
            cp.start()

    for cp in page_copies(b, slot):
        cp.wait()
    k_pages = [kbuf.at[slot, p] for p in range(n_pages)]
    v_pages = [vbuf.at[slot, p] for p in range(n_pages)]
    d_att = heads * ATT_HEAD_DIM
    page = kbuf.shape[-1]
    pages_per_block = MOBA_BLOCK // page
    n_blocks = n_pages // pages_per_block
    n_q = heads * rows
    feat = lax.broadcasted_iota(jnp.int32, (1, d_att), 1)
    lane = lax.broadcasted_iota(jnp.int32, (n_q, LANES), 1)

    q = q_ref[0] * (ATT_HEAD_DIM ** -0.5)
    q_bd = jnp.concatenate([jnp.where(_in_head(feat, h), q, 0.0) for h in range(heads)], axis=0).astype(BF16)

    gate = jnp.zeros((n_q, LANES), F32)
    for n in range(n_blocks):
        tot = jnp.zeros((n_q, page), F32)
        for p in range(n * pages_per_block, (n + 1) * pages_per_block):
            kt = k_pages[p][...].astype(BF16)
            s = jnp.dot(q_bd, kt, preferred_element_type=F32)
            s_scr[:, p * page:(p + 1) * page] = s
            tot = tot + s
        gate = jnp.where(lane == n, jnp.sum(tot, axis=-1, keepdims=True) * (1.0 / MOBA_BLOCK), gate)
    keep = _topk_keep(gate, lane, jnp.full(gate.shape, n_blocks, jnp.int32), axis=1, n=n_blocks)

    def new_rows(ref):
        return jnp.concatenate([ref[0], jnp.zeros((LANES - rows, d_att), F32)], axis=0).astype(BF16)

    own = n_pages * page
    s_own = lax.dot_general(q_bd, new_rows(kn_ref), NT_DIMS, preferred_element_type=F32)
    s_own = s_own + bias_ref[:, own:own + LANES]
    s_scr[:, own:own + LANES] = s_own
    m_wide = s_own
    for p in range(n_pages):
        n = p // pages_per_block
        s = s_scr[:, p * page:(p + 1) * page] + bias_ref[:, p * page:(p + 1) * page] + keep[:, n:n + 1]
        s_scr[:, p * page:(p + 1) * page] = s
        m_wide = jnp.maximum(m_wide, s)
    m = jnp.max(m_wide, axis=-1, keepdims=True)

    pr = jnp.exp(s_scr[:, own:own + LANES] - m)
    l_wide = pr
    acc = jnp.dot(pr.astype(BF16), new_rows(vn_ref), preferred_element_type=F32)
    for p in range(n_pages):
        pr = jnp.exp(s_scr[:, p * page:(p + 1) * page] - m)
        l_wide = l_wide + pr
        vt = v_pages[p][...].astype(BF16)
        acc = acc + lax.dot_general(pr.astype(BF16), vt, NT_DIMS, preferred_element_type=F32)
    acc = acc / jnp.sum(l_wide, axis=-1, keepdims=True)
    out = jnp.zeros((rows, d_att), F32)
    for h in range(heads):
        out = jnp.where(_in_head(feat, h), acc[h * rows:(h + 1) * rows, :], out)
    o_ref[0] = out


def _moba_sample(aq, ak, av, cache_kt, cache_vt, page_table, bias, *, heads):
    nb, rows, d_att = aq.shape
    n_pages = page_table.shape[1]
    page = cache_kt.shape[-1]
    assert cache_kt.shape[1:] == (d_att, page) and MOBA_BLOCK % page == 0 and page == LANES
    tok = pl.BlockSpec((1, rows, d_att), lambda b, pt: (b, 0, 0))
    in_hbm = pl.BlockSpec(memory_space=pl.ANY)
    n_slots = 3
    assert nb >= n_slots
    page_slots = pltpu.VMEM((n_slots, n_pages, d_att, page), F32)
    grid_spec = pltpu.PrefetchScalarGridSpec(
        num_scalar_prefetch=1,
        grid=(nb,),
        in_specs=[tok, tok, tok, pl.BlockSpec(bias.shape, lambda b, pt: (0, 0)), in_hbm, in_hbm],
        out_specs=tok,
        scratch_shapes=[page_slots, page_slots, pltpu.SemaphoreType.DMA((2, n_slots)),
                        pltpu.VMEM(bias.shape, F32)],
    )
    return pl.pallas_call(
        functools.partial(_moba_sample_body, n_pages=n_pages, heads=heads, rows=rows),
        grid_spec=grid_spec,
        out_shape=jax.ShapeDtypeStruct((nb, rows, d_att), F32),
        compiler_params=pltpu.CompilerParams(
            dimension_semantics=("arbitrary",), vmem_limit_bytes=VMEM_LIMIT),
        name="moba_sample",
    )(page_table, aq, ak, av, bias, cache_kt, cache_vt)


def _out_proj_body(*refs, d_lin, gated):
    if gated:
        ol_ref, oa_ref, x_ref, w_ref, gpost_ref, y_ref = refs
        oa = oa_ref[...]
    else:
        ol_ref, oa_ref, ga_ref, x_ref, w_ref, gpost_ref, y_ref = refs
        oa = oa_ref[...] * ga_ref[...]
    o = jnp.dot(ol_ref[...].astype(BF16), w_ref[0:d_lin, :], preferred_element_type=F32)
    o = o + jnp.dot(oa.astype(BF16), w_ref[d_lin:, :], preferred_element_type=F32)
    y_ref[...] = x_ref[...] + _rms(o, gpost_ref[...])


def _out_proj(o_lin, o_att, ga, x, w_bf, gpost, *, tm=256):
    rows, d_model = x.shape
    tm = min(tm, rows)
    assert rows % tm == 0
    d_lin = o_lin.shape[1]
    row_spec = lambda w: pl.BlockSpec((tm, w), lambda i: (i, 0))
    acts = [o_lin, o_att] + ([] if ga is None else [ga])
    return pl.pallas_call(
        functools.partial(_out_proj_body, d_lin=d_lin, gated=ga is None),
        grid=(rows // tm,),
        in_specs=[row_spec(a.shape[1]) for a in acts] + [
            row_spec(d_model),
            pl.BlockSpec(w_bf.shape, lambda i: (0, 0)),
            pl.BlockSpec((1, d_model), lambda i: (0, 0)),
        ],
        out_specs=row_spec(d_model),
        out_shape=jax.ShapeDtypeStruct((rows, d_model), F32),
        compiler_params=pltpu.CompilerParams(dimension_semantics=("arbitrary",)),
        name="out_proj",
    )(*acts, x, w_bf, gpost)


def kernel(x_prompt, x_sample, cache_k, cache_v, state_hgrn, page_table, w_in, w_out,
           norm_pre, norm_post, norm_lin_out, lin_lower_bound, rel_bias):
    depth = w_in.shape[0]
    assert depth == 1 and lin_lower_bound.shape[0] == depth + 1
    b, t, d_model = x_prompt.shape
    nb, ts, _ = x_sample.shape
    d_lin = lin_lower_bound.shape[1]
    d_att = w_out.shape[1] - d_lin
    lin_heads = d_lin // LIN_HEAD_DIM
    att_heads = rel_bias.shape[1]
    n_pages = page_table.shape[1]
    page = cache_k.shape[2]
    past_len = n_pages * page
    assert d_att == att_heads * ATT_HEAD_DIM and t % MOBA_BLOCK == 0
    assert past_len % MOBA_BLOCK == 0 and ts <= SUBLANES

    w_in_bf = w_in[0].astype(BF16)
    w_out_bf = w_out[0].astype(BF16)
    gpre, gpost, glin = norm_pre, norm_post, norm_lin_out
    proj = functools.partial(_in_proj, gpre=gpre, w_bf=w_in_bf, llb=lin_lower_bound,
                             d_lin=d_lin, d_att=d_att)

    xp = x_prompt.reshape(b * t, d_model)
    q, g, kk, v, gl, ga, ak, aqt, akt, avt = proj(xp, seq_len=t, tm=512)
    q, g, kk, v, gl, ak = [a.reshape(b, t, -1) for a in (q, g, kk, v, gl, ak)]
    s0 = jnp.zeros((b, lin_heads, LIN_HEAD_DIM, LIN_HEAD_DIM), F32)
    o_lin, s_prompt = _hgrn_tiles(q, g, kk, v, gl, s0, glin, chunk=32, t_tile=512)
    keep = _moba_gate(aqt, ak)
    n_bias = 6
    assert (n_bias - 1) * MOBA_BLOCK - (MOBA_BLOCK - 1) >= MAX_DISTANCE
    bias = _prompt_bias(rel_bias, n_bias)
    o_att = _moba_prompt(aqt, ak, avt, keep, bias, ga)
    y_prompt = _out_proj(o_lin.reshape(b * t, d_lin), o_att.reshape(b * t, d_att),
                         None, xp, w_out_bf, gpost, tm=1024).reshape(b, t, d_model)

    rows = SUBLANES
    xs = jnp.pad(x_sample, ((0, 0), (0, rows - ts), (0, 0))).reshape(nb * rows, d_model)
    qs, gs, kks, vs, gls, gas, aqs, aks, avs = [a.reshape(nb, rows, -1) for a in proj(xs)]
    o_lin_s, s_sample = _hgrn_step(qs, gs, kks, vs, gls, state_hgrn[0], glin, valid=ts, n_seq=16)
    bias_s = _sample_bias(rel_bias, past_len=past_len, n_valid=ts, rows=rows)
    cache_kt = cache_k[0].transpose(0, 2, 3, 1).reshape(-1, d_att, page)
    cache_vt = cache_v[0].transpose(0, 2, 3, 1).reshape(-1, d_att, page)
    o_att_s = _moba_sample(aqs, aks, avs, cache_kt, cache_vt, page_table, bias_s, heads=att_heads)
    y_s = _out_proj(o_lin_s.reshape(nb * rows, d_lin), o_att_s.reshape(nb * rows, d_att),
                    gas.reshape(nb * rows, d_att), xs, w_out_bf, gpost)
    y_sample = y_s.reshape(nb, rows, d_model)[:, :ts]

    def prompt_kv(a):
        return a.reshape(1, b, att_heads, ATT_HEAD_DIM, t).transpose(0, 1, 4, 2, 3)

    kvs_shape = (1, nb, ts, att_heads, ATT_HEAD_DIM)
    return (y_prompt, y_sample, prompt_kv(akt), prompt_kv(avt), s_prompt[None],
            aks[:, :ts].reshape(kvs_shape), avs[:, :ts].reshape(kvs_shape), s_sample[None])
```

```python
import functools
import math

import jax
import jax.numpy as jnp
import numpy as np
from jax import lax
from jax.experimental import pallas as pl
from jax.experimental.pallas import tpu as pltpu

F32 = jnp.float32
BF16 = jnp.bfloat16

EPS = 1e-6
LIN_HEAD_DIM = 128
ATT_HEAD_DIM = 64
MOBA_BLOCK = 256
MOBA_TOPK = 3
N_BUCKETS = 32
MAX_DISTANCE = 1024
NEG = -2e30
M_INIT = -1e30
LOG2E = 1.4426950408889634
LANES = 128
SUBLANES = 8
VMEM_LIMIT = 56 * 1024 * 1024
HEADS_PER_TILE = LANES // ATT_HEAD_DIM

NT_DIMS = (((1,), (1,)), ((), ()))
TN_DIMS = (((0,), (0,)), ((), ()))


def _sigmoid(x):
    return 1.0 / (1.0 + jnp.exp(-x))


def _silu(x):
    return x * _sigmoid(x)


def _rms(x, g):
    return x * lax.rsqrt(jnp.mean(x * x, axis=-1, keepdims=True) + EPS) * g


def _in_head(idx, head):
    return (idx >= head * ATT_HEAD_DIM) & (idx < (head + 1) * ATT_HEAD_DIM)


def _in_proj_body(x_ref, gpre_ref, w_ref, llb_ref, *rest, d_lin, d_att, feat_major):
    if feat_major:
        wt_ref, q_o, g_o, kk_o, v_o, gl_o, ga_o, ak_o, aqt_o, akt_o, avt_o = rest
    else:
        q_o, g_o, kk_o, v_o, gl_o, ga_o, aq_o, ak_o, av_o = rest
    h = _rms(x_ref[...], gpre_ref[...]).astype(BF16)

    def proj(col, width):
        return jnp.dot(h, w_ref[:, col:col + width], preferred_element_type=F32)

    def proj_t(row, width):
        return lax.dot_general(wt_ref[row:row + width, :], h, NT_DIMS, preferred_element_type=F32)

    llb = llb_ref[...]
    e = jnp.exp(llb - jnp.max(llb, axis=0, keepdims=True))
    lb = e[0:1, :] / jnp.sum(e, axis=0, keepdims=True)

    q_o[...] = _silu(proj(0, d_lin))
    f = lb + (1.0 - lb) * _sigmoid(proj(d_lin, d_lin))
    g_o[...] = jnp.log(f)
    kk_o[...] = 1.0 - f
    v_o[...] = proj(2 * d_lin, d_lin)
    gl_o[...] = _silu(proj(3 * d_lin, d_lin))
    a0 = 4 * d_lin
    ga_o[...] = _silu(proj(a0 + 3 * d_att, d_att))
    if feat_major:
        aqt_o[0] = proj_t(0, d_att)
        akt = proj_t(d_att, d_att)
        akt_o[0] = akt
        ak_o[...] = akt.T
        avt_o[0] = proj_t(2 * d_att, d_att)
    else:
        aq_o[...] = proj(a0, d_att)
        ak_o[...] = proj(a0 + d_att, d_att)
        av_o[...] = proj(a0 + 2 * d_att, d_att)


def _in_proj(x, gpre, w_bf, llb, *, d_lin, d_att, seq_len=None, tm=256):
    rows, d_model = x.shape
    tm = min(tm, rows)
    assert rows % tm == 0
    d_in = w_bf.shape[1]
    feat_major = seq_len is not None
    row_spec = lambda w: pl.BlockSpec((tm, w), lambda i: (i, 0))
    in_specs = [
        pl.BlockSpec((tm, d_model), lambda i: (i, 0)),
        pl.BlockSpec((1, d_model), lambda i: (0, 0)),
        pl.BlockSpec((d_model, d_in), lambda i: (0, 0), pipeline_mode=pl.Buffered(1)),
        pl.BlockSpec(llb.shape, lambda i: (0, 0)),
    ]
    args = [x, gpre, w_bf, llb]
    out_specs = [row_spec(d_lin)] * 5 + [row_spec(d_att)]
    out_shape = [jax.ShapeDtypeStruct((rows, d_lin), F32)] * 5 + [jax.ShapeDtypeStruct((rows, d_att), F32)]
    if feat_major:
        assert seq_len % tm == 0
        tiles = seq_len // tm
        a0 = 4 * d_lin
        wt_bf = w_bf[:, a0:a0 + 3 * d_att].T
        in_specs.append(pl.BlockSpec(wt_bf.shape, lambda i: (0, 0), pipeline_mode=pl.Buffered(1)))
        args.append(wt_bf)
        t_spec = pl.BlockSpec((1, d_att, tm), lambda i: (i // tiles, 0, i % tiles))
        t_shape = jax.ShapeDtypeStruct((rows // seq_len, d_att, seq_len), F32)
        out_specs += [row_spec(d_att)] + [t_spec] * 3
        out_shape += [jax.ShapeDtypeStruct((rows, d_att), F32)] + [t_shape] * 3
    else:
        out_specs += [row_spec(d_att)] * 3
        out_shape += [jax.ShapeDtypeStruct((rows, d_att), F32)] * 3
    return pl.pallas_call(
        functools.partial(_in_proj_body, d_lin=d_lin, d_att=d_att, feat_major=feat_major),
        grid=(rows // tm,),
        in_specs=in_specs,
        out_specs=out_specs,
        out_shape=out_shape,
        compiler_params=pltpu.CompilerParams(
            dimension_semantics=("arbitrary",), vmem_limit_bytes=VMEM_LIMIT),
        name="in_proj",
    )(*args)


def _hgrn_step_body(q_ref, g_ref, kk_ref, v_ref, gl_ref, s0_ref, glin_ref, o_ref, sout_ref, b_scr,
                    *, rows, valid, n_seq):
    row = lax.broadcasted_iota(jnp.int32, (rows, 1), 0)
    for s in range(n_seq):
        acc = jnp.zeros((1, LIN_HEAD_DIM), F32)
        for r in range(rows):
            if r < valid:
                acc = acc + g_ref[s, r:r + 1, :]
            b_scr[s, r:r + 1, :] = acc
    seqs = range(n_seq)
    b = [b_scr[s] for s in seqs]
    b_last = [b_scr[s, rows - 1:rows, :] for s in seqs]
    q = [q_ref[s] for s in seqs]
    kk = [jnp.where(row < valid, kk_ref[s], 0.0) for s in seqs]

    o = [jnp.dot((q[s] * jnp.exp(b[s])).astype(BF16), s0_ref[s, 0].astype(BF16), preferred_element_type=F32)
         for s in seqs]
    for s in seqs:
        for j in range(valid):
            w = q[s] * kk_ref[s, j:j + 1, :] * jnp.exp(jnp.minimum(b[s] - b_scr[s, j:j + 1, :], 0.0))
            a = jnp.sum(w, axis=-1, keepdims=True)
            o[s] = o[s] + jnp.where(row >= j, a, 0.0) * v_ref[s, j:j + 1, :]
    u = [lax.dot_general((kk[s] * jnp.exp(b_last[s] - b[s])).astype(BF16), v_ref[s].astype(BF16), TN_DIMS,
                         preferred_element_type=F32) for s in seqs]
    ones = jnp.ones((rows, LIN_HEAD_DIM), BF16)
    decay = []
    for s in seqs:
        rest = jnp.exp(b_last[s])
        pieces = jnp.zeros((rows, LIN_HEAD_DIM), F32)
        for pi in range(3):
            piece = rest.astype(BF16).astype(F32)
            rest = rest - piece
            pieces = jnp.where(row == pi, piece, pieces)
        decay.append(lax.dot_general(pieces.astype(BF16), ones, TN_DIMS, preferred_element_type=F32))
    for s in seqs:
        sout_ref[s, 0] = s0_ref[s, 0] * decay[s] + u[s]
        o_ref[s] = _rms(o[s], glin_ref[...]) * gl_ref[s]


def _hgrn_step(q, g, kk, v, gl, s0, glin, *, valid, n_seq):
    nb, rows, d_lin = q.shape
    heads = d_lin // LIN_HEAD_DIM
    assert rows >= 3 and nb % n_seq == 0
    seq_spec = pl.BlockSpec((n_seq, rows, LIN_HEAD_DIM), lambda b, h: (b, 0, h))
    st_spec = pl.BlockSpec((n_seq, 1, LIN_HEAD_DIM, LIN_HEAD_DIM), lambda b, h: (b, h, 0, 0))
    return pl.pallas_call(
        functools.partial(_hgrn_step_body, rows=rows, valid=valid, n_seq=n_seq),
        grid=(nb // n_seq, heads),
        in_specs=[seq_spec] * 5 + [st_spec, pl.BlockSpec((1, LIN_HEAD_DIM), lambda b, h: (0, 0))],
        out_specs=[seq_spec, st_spec],
        out_shape=[jax.ShapeDtypeStruct((nb, rows, d_lin), F32),
                   jax.ShapeDtypeStruct(s0.shape, F32)],
        scratch_shapes=[pltpu.VMEM((n_seq, rows, LIN_HEAD_DIM), F32)],
        compiler_params=pltpu.CompilerParams(dimension_semantics=("arbitrary", "arbitrary")),
        name="hgrn_step",
    )(q, g, kk, v, gl, s0, glin)


SAFE_DECAY = 60.0


def _hgrn_tile_body(q_ref, g_ref, kk_ref, v_ref, gl_ref, s0_ref, glin_ref, o_ref, sout_ref,
                    st_scr, b_scr, oi_scr, u_scr, kkp_scr, bp_scr, vp_scr, *, chunk, t_tile, heads):
    it = pl.program_id(1)
    hs = [slice(h * LIN_HEAD_DIM, (h + 1) * LIN_HEAD_DIM) for h in range(heads)]

    @pl.when(it == 0)
    def _():
        for h in range(heads):
            st_scr[h] = s0_ref[0, h].T

    lg = int(math.log2(chunk))
    n_groups = t_tile // LANES
    n_chunks = t_tile // chunk
    r = lax.broadcasted_iota(jnp.int32, (LANES, LANES), 0)
    c = lax.broadcasted_iota(jnp.int32, (LANES, LANES), 1)
    causal = ((r >> lg) == (c >> lg)) & (c <= r)
    tri = jnp.where(causal, 1.0, 0.0).astype(BF16)
    n_pieces = 3
    for h in range(heads):
        pieces = []
        for gi in range(n_groups):
            rest = g_ref[0, gi * LANES:(gi + 1) * LANES, hs[h]]
            for _ in range(n_pieces):
                piece = rest.astype(BF16)
                rest = rest - piece.astype(F32)
                pieces.append(piece)
        sums = jnp.dot(tri, jnp.concatenate(pieces, axis=1), preferred_element_type=F32)
        for gi in range(n_groups):
            b = jnp.zeros((LANES, LIN_HEAD_DIM), F32)
            for pi in range(n_pieces):
                col = (gi * n_pieces + pi) * LIN_HEAD_DIM
                b = b + sums[:, col:col + LIN_HEAD_DIM]
            b_scr[h, gi * LANES:(gi + 1) * LANES, :] = b

    def intra_mxu():
        atts = {}
        for h in range(heads):
            for gi in range(n_groups):
                rows = slice(gi * LANES, (gi + 1) * LANES)
                b = b_scr[h, rows, :]
                qe = (q_ref[0, rows, hs[h]] * jnp.exp(b)).astype(BF16)
                ke = (kk_ref[0, rows, hs[h]] * jnp.exp(-b)).astype(BF16)
                att = lax.dot_general(qe, ke, NT_DIMS, preferred_element_type=F32)
                atts[h, gi] = jnp.where(causal, att, 0.0).astype(BF16)
        for h in range(heads):
            for gi in range(n_groups):
                rows = slice(gi * LANES, (gi + 1) * LANES)
                oi_scr[h, rows, :] = jnp.dot(atts[h, gi], v_ref[0, rows, hs[h]].astype(BF16),
                                             preferred_element_type=F32)

    def intra_pairs():
        zeros = jnp.zeros((chunk, LIN_HEAD_DIM), F32)
        tmod = lax.broadcasted_iota(jnp.int32, (t_tile, 1), 0) & (chunk - 1)
        for h in range(heads):
            for dst, src in ((kkp_scr, kk_ref[0, :, hs[h]]), (bp_scr, b_scr[h]), (vp_scr, v_ref[0, :, hs[h]])):
                dst[0:chunk, :] = zeros
                dst[chunk:, :] = src
            oi_scr[h] = jnp.zeros((t_tile, LIN_HEAD_DIM), F32)
            q = q_ref[0, :, hs[h]]
            b = b_scr[h]

            def lag(d, carry, h=h, q=q, b=b):
                start = chunk - d
                w = q * kkp_scr[pl.ds(start, t_tile), :] * jnp.exp(
                    jnp.minimum(b - bp_scr[pl.ds(start, t_tile), :], 0.0))
                a = jnp.sum(w, axis=-1, keepdims=True)
                oi_scr[h] += jnp.where(tmod >= d, a, 0.0) * vp_scr[pl.ds(start, t_tile), :]
                return carry

            lax.fori_loop(0, chunk, lag, 0)

    lax.cond(jnp.min(b_scr[...]) >= -SAFE_DECAY, intra_mxu, intra_pairs)

    for h in range(heads):
        for n in range(n_chunks):
            rows = slice(n * chunk, (n + 1) * chunk)
            b_last = b_scr[h, (n + 1) * chunk - 1:(n + 1) * chunk, :]
            kt = (kk_ref[0, rows, hs[h]] * jnp.exp(b_last - b_scr[h, rows, :])).astype(BF16)
            u_scr[h, n] = lax.dot_general(v_ref[0, rows, hs[h]].astype(BF16), kt, TN_DIMS,
                                          preferred_element_type=F32)

    st = [st_scr[h] for h in range(heads)]
    for n in range(n_chunks):
        rows = slice(n * chunk, (n + 1) * chunk)
        for h in range(heads):
            b_last = b_scr[h, (n + 1) * chunk - 1:(n + 1) * chunk, :]
            o = lax.dot_general((q_ref[0, rows, hs[h]] * jnp.exp(b_scr[h, rows, :])).astype(BF16),
                                st[h].astype(BF16), NT_DIMS, preferred_element_type=F32) + oi_scr[h, rows, :]
            st[h] = st[h] * jnp.exp(b_last) + u_scr[h, n]
            o_ref[0, rows, hs[h]] = (_rms(o, glin_ref[...]) * gl_ref[0, rows, hs[h]]).astype(o_ref.dtype)
    for h in range(heads):
        st_scr[h] = st[h]

    @pl.when(it == pl.num_programs(1) - 1)
    def _():
        for h in range(heads):
            sout_ref[0, h] = st[h].T


def _hgrn_tiles(q, g, kk, v, gl, s0, glin, *, chunk, t_tile):
    nb, t, d_lin = q.shape
    heads = d_lin // LIN_HEAD_DIM
    assert LANES % chunk == 0 and t_tile % LANES == 0 and t % t_tile == 0
    seq_spec = pl.BlockSpec((1, t_tile, d_lin), lambda b, i: (b, i, 0))
    st_spec = pl.BlockSpec((1, heads, LIN_HEAD_DIM, LIN_HEAD_DIM), lambda b, i: (b, 0, 0, 0))
    tile = pltpu.VMEM((heads, t_tile, LIN_HEAD_DIM), F32)
    halo = pltpu.VMEM((chunk + t_tile, LIN_HEAD_DIM), F32)
    return pl.pallas_call(
        functools.partial(_hgrn_tile_body, chunk=chunk, t_tile=t_tile, heads=heads),
        grid=(nb, t // t_tile),
        in_specs=[seq_spec] * 5 + [st_spec, pl.BlockSpec((1, LIN_HEAD_DIM), lambda b, i: (0, 0))],
        out_specs=[seq_spec, st_spec],
        out_shape=[jax.ShapeDtypeStruct((nb, t, d_lin), BF16),
                   jax.ShapeDtypeStruct(s0.shape, F32)],
        scratch_shapes=[pltpu.VMEM((heads, LIN_HEAD_DIM, LIN_HEAD_DIM), F32), tile, tile,
                        pltpu.VMEM((heads, t_tile // chunk, LIN_HEAD_DIM, LIN_HEAD_DIM), F32),
                        halo, halo, halo],
        compiler_params=pltpu.CompilerParams(
            dimension_semantics=("arbitrary", "arbitrary"), vmem_limit_bytes=VMEM_LIMIT),
        name="hgrn_tiles",
    )(q, g, kk, v, gl, s0, glin)


def _t5_first_distances():
    max_exact = N_BUCKETS // 2
    n = np.arange(0, MAX_DISTANCE + 1, dtype=np.int32)
    ratio = np.maximum(n, 1).astype(np.float32) / np.float32(max_exact)
    large = max_exact + (np.log(ratio) / np.float32(math.log(MAX_DISTANCE / max_exact))
                         * np.float32(N_BUCKETS - max_exact)).astype(np.int32)
    bucket = np.where(n < max_exact, n, np.minimum(large, N_BUCKETS - 1))
    assert np.all(np.diff(bucket) >= 0) and bucket[-1] == N_BUCKETS - 1
    return [int(np.argmax(bucket >= bk)) for bk in range(N_BUCKETS)]


_T5_FIRST_DISTANCE = _t5_first_distances()


def _t5_bias(dist, rb_ref, head):
    n = jnp.maximum(dist, 0)
    out = jnp.full(dist.shape, rb_ref[0, head], F32)
    for bk in range(1, N_BUCKETS):
        out = jnp.where(n >= _T5_FIRST_DISTANCE[bk], rb_ref[bk, head], out)
    return out


def _prompt_bias_body(rb_ref, o_ref, *, n_tiles):
    h = pl.program_id(0)
    x = lax.broadcasted_iota(jnp.int32, (SUBLANES, 2 * MOBA_BLOCK), 1)
    for d in range(n_tiles):
        dist = (d - 1) * MOBA_BLOCK + x
        by_dist = jnp.where(dist >= 0, _t5_bias(dist, rb_ref, h) * LOG2E, NEG)
        wide = jnp.broadcast_to(by_dist[0:1, :], (MOBA_BLOCK, 2 * MOBA_BLOCK))
        o_ref[0, d] = pltpu.roll(wide, 0, 1, stride=1, stride_axis=0)[:, MOBA_BLOCK:]


def _prompt_bias(rel_bias, n_tiles):
    heads = rel_bias.shape[1]
    return pl.pallas_call(
        functools.partial(_prompt_bias_body, n_tiles=n_tiles),
        grid=(heads,),
        in_specs=[pl.BlockSpec(memory_space=pltpu.SMEM)],
        out_specs=pl.BlockSpec((1, n_tiles, MOBA_BLOCK, MOBA_BLOCK), lambda h: (h, 0, 0, 0)),
        out_shape=jax.ShapeDtypeStruct((heads, n_tiles, MOBA_BLOCK, MOBA_BLOCK), F32),
        name="prompt_bias",
    )(rel_bias)


def _sample_bias_body(rb_ref, o_ref, *, past_len, n_valid, rows):
    h = pl.program_id(0)
    width = past_len + LANES
    r = lax.broadcasted_iota(jnp.int32, (rows, width), 0)
    kpos = lax.broadcasted_iota(jnp.int32, (rows, width), 1)
    dist = past_len + r - kpos
    ok = (dist >= 0) & (kpos < past_len + n_valid)
    o_ref[...] = jnp.where(ok, _t5_bias(dist, rb_ref, h), NEG)


def _sample_bias(rel_bias, *, past_len, n_valid, rows):
    heads = rel_bias.shape[1]
    width = past_len + LANES
    return pl.pallas_call(
        functools.partial(_sample_bias_body, past_len=past_len, n_valid=n_valid, rows=rows),
        grid=(heads,),
        in_specs=[pl.BlockSpec(memory_space=pltpu.SMEM)],
        out_specs=pl.BlockSpec((rows, width), lambda h: (h, 0)),
        out_shape=jax.ShapeDtypeStruct((heads * rows, width), F32),
        name="sample_bias",
    )(rel_bias)


def _topk_keep(gate, idx, own, axis, n):
    past = idx < own
    gm = jnp.where(past, gate, -jnp.inf)
    rank = jnp.zeros(gate.shape, jnp.int32)
    for jp in range(n):
        gj = lax.slice_in_dim(gm, jp, jp + 1, axis=axis)
        beats = (gj > gm) | ((gj == gm) & (jp < idx))
        rank = rank + beats.astype(jnp.int32)
    keep = ((rank < MOBA_TOPK) & past) | (idx == own)
    return jnp.where(keep, 0.0, NEG)


def _topk_keep_rows(gate, idx, own):
    past = idx < own
    gm = jnp.where(past, gate, -jnp.inf)
    idx_f = idx.astype(F32)
    chosen = jnp.zeros(gate.shape, jnp.bool_)
    for _ in range(MOBA_TOPK):
        top = jnp.max(gm, axis=0, keepdims=True)
        first = jnp.min(jnp.where(gm == top, idx_f, float(gate.shape[0])), axis=0, keepdims=True)
        pick = idx_f == first
        chosen = chosen | pick
        gm = jnp.where(pick, -jnp.inf, gm)
    keep = (chosen & past) | (idx == own)
    return jnp.where(keep, 0.0, NEG)


def _moba_gate_body(qt_ref, k_ref, a_ref, km_scr, *, n_blocks):
    t = n_blocks * MOBA_BLOCK
    qt = qt_ref[0]
    lane = lax.broadcasted_iota(jnp.int32, (1, LANES), 1)
    for n in range(n_blocks):
        km_scr[pl.ds(n, 1), :] = jnp.sum(
            k_ref[0, n * MOBA_BLOCK:(n + 1) * MOBA_BLOCK, :], axis=0, keepdims=True) * (1.0 / MOBA_BLOCK)
    km = km_scr[...]
    blk = lax.broadcasted_iota(jnp.int32, (n_blocks, t), 0)
    own = lax.broadcasted_iota(jnp.int32, (n_blocks, t), 1) >> int(math.log2(MOBA_BLOCK))
    for hh in range(HEADS_PER_TILE):
        kmh = jnp.where(_in_head(lane, hh), km, 0.0)
        gate_t = jnp.dot(kmh, qt, precision=lax.Precision.HIGHEST, preferred_element_type=F32)
        a_ref[0, 0, hh * n_blocks:(hh + 1) * n_blocks, :] = _topk_keep_rows(gate_t, blk, own)


def _moba_gate(aqt, ak):
    b, d_att, t = aqt.shape
    tiles = d_att // LANES
    n_blocks = t // MOBA_BLOCK
    return pl.pallas_call(
        functools.partial(_moba_gate_body, n_blocks=n_blocks),
        grid=(b, tiles),
        in_specs=[pl.BlockSpec((1, LANES, t), lambda i, p: (i, p, 0)),
                  pl.BlockSpec((1, t, LANES), lambda i, p: (i, 0, p))],
        out_specs=pl.BlockSpec((1, 1, HEADS_PER_TILE * n_blocks, t), lambda i, p: (i, p, 0, 0)),
        out_shape=jax.ShapeDtypeStruct((b, tiles, HEADS_PER_TILE * n_blocks, t), F32),
        scratch_shapes=[pltpu.VMEM((n_blocks, LANES), F32)],
        compiler_params=pltpu.CompilerParams(
            dimension_semantics=("arbitrary", "arbitrary"), vmem_limit_bytes=VMEM_LIMIT),
        name="moba_gate",
    )(aqt, ak)


def _moba_prompt_body(qt_ref, k_ref, vt_ref, keep_ref, bias_ref, ga_ref, o_ref, kaug_scr, vtaug_scr, s_scr,
                      *, n_bias, n_blocks, group, tps):
    i = pl.program_id(2)
    rows = group * MOBA_BLOCK
    t = n_blocks * MOBA_BLOCK
    lg_block = int(math.log2(MOBA_BLOCK))
    feat = lax.broadcasted_iota(jnp.int32, (LANES, 1), 0)
    heads = [(tt, hh) for tt in range(tps) for hh in range(HEADS_PER_TILE)]
    tile = [slice(tt * LANES, (tt + 1) * LANES) for tt in range(tps)]
    ones_row = [((hh + 1) % HEADS_PER_TILE) * ATT_HEAD_DIM for hh in range(HEADS_PER_TILE)]

    @pl.when(i == 0)
    def _():
        blk = lax.broadcasted_iota(jnp.int32, (rows, LANES), 0) >> lg_block
        lane = lax.broadcasted_iota(jnp.int32, (rows, LANES), 1)
        for c in range(t // rows):
            sl = slice(c * rows, (c + 1) * rows)
            for hd, (tt, hh) in enumerate(heads):
                onehot = jnp.where(lane == hh * n_blocks + c * group + blk, 1.0, 0.0)
                kaug_scr[hd, sl, :] = jnp.concatenate([k_ref[0, sl, tile[tt]], onehot], axis=1).astype(BF16)
                vtaug_scr[hd, :, sl] = jnp.where(feat == ones_row[hh], 1.0, vt_ref[0, tile[tt], sl]).astype(BF16)

    q_aug = []
    for tt, hh in heads:
        qt = qt_ref[0, tile[tt], :] * (ATT_HEAD_DIM ** -0.5 * LOG2E)
        keep = keep_ref[0, tt]
        pad = jnp.zeros((LANES - keep.shape[0], MOBA_BLOCK), F32)
        q_aug.append(jnp.concatenate([jnp.where(_in_head(feat, hh), qt, 0.0), keep, pad], axis=0).astype(BF16))
    n_groups = (i >> int(math.log2(group))) + 1
    n_heads = len(heads)

    def scores(gi, m):
        r0 = pl.multiple_of(gi * rows, rows)
        m_new = []
        for hd in range(n_heads):
            s = jnp.dot(kaug_scr[hd, pl.ds(r0, rows), :], q_aug[hd], preferred_element_type=F32)
            mh = m[hd]
            for u in range(group):
                d = jnp.clip(i - (gi * group + u), 0, n_bias - 1)
                su = s[u * MOBA_BLOCK:(u + 1) * MOBA_BLOCK] + bias_ref[hd, d]
                s_scr[hd, pl.ds(r0 + u * MOBA_BLOCK, MOBA_BLOCK), :] = su
                mh = jnp.maximum(mh, jnp.max(su, axis=0, keepdims=True))
            m_new.append(mh)
        return tuple(m_new)

    def values(gi, m, m_before, acc):
        r0 = pl.multiple_of(gi * rows, rows)
        out = []
        for hd in range(n_heads):
            p = jnp.exp2(s_scr[hd, pl.ds(r0, rows), :] - m[hd]).astype(BF16)
            out.append(acc[hd] * jnp.exp2(m_before[hd] - m[hd])
                       + jnp.dot(vtaug_scr[hd, :, pl.ds(r0, rows)], p, preferred_element_type=F32))
        return tuple(out)

    m_init = tuple(jnp.full((1, MOBA_BLOCK), M_INIT, F32) for _ in range(n_heads))
    acc0 = tuple(jnp.zeros((LANES, MOBA_BLOCK), F32) for _ in range(n_heads))

    def stage(gi, carry):
        m, m_before, acc = carry
        acc = values(gi - 1, m, m_before, acc)
        return scores(gi, m), m, acc

    m, m_before, acc = lax.fori_loop(1, n_groups, stage, (scores(0, m_init), m_init, acc0))
    acc = values(n_groups - 1, m, m_before, acc)
    for tt in range(tps):
        out = jnp.zeros((LANES, MOBA_BLOCK), F32)
        for hh in range(HEADS_PER_TILE):
            a = acc[tt * HEADS_PER_TILE + hh]
            out = jnp.where(_in_head(feat, hh), a / a[ones_row[hh]:ones_row[hh] + 1, :], out)
        o_ref[0, :, tile[tt]] = (out.T * ga_ref[0, :, tile[tt]]).astype(o_ref.dtype)


def _moba_prompt(aqt, ak, avt, keep, bias, ga, *, group=4, tps=2):
    b, d_att, t = aqt.shape
    tiles = d_att // LANES
    n_blocks = t // MOBA_BLOCK
    n_bias = bias.shape[1]
    group = min(group, n_blocks)
    assert n_blocks % group == 0 and group & (group - 1) == 0 and tiles % tps == 0
    width = tps * LANES
    n_heads = tps * HEADS_PER_TILE
    once = dict(pipeline_mode=pl.Buffered(1))
    return pl.pallas_call(
        functools.partial(_moba_prompt_body, n_bias=n_bias, n_blocks=n_blocks, group=group, tps=tps),
        grid=(b, tiles // tps, n_blocks),
        in_specs=[pl.BlockSpec((1, width, MOBA_BLOCK), lambda ib, p, i: (ib, p, i)),
                  pl.BlockSpec((1, t, width), lambda ib, p, i: (ib, 0, p), **once),
                  pl.BlockSpec((1, width, t), lambda ib, p, i: (ib, p, 0), **once),
                  pl.BlockSpec((1, tps, keep.shape[2], MOBA_BLOCK), lambda ib, p, i: (ib, p, 0, i)),
                  pl.BlockSpec((n_heads, n_bias, MOBA_BLOCK, MOBA_BLOCK), lambda ib, p, i: (p, 0, 0, 0), **once),
                  pl.BlockSpec((1, MOBA_BLOCK, width), lambda ib, p, i: (ib, i, p))],
        out_specs=pl.BlockSpec((1, MOBA_BLOCK, width), lambda ib, p, i: (ib, i, p)),
        out_shape=jax.ShapeDtypeStruct((b, t, d_att), BF16),
        scratch_shapes=[pltpu.VMEM((n_heads, t, 2 * LANES), BF16),
                        pltpu.VMEM((n_heads, LANES, t), BF16),
                        pltpu.VMEM((n_heads, t, MOBA_BLOCK), F32)],
        compiler_params=pltpu.CompilerParams(
            dimension_semantics=("arbitrary", "arbitrary", "arbitrary"), vmem_limit_bytes=VMEM_LIMIT),
        name="moba_prompt",
    )(aqt, ak, avt, keep, bias, ga.reshape(b, t, d_att))


def _moba_sample_body(pt_ref, q_ref, kn_ref, vn_ref, bias_ref, ck_hbm, cv_hbm, o_ref,
                      kbuf, vbuf, sems, s_scr, *, n_pages, heads, rows):
    b = pl.program_id(0)
    n_slots = kbuf.shape[0]
    ahead = n_slots - 1
    slot = lax.rem(b, n_slots)

    def page_copies(seq, sl):
        out = []
        for p in range(n_pages):
            phys = pt_ref[seq, p]
            out.append(pltpu.make_async_copy(ck_hbm.at[phys], kbuf.at[sl, p], sems.at[0, sl]))
            out.append(pltpu.make_async_copy(cv_hbm.at[phys], vbuf.at[sl, p], sems.at[1, sl]))
        return out

    @pl.when(b == 0)
    def _():
        for seq in range(ahead):
            for cp in page_copies(seq, seq):
                cp.start()

    @pl.when(b + ahead < pl.num_programs(0))
    def _():
        for cp in page_copies(b + ahead, lax.rem(b + ahead, n_slots)):
            cp.start()

    for cp in page_copies(b, slot):
        cp.wait()
    k_pages = [kbuf.at[slot, p] for p in range(n_pages)]
    v_pages = [vbuf.at[slot, p] for p in range(n_pages)]
    d_att = heads * ATT_HEAD_DIM
    page = kbuf.shape[-1]
    pages_per_block = MOBA_BLOCK // page
    n_blocks = n_pages // pages_per_block
    n_q = heads * rows
    feat = lax.broadcasted_iota(jnp.int32, (1, d_att), 1)
    lane = lax.broadcasted_iota(jnp.int32, (n_q, LANES), 1)

    q = q_ref[0] * (ATT_HEAD_DIM ** -0.5)
    q_bd = jnp.concatenate([jnp.where(_in_head(feat, h), q, 0.0) for h in range(heads)], axis=0).astype(BF16)

    gate = jnp.zeros((n_q, LANES), F32)
    for n in range(n_blocks):
        tot = jnp.zeros((n_q, page), F32)
        for p in range(n * pages_per_block, (n + 1) * pages_per_block):
            kt = k_pages[p][...].astype(BF16)
            s = jnp.dot(q_bd, kt, preferred_element_type=F32)
            s_scr[:, p * page:(p + 1) * page] = s
            tot = tot + s
        gate = jnp.where(lane == n, jnp.sum(tot, axis=-1, keepdims=True) * (1.0 / MOBA_BLOCK), gate)
    keep = _topk_keep(gate, lane, jnp.full(gate.shape, n_blocks, jnp.int32), axis=1, n=n_blocks)

    def new_rows(ref):
        return jnp.concatenate([ref[0], jnp.zeros((LANES - rows, d_att), F32)], axis=0).astype(BF16)

    own = n_pages * page
    s_own = lax.dot_general(q_bd, new_rows(kn_ref), NT_DIMS, preferred_element_type=F32)
    s_own = s_own + bias_ref[:, own:own + LANES]
    s_scr[:, own:own + LANES] = s_own
    m_wide = s_own
    for p in range(n_pages):
        n = p // pages_per_block
        s = s_scr[:, p * page:(p + 1) * page] + bias_ref[:, p * page:(p + 1) * page] + keep[:, n:n + 1]
        s_scr[:, p * page:(p + 1) * page] = s
        m_wide = jnp.maximum(m_wide, s)
    m = jnp.max(m_wide, axis=-1, keepdims=True)

    pr = jnp.exp(s_scr[:, own:own + LANES] - m)
    l_wide = pr
    acc = jnp.dot(pr.astype(BF16), new_rows(vn_ref), preferred_element_type=F32)
    for p in range(n_pages):
        pr = jnp.exp(s_scr[:, p * page:(p + 1) * page] - m)
        l_wide = l_wide + pr
        vt = v_pages[p][...].astype(BF16)
        acc = acc + lax.dot_general(pr.astype(BF16), vt, NT_DIMS, preferred_element_type=F32)
    acc = acc / jnp.sum(l_wide, axis=-1, keepdims=True)
    out = jnp.zeros((rows, d_att), F32)
    for h in range(heads):
        out = jnp.where(_in_head(feat, h), acc[h * rows:(h + 1) * rows, :], out)
    o_ref[0] = out


def _moba_sample(aq, ak, av, cache_kt, cache_vt, page_table, bias, *, heads):
    nb, rows, d_att = aq.shape
    n_pages = page_table.shape[1]
    page = cache_kt.shape[-1]
    assert cache_kt.shape[1:] == (d_att, page) and MOBA_BLOCK % page == 0 and page == LANES
    tok = pl.BlockSpec((1, rows, d_att), lambda b, pt: (b, 0, 0))
    in_hbm = pl.BlockSpec(memory_space=pl.ANY)
    n_slots = 3
    assert nb >= n_slots
    page_slots = pltpu.VMEM((n_slots, n_pages, d_att, page), F32)
    grid_spec = pltpu.PrefetchScalarGridSpec(
        num_scalar_prefetch=1,
        grid=(nb,),
        in_specs=[tok, tok, tok, pl.BlockSpec(bias.shape, lambda b, pt: (0, 0)), in_hbm, in_hbm],
        out_specs=tok,
        scratch_shapes=[page_slots, page_slots, pltpu.SemaphoreType.DMA((2, n_slots)),
                        pltpu.VMEM(bias.shape, F32)],
    )
    return pl.pallas_call(
        functools.partial(_moba_sample_body, n_pages=n_pages, heads=heads, rows=rows),
        grid_spec=grid_spec,
        out_shape=jax.ShapeDtypeStruct((nb, rows, d_att), F32),
        compiler_params=pltpu.CompilerParams(
            dimension_semantics=("arbitrary",), vmem_limit_bytes=VMEM_LIMIT),
        name="moba_sample",
    )(page_table, aq, ak, av, bias, cache_kt, cache_vt)


def _out_proj_body(*refs, d_lin, gated):
    if gated:
        ol_ref, oa_ref, x_ref, w_ref, gpost_ref, y_ref = refs
        oa = oa_ref[...]
    else:
        ol_ref, oa_ref, ga_ref, x_ref, w_ref, gpost_ref, y_ref = refs
        oa = oa_ref[...] * ga_ref[...]
    o = jnp.dot(ol_ref[...].astype(BF16), w_ref[0:d_lin, :], preferred_element_type=F32)
    o = o + jnp.dot(oa.astype(BF16), w_ref[d_lin:, :], preferred_element_type=F32)
    y_ref[...] = x_ref[...] + _rms(o, gpost_ref[...])


def _out_proj(o_lin, o_att, ga, x, w_bf, gpost, *, tm=256):
    rows, d_model = x.shape
    tm = min(tm, rows)
    assert rows % tm == 0
    d_lin = o_lin.shape[1]
    row_spec = lambda w: pl.BlockSpec((tm, w), lambda i: (i, 0))
    acts = [o_lin, o_att] + ([] if ga is None else [ga])
    return pl.pallas_call(
        functools.partial(_out_proj_body, d_lin=d_lin, gated=ga is None),
        grid=(rows // tm,),
        in_specs=[row_spec(a.shape[1]) for a in acts] + [
            row_spec(d_model),
            pl.BlockSpec(w_bf.shape, lambda i: (0, 0)),
            pl.BlockSpec((1, d_model), lambda i: (0, 0)),
        ],
        out_specs=row_spec(d_model),
        out_shape=jax.ShapeDtypeStruct((rows, d_model), F32),
        compiler_params=pltpu.CompilerParams(dimension_semantics=("arbitrary",)),
        name="out_proj",
    )(*acts, x, w_bf, gpost)


def kernel(x_prompt, x_sample, cache_k, cache_v, state_hgrn, page_table, w_in, w_out,
           norm_pre, norm_post, norm_lin_out, lin_lower_bound, rel_bias):
    depth = w_in.shape[0]
    assert depth == 1 and lin_lower_bound.shape[0] == depth + 1
    b, t, d_model = x_prompt.shape
    nb, ts, _ = x_sample.shape
    d_lin = lin_lower_bound.shape[1]
    d_att = w_out.shape[1] - d_lin
    lin_heads = d_lin // LIN_HEAD_DIM
    att_heads = rel_bias.shape[1]
    n_pages = page_table.shape[1]
    page = cache_k.shape[2]
    past_len = n_pages * page
    assert d_att == att_heads * ATT_HEAD_DIM and t % MOBA_BLOCK == 0
    assert past_len % MOBA_BLOCK == 0 and ts <= SUBLANES

    w_in_bf = w_in[0].astype(BF16)
    w_out_bf = w_out[0].astype(BF16)
    gpre, gpost, glin = norm_pre, norm_post, norm_lin_out
    proj = functools.partial(_in_proj, gpre=gpre, w_bf=w_in_bf, llb=lin_lower_bound,
                             d_lin=d_lin, d_att=d_att)

    xp = x_prompt.reshape(b * t, d_model)
    q, g, kk, v, gl, ga, ak, aqt, akt, avt = proj(xp, seq_len=t, tm=512)
    q, g, kk, v, gl, ak = [a.reshape(b, t, -1) for a in (q, g, kk, v, gl, ak)]
    s0 = jnp.zeros((b, lin_heads, LIN_HEAD_DIM, LIN_HEAD_DIM), F32)
    o_lin, s_prompt = _hgrn_tiles(q, g, kk, v, gl, s0, glin, chunk=32, t_tile=512)
    keep = _moba_gate(aqt, ak)
    n_bias = 6
    assert (n_bias - 1) * MOBA_BLOCK - (MOBA_BLOCK - 1) >= MAX_DISTANCE
    bias = _prompt_bias(rel_bias, n_bias)
    o_att = _moba_prompt(aqt, ak, avt, keep, bias, ga)
    y_prompt = _out_proj(o_lin.reshape(b * t, d_lin), o_att.reshape(b * t, d_att),
                         None, xp, w_out_bf, gpost, tm=1024).reshape(b, t, d_model)

    rows = SUBLANES
    xs = jnp.pad(x_sample, ((0, 0), (0, rows - ts), (0, 0))).reshape(nb * rows, d_model)
    qs, gs, kks, vs, gls, gas, aqs, aks, avs = [a.reshape(nb, rows, -1) for a in proj(xs)]
    o_lin_s, s_sample = _hgrn_step(qs, gs, kks, vs, gls, state_hgrn[0], glin, valid=ts,
                                   n_seq=math.gcd(nb, 32))
    bias_s = _sample_bias(rel_bias, past_len=past_len, n_valid=ts, rows=rows)
    cache_kt = cache_k[0].transpose(0, 2, 3, 1).reshape(-1, d_att, page)
    cache_vt = cache_v[0].transpose(0, 2, 3, 1).reshape(-1, d_att, page)
    o_att_s = _moba_sample(aqs, aks, avs, cache_kt, cache_vt, page_table, bias_s, heads=att_heads)
    y_s = _out_proj(o_lin_s.reshape(nb * rows, d_lin), o_att_s.reshape(nb * rows, d_att),
                    gas.reshape(nb * rows, d_att), xs, w_out_bf, gpost)
    y_sample = y_s.reshape(nb, rows, d_model)[:, :ts]

    def prompt_kv(a):
        return a.reshape(1, b, att_heads, ATT_HEAD_DIM, t).transpose(0, 1, 4, 2, 3)

    kvs_shape = (1, nb, ts, att_heads, ATT_HEAD_DIM)
    return (y_prompt, y_sample, prompt_kv(akt), prompt_kv(avt), s_prompt[None],
            aks[:, :ts].reshape(kvs_shape), avs[:, :ts].reshape(kvs_shape), s_sample[None])
```

```python
import functools
import math

import jax
import jax.numpy as jnp
import numpy as np
from jax import lax
from jax.experimental import pallas as pl
from jax.experimental.pallas import tpu as pltpu

F32 = jnp.float32
BF16 = jnp.bfloat16

EPS = 1e-6
LIN_HEAD_DIM = 128
ATT_HEAD_DIM = 64
MOBA_BLOCK = 256
MOBA_TOPK = 3
N_BUCKETS = 32
MAX_DISTANCE = 1024
NEG = -2e30
M_INIT = -1e30
LOG2E = 1.4426950408889634
LANES = 128
SUBLANES = 8
VMEM_LIMIT = 56 * 1024 * 1024
HEADS_PER_TILE = LANES // ATT_HEAD_DIM

NT_DIMS = (((1,), (1,)), ((), ()))
TN_DIMS = (((0,), (0,)), ((), ()))


def _sigmoid(x):
    return 1.0 / (1.0 + jnp.exp(-x))


def _silu(x):
    return x * _sigmoid(x)


def _rms(x, g):
    return x * lax.rsqrt(jnp.mean(x * x, axis=-1, keepdims=True) + EPS) * g


def _in_head(idx, head):
    return (idx >= head * ATT_HEAD_DIM) & (idx < (head + 1) * ATT_HEAD_DIM)


def _in_proj_body(x_ref, gpre_ref, w_ref, llb_ref, *rest, d_lin, d_att, feat_major):
    if feat_major:
        wt_ref, q_o, g_o, kk_o, v_o, gl_o, ga_o, ak_o, aqt_o, akt_o, avt_o = rest
    else:
        q_o, g_o, kk_o, v_o, gl_o, ga_o, aq_o, ak_o, av_o = rest
    h = _rms(x_ref[...], gpre_ref[...]).astype(BF16)

    def proj(col, width):
        return jnp.dot(h, w_ref[:, col:col + width], preferred_element_type=F32)

    def proj_t(row, width):
        return lax.dot_general(wt_ref[row:row + width, :], h, NT_DIMS, preferred_element_type=F32)

    llb = llb_ref[...]
    e = jnp.exp(llb - jnp.max(llb, axis=0, keepdims=True))
    lb = e[0:1, :] / jnp.sum(e, axis=0, keepdims=True)

    q_o[...] = _silu(proj(0, d_lin))
    f = lb + (1.0 - lb) * _sigmoid(proj(d_lin, d_lin))
    g_o[...] = jnp.log(f)
    kk_o[...] = 1.0 - f
    v_o[...] = proj(2 * d_lin, d_lin)
    gl_o[...] = _silu(proj(3 * d_lin, d_lin))
    a0 = 4 * d_lin
    ga_o[...] = _silu(proj(a0 + 3 * d_att, d_att))
    if feat_major:
        aqt_o[0] = proj_t(0, d_att)
        akt = proj_t(d_att, d_att)
        akt_o[0] = akt
        ak_o[...] = akt.T
        avt_o[0] = proj_t(2 * d_att, d_att)
    else:
        aq_o[...] = proj(a0, d_att)
        ak_o[...] = proj(a0 + d_att, d_att)
        av_o[...] = proj(a0 + 2 * d_att, d_att)


def _in_proj(x, gpre, w_bf, llb, *, d_lin, d_att, seq_len=None, tm=256):
    rows, d_model = x.shape
    tm = min(tm, rows)
    assert rows % tm == 0
    d_in = w_bf.shape[1]
    feat_major = seq_len is not None
    row_spec = lambda w: pl.BlockSpec((tm, w), lambda i: (i, 0))
    in_specs = [
        pl.BlockSpec((tm, d_model), lambda i: (i, 0)),
        pl.BlockSpec((1, d_model), lambda i: (0, 0)),
        pl.BlockSpec((d_model, d_in), lambda i: (0, 0), pipeline_mode=pl.Buffered(1)),
        pl.BlockSpec(llb.shape, lambda i: (0, 0)),
    ]
    args = [x, gpre, w_bf, llb]
    out_specs = [row_spec(d_lin)] * 5 + [row_spec(d_att)]
    out_shape = [jax.ShapeDtypeStruct((rows, d_lin), F32)] * 5 + [jax.ShapeDtypeStruct((rows, d_att), F32)]
    if feat_major:
        assert seq_len % tm == 0
        tiles = seq_len // tm
        a0 = 4 * d_lin
        wt_bf = w_bf[:, a0:a0 + 3 * d_att].T
        in_specs.append(pl.BlockSpec(wt_bf.shape, lambda i: (0, 0), pipeline_mode=pl.Buffered(1)))
        args.append(wt_bf)
        t_spec = pl.BlockSpec((1, d_att, tm), lambda i: (i // tiles, 0, i % tiles))
        t_shape = jax.ShapeDtypeStruct((rows // seq_len, d_att, seq_len), F32)
        out_specs += [row_spec(d_att)] + [t_spec] * 3
        out_shape += [jax.ShapeDtypeStruct((rows, d_att), F32)] + [t_shape] * 3
    else:
        out_specs += [row_spec(d_att)] * 3
        out_shape += [jax.ShapeDtypeStruct((rows, d_att), F32)] * 3
    return pl.pallas_call(
        functools.partial(_in_proj_body, d_lin=d_lin, d_att=d_att, feat_major=feat_major),
        grid=(rows // tm,),
        in_specs=in_specs,
        out_specs=out_specs,
        out_shape=out_shape,
        compiler_params=pltpu.CompilerParams(
            dimension_semantics=("arbitrary",), vmem_limit_bytes=VMEM_LIMIT),
        name="in_proj",
    )(*args)


def _hgrn_step_body(q_ref, g_ref, kk_ref, v_ref, gl_ref, s0_ref, glin_ref, o_ref, sout_ref, b_scr,
                    *, rows, valid, n_seq):
    row = lax.broadcasted_iota(jnp.int32, (rows, 1), 0)
    for s in range(n_seq):
        acc = jnp.zeros((1, LIN_HEAD_DIM), F32)
        for r in range(rows):
            if r < valid:
                acc = acc + g_ref[s, r:r + 1, :]
            b_scr[s, r:r + 1, :] = acc
    seqs = range(n_seq)
    b = [b_scr[s] for s in seqs]
    b_last = [b_scr[s, rows - 1:rows, :] for s in seqs]
    q = [q_ref[s] for s in seqs]
    kk = [jnp.where(row < valid, kk_ref[s], 0.0) for s in seqs]

    o = [jnp.dot((q[s] * jnp.exp(b[s])).astype(BF16), s0_ref[s, 0].astype(BF16), preferred_element_type=F32)
         for s in seqs]
    for s in seqs:
        for j in range(valid):
            w = q[s] * kk_ref[s, j:j + 1, :] * jnp.exp(jnp.minimum(b[s] - b_scr[s, j:j + 1, :], 0.0))
            a = jnp.sum(w, axis=-1, keepdims=True)
            o[s] = o[s] + jnp.where(row >= j, a, 0.0) * v_ref[s, j:j + 1, :]
    u = [lax.dot_general((kk[s] * jnp.exp(b_last[s] - b[s])).astype(BF16), v_ref[s].astype(BF16), TN_DIMS,
                         preferred_element_type=F32) for s in seqs]
    ones = jnp.ones((rows, LIN_HEAD_DIM), BF16)
    decay = []
    for s in seqs:
        rest = jnp.exp(b_last[s])
        pieces = jnp.zeros((rows, LIN_HEAD_DIM), F32)
        for pi in range(3):
            piece = rest.astype(BF16).astype(F32)
            rest = rest - piece
            pieces = jnp.where(row == pi, piece, pieces)
        decay.append(lax.dot_general(pieces.astype(BF16), ones, TN_DIMS, preferred_element_type=F32))
    for s in seqs:
        sout_ref[s, 0] = s0_ref[s, 0] * decay[s] + u[s]
        o_ref[s] = _rms(o[s], glin_ref[...]) * gl_ref[s]


def _hgrn_step(q, g, kk, v, gl, s0, glin, *, valid, n_seq):
    nb, rows, d_lin = q.shape
    heads = d_lin // LIN_HEAD_DIM
    assert rows >= 3 and nb % n_seq == 0
    seq_spec = pl.BlockSpec((n_seq, rows, LIN_HEAD_DIM), lambda b, h: (b, 0, h))
    st_spec = pl.BlockSpec((n_seq, 1, LIN_HEAD_DIM, LIN_HEAD_DIM), lambda b, h: (b, h, 0, 0))
    return pl.pallas_call(
        functools.partial(_hgrn_step_body, rows=rows, valid=valid, n_seq=n_seq),
        grid=(nb // n_seq, heads),
        in_specs=[seq_spec] * 5 + [st_spec, pl.BlockSpec((1, LIN_HEAD_DIM), lambda b, h: (0, 0))],
        out_specs=[seq_spec, st_spec],
        out_shape=[jax.ShapeDtypeStruct((nb, rows, d_lin), F32),
                   jax.ShapeDtypeStruct(s0.shape, F32)],
        scratch_shapes=[pltpu.VMEM((n_seq, rows, LIN_HEAD_DIM), F32)],
        compiler_params=pltpu.CompilerParams(dimension_semantics=("arbitrary", "arbitrary")),
        name="hgrn_step",
    )(q, g, kk, v, gl, s0, glin)


SAFE_DECAY = 60.0


def _hgrn_tile_body(q_ref, g_ref, kk_ref, v_ref, gl_ref, s0_ref, glin_ref, o_ref, sout_ref,
                    st_scr, b_scr, oi_scr, u_scr, kkp_scr, bp_scr, vp_scr, *, chunk, t_tile, heads):
    it = pl.program_id(1)
    hs = [slice(h * LIN_HEAD_DIM, (h + 1) * LIN_HEAD_DIM) for h in range(heads)]

    @pl.when(it == 0)
    def _():
        for h in range(heads):
            st_scr[h] = s0_ref[0, h].T

    lg = int(math.log2(chunk))
    n_groups = t_tile // LANES
    n_chunks = t_tile // chunk
    r = lax.broadcasted_iota(jnp.int32, (LANES, LANES), 0)
    c = lax.broadcasted_iota(jnp.int32, (LANES, LANES), 1)
    causal = ((r >> lg) == (c >> lg)) & (c <= r)
    tri = jnp.where(causal, 1.0, 0.0).astype(BF16)
    n_pieces = 3
    for h in range(heads):
        pieces = []
        for gi in range(n_groups):
            rest = g_ref[0, gi * LANES:(gi + 1) * LANES, hs[h]]
            for _ in range(n_pieces):
                piece = rest.astype(BF16)
                rest = rest - piece.astype(F32)
                pieces.append(piece)
        sums = jnp.dot(tri, jnp.concatenate(pieces, axis=1), preferred_element_type=F32)
        for gi in range(n_groups):
            b = jnp.zeros((LANES, LIN_HEAD_DIM), F32)
            for pi in range(n_pieces):
                col = (gi * n_pieces + pi) * LIN_HEAD_DIM
                b = b + sums[:, col:col + LIN_HEAD_DIM]
            b_scr[h, gi * LANES:(gi + 1) * LANES, :] = b

    def intra_mxu():
        atts = {}
        for h in range(heads):
            for gi in range(n_groups):
                rows = slice(gi * LANES, (gi + 1) * LANES)
                b = b_scr[h, rows, :]
                qe = (q_ref[0, rows, hs[h]] * jnp.exp(b)).astype(BF16)
                ke = (kk_ref[0, rows, hs[h]] * jnp.exp(-b)).astype(BF16)
                att = lax.dot_general(qe, ke, NT_DIMS, preferred_element_type=F32)
                atts[h, gi] = jnp.where(causal, att, 0.0).astype(BF16)
        for h in range(heads):
            for gi in range(n_groups):
                rows = slice(gi * LANES, (gi + 1) * LANES)
                oi_scr[h, rows, :] = jnp.dot(atts[h, gi], v_ref[0, rows, hs[h]].astype(BF16),
                                             preferred_element_type=F32)

    def intra_pairs():
        zeros = jnp.zeros((chunk, LIN_HEAD_DIM), F32)
        tmod = lax.broadcasted_iota(jnp.int32, (t_tile, 1), 0) & (chunk - 1)
        for h in range(heads):
            for dst, src in ((kkp_scr, kk_ref[0, :, hs[h]]), (bp_scr, b_scr[h]), (vp_scr, v_ref[0, :, hs[h]])):
                dst[0:chunk, :] = zeros
                dst[chunk:, :] = src
            oi_scr[h] = jnp.zeros((t_tile, LIN_HEAD_DIM), F32)
            q = q_ref[0, :, hs[h]]
            b = b_scr[h]

            def lag(d, carry, h=h, q=q, b=b):
                start = chunk - d
                w = q * kkp_scr[pl.ds(start, t_tile), :] * jnp.exp(
                    jnp.minimum(b - bp_scr[pl.ds(start, t_tile), :], 0.0))
                a = jnp.sum(w, axis=-1, keepdims=True)
                oi_scr[h] += jnp.where(tmod >= d, a, 0.0) * vp_scr[pl.ds(start, t_tile), :]
                return carry

            lax.fori_loop(0, chunk, lag, 0)

    lax.cond(jnp.min(b_scr[...]) >= -SAFE_DECAY, intra_mxu, intra_pairs)

    for h in range(heads):
        for n in range(n_chunks):
            rows = slice(n * chunk, (n + 1) * chunk)
            b_last = b_scr[h, (n + 1) * chunk - 1:(n + 1) * chunk, :]
            kt = (kk_ref[0, rows, hs[h]] * jnp.exp(b_last - b_scr[h, rows, :])).astype(BF16)
            u_scr[h, n] = lax.dot_general(v_ref[0, rows, hs[h]].astype(BF16), kt, TN_DIMS,
                                          preferred_element_type=F32)

    st = [st_scr[h] for h in range(heads)]
    for n in range(n_chunks):
        rows = slice(n * chunk, (n + 1) * chunk)
        for h in range(heads):
            b_last = b_scr[h, (n + 1) * chunk - 1:(n + 1) * chunk, :]
            o = lax.dot_general((q_ref[0, rows, hs[h]] * jnp.exp(b_scr[h, rows, :])).astype(BF16),
                                st[h].astype(BF16), NT_DIMS, preferred_element_type=F32) + oi_scr[h, rows, :]
            st[h] = st[h] * jnp.exp(b_last) + u_scr[h, n]
            o_ref[0, rows, hs[h]] = (_rms(o, glin_ref[...]) * gl_ref[0, rows, hs[h]]).astype(o_ref.dtype)
    for h in range(heads):
        st_scr[h] = st[h]

    @pl.when(it == pl.num_programs(1) - 1)
    def _():
        for h in range(heads):
            sout_ref[0, h] = st[h].T


def _hgrn_tiles(q, g, kk, v, gl, s0, glin, *, chunk, t_tile):
    nb, t, d_lin = q.shape
    heads = d_lin // LIN_HEAD_DIM
    assert LANES % chunk == 0 and t_tile % LANES == 0 and t % t_tile == 0
    seq_spec = pl.BlockSpec((1, t_tile, d_lin), lambda b, i: (b, i, 0))
    st_spec = pl.BlockSpec((1, heads, LIN_HEAD_DIM, LIN_HEAD_DIM), lambda b, i: (b, 0, 0, 0))
    tile = pltpu.VMEM((heads, t_tile, LIN_HEAD_DIM), F32)
    halo = pltpu.VMEM((chunk + t_tile, LIN_HEAD_DIM), F32)
    return pl.pallas_call(
        functools.partial(_hgrn_tile_body, chunk=chunk, t_tile=t_tile, heads=heads),
        grid=(nb, t // t_tile),
        in_specs=[seq_spec] * 5 + [st_spec, pl.BlockSpec((1, LIN_HEAD_DIM), lambda b, i: (0, 0))],
        out_specs=[seq_spec, st_spec],
        out_shape=[jax.ShapeDtypeStruct((nb, t, d_lin), BF16),
                   jax.ShapeDtypeStruct(s0.shape, F32)],
        scratch_shapes=[pltpu.VMEM((heads, LIN_HEAD_DIM, LIN_HEAD_DIM), F32), tile, tile,
                        pltpu.VMEM((heads, t_tile // chunk, LIN_HEAD_DIM, LIN_HEAD_DIM), F32),
                        halo, halo, halo],
        compiler_params=pltpu.CompilerParams(
            dimension_semantics=("arbitrary", "arbitrary"), vmem_limit_bytes=VMEM_LIMIT),
        name="hgrn_tiles",
    )(q, g, kk, v, gl, s0, glin)


def _t5_first_distances():
    max_exact = N_BUCKETS // 2
    n = np.arange(0, MAX_DISTANCE + 1, dtype=np.int32)
    ratio = np.maximum(n, 1).astype(np.float32) / np.float32(max_exact)
    large = max_exact + (np.log(ratio) / np.float32(math.log(MAX_DISTANCE / max_exact))
                         * np.float32(N_BUCKETS - max_exact)).astype(np.int32)
    bucket = np.where(n < max_exact, n, np.minimum(large, N_BUCKETS - 1))
    assert np.all(np.diff(bucket) >= 0) and bucket[-1] == N_BUCKETS - 1
    return [int(np.argmax(bucket >= bk)) for bk in range(N_BUCKETS)]


_T5_FIRST_DISTANCE = _t5_first_distances()


def _t5_bias(dist, rb_ref, head):
    n = jnp.maximum(dist, 0)
    out = jnp.full(dist.shape, rb_ref[0, head], F32)
    for bk in range(1, N_BUCKETS):
        out = jnp.where(n >= _T5_FIRST_DISTANCE[bk], rb_ref[bk, head], out)
    return out


def _prompt_bias_body(rb_ref, o_ref, *, n_tiles):
    h = pl.program_id(0)
    x = lax.broadcasted_iota(jnp.int32, (SUBLANES, 2 * MOBA_BLOCK), 1)
    for d in range(n_tiles):
        dist = (d - 1) * MOBA_BLOCK + x
        by_dist = jnp.where(dist >= 0, _t5_bias(dist, rb_ref, h) * LOG2E, NEG)
        wide = jnp.broadcast_to(by_dist[0:1, :], (MOBA_BLOCK, 2 * MOBA_BLOCK))
        o_ref[0, d] = pltpu.roll(wide, 0, 1, stride=1, stride_axis=0)[:, MOBA_BLOCK:]


def _prompt_bias(rel_bias, n_tiles):
    heads = rel_bias.shape[1]
    return pl.pallas_call(
        functools.partial(_prompt_bias_body, n_tiles=n_tiles),
        grid=(heads,),
        in_specs=[pl.BlockSpec(memory_space=pltpu.SMEM)],
        out_specs=pl.BlockSpec((1, n_tiles, MOBA_BLOCK, MOBA_BLOCK), lambda h: (h, 0, 0, 0)),
        out_shape=jax.ShapeDtypeStruct((heads, n_tiles, MOBA_BLOCK, MOBA_BLOCK), F32),
        name="prompt_bias",
    )(rel_bias)


def _sample_bias_body(rb_ref, o_ref, *, past_len, n_valid, rows):
    h = pl.program_id(0)
    width = past_len + LANES
    r = lax.broadcasted_iota(jnp.int32, (rows, width), 0)
    kpos = lax.broadcasted_iota(jnp.int32, (rows, width), 1)
    dist = past_len + r - kpos
    ok = (dist >= 0) & (kpos < past_len + n_valid)
    o_ref[...] = jnp.where(ok, _t5_bias(dist, rb_ref, h), NEG)


def _sample_bias(rel_bias, *, past_len, n_valid, rows):
    heads = rel_bias.shape[1]
    width = past_len + LANES
    return pl.pallas_call(
        functools.partial(_sample_bias_body, past_len=past_len, n_valid=n_valid, rows=rows),
        grid=(heads,),
        in_specs=[pl.BlockSpec(memory_space=pltpu.SMEM)],
        out_specs=pl.BlockSpec((rows, width), lambda h: (h, 0)),
        out_shape=jax.ShapeDtypeStruct((heads * rows, width), F32),
        name="sample_bias",
    )(rel_bias)


def _topk_keep(gate, idx, own, axis, n):
    past = idx < own
    gm = jnp.where(past, gate, -jnp.inf)
    rank = jnp.zeros(gate.shape, jnp.int32)
    for jp in range(n):
        gj = lax.slice_in_dim(gm, jp, jp + 1, axis=axis)
        beats = (gj > gm) | ((gj == gm) & (jp < idx))
        rank = rank + beats.astype(jnp.int32)
    keep = ((rank < MOBA_TOPK) & past) | (idx == own)
    return jnp.where(keep, 0.0, NEG)


def _topk_keep_rows(gate, idx, own):
    past = idx < own
    gm = jnp.where(past, gate, -jnp.inf)
    idx_f = idx.astype(F32)
    chosen = jnp.zeros(gate.shape, jnp.bool_)
    for _ in range(MOBA_TOPK):
        top = jnp.max(gm, axis=0, keepdims=True)
        first = jnp.min(jnp.where(gm == top, idx_f, float(gate.shape[0])), axis=0, keepdims=True)
        pick = idx_f == first
        chosen = chosen | pick
        gm = jnp.where(pick, -jnp.inf, gm)
    keep = (chosen & past) | (idx == own)
    return jnp.where(keep, 0.0, NEG)


def _moba_gate_body(qt_ref, k_ref, a_ref, km_scr, *, n_blocks):
    t = n_blocks * MOBA_BLOCK
    qt = qt_ref[0]
    lane = lax.broadcasted_iota(jnp.int32, (1, LANES), 1)
    for n in range(n_blocks):
        km_scr[pl.ds(n, 1), :] = jnp.sum(
            k_ref[0, n * MOBA_BLOCK:(n + 1) * MOBA_BLOCK, :], axis=0, keepdims=True) * (1.0 / MOBA_BLOCK)
    km = km_scr[...]
    blk = lax.broadcasted_iota(jnp.int32, (n_blocks, t), 0)
    own = lax.broadcasted_iota(jnp.int32, (n_blocks, t), 1) >> int(math.log2(MOBA_BLOCK))
    for hh in range(HEADS_PER_TILE):
        kmh = jnp.where(_in_head(lane, hh), km, 0.0)
        gate_t = jnp.dot(kmh, qt, precision=lax.Precision.HIGHEST, preferred_element_type=F32)
        a_ref[0, 0, hh * n_blocks:(hh + 1) * n_blocks, :] = _topk_keep_rows(gate_t, blk, own)


def _moba_gate(aqt, ak):
    b, d_att, t = aqt.shape
    tiles = d_att // LANES
    n_blocks = t // MOBA_BLOCK
    return pl.pallas_call(
        functools.partial(_moba_gate_body, n_blocks=n_blocks),
        grid=(b, tiles),
        in_specs=[pl.BlockSpec((1, LANES, t), lambda i, p: (i, p, 0)),
                  pl.BlockSpec((1, t, LANES), lambda i, p: (i, 0, p))],
        out_specs=pl.BlockSpec((1, 1, HEADS_PER_TILE * n_blocks, t), lambda i, p: (i, p, 0, 0)),
        out_shape=jax.ShapeDtypeStruct((b, tiles, HEADS_PER_TILE * n_blocks, t), F32),
        scratch_shapes=[pltpu.VMEM((n_blocks, LANES), F32)],
        compiler_params=pltpu.CompilerParams(
            dimension_semantics=("arbitrary", "arbitrary"), vmem_limit_bytes=VMEM_LIMIT),
        name="moba_gate",
    )(aqt, ak)


def _moba_prompt_body(qt_ref, k_ref, vt_ref, keep_ref, bias_ref, ga_ref, o_ref, kaug_scr, vtaug_scr, s_scr,
                      *, n_bias, n_blocks, group, tps):
    i = pl.program_id(2)
    rows = group * MOBA_BLOCK
    t = n_blocks * MOBA_BLOCK
    lg_block = int(math.log2(MOBA_BLOCK))
    feat = lax.broadcasted_iota(jnp.int32, (LANES, 1), 0)
    heads = [(tt, hh) for tt in range(tps) for hh in range(HEADS_PER_TILE)]
    tile = [slice(tt * LANES, (tt + 1) * LANES) for tt in range(tps)]
    ones_row = [((hh + 1) % HEADS_PER_TILE) * ATT_HEAD_DIM for hh in range(HEADS_PER_TILE)]

    @pl.when(i == 0)
    def _():
        blk = lax.broadcasted_iota(jnp.int32, (rows, LANES), 0) >> lg_block
        lane = lax.broadcasted_iota(jnp.int32, (rows, LANES), 1)
        for c in range(t // rows):
            sl = slice(c * rows, (c + 1) * rows)
            for hd, (tt, hh) in enumerate(heads):
                onehot = jnp.where(lane == hh * n_blocks + c * group + blk, 1.0, 0.0)
                kaug_scr[hd, sl, :] = jnp.concatenate([k_ref[0, sl, tile[tt]], onehot], axis=1).astype(BF16)
                vtaug_scr[hd, :, sl] = jnp.where(feat == ones_row[hh], 1.0, vt_ref[0, tile[tt], sl]).astype(BF16)

    q_aug = []
    for tt, hh in heads:
        qt = qt_ref[0, tile[tt], :] * (ATT_HEAD_DIM ** -0.5 * LOG2E)
        keep = keep_ref[0, tt]
        pad = jnp.zeros((LANES - keep.shape[0], MOBA_BLOCK), F32)
        q_aug.append(jnp.concatenate([jnp.where(_in_head(feat, hh), qt, 0.0), keep, pad], axis=0).astype(BF16))
    n_groups = (i >> int(math.log2(group))) + 1
    n_heads = len(heads)

    def scores(gi, m):
        r0 = pl.multiple_of(gi * rows, rows)
        m_new = []
        for hd in range(n_heads):
            s = jnp.dot(kaug_scr[hd, pl.ds(r0, rows), :], q_aug[hd], preferred_element_type=F32)
            mh = m[hd]
            for u in range(group):
                d = jnp.clip(i - (gi * group + u), 0, n_bias - 1)
                su = s[u * MOBA_BLOCK:(u + 1) * MOBA_BLOCK] + bias_ref[hd, d]
                s_scr[hd, pl.ds(r0 + u * MOBA_BLOCK, MOBA_BLOCK), :] = su
                mh = jnp.maximum(mh, jnp.max(su, axis=0, keepdims=True))
            m_new.append(mh)
        return tuple(m_new)

    def values(gi, m, m_before, acc):
        r0 = pl.multiple_of(gi * rows, rows)
        out = []
        for hd in range(n_heads):
            p = jnp.exp2(s_scr[hd, pl.ds(r0, rows), :] - m[hd]).astype(BF16)
            out.append(acc[hd] * jnp.exp2(m_before[hd] - m[hd])
                       + jnp.dot(vtaug_scr[hd, :, pl.ds(r0, rows)], p, preferred_element_type=F32))
        return tuple(out)

    m_init = tuple(jnp.full((1, MOBA_BLOCK), M_INIT, F32) for _ in range(n_heads))
    acc0 = tuple(jnp.zeros((LANES, MOBA_BLOCK), F32) for _ in range(n_heads))

    def stage(gi, carry):
        m, m_before, acc = carry
        acc = values(gi - 1, m, m_before, acc)
        return scores(gi, m), m, acc

    m, m_before, acc = lax.fori_loop(1, n_groups, stage, (scores(0, m_init), m_init, acc0))
    acc = values(n_groups - 1, m, m_before, acc)
    for tt in range(tps):
        out = jnp.zeros((LANES, MOBA_BLOCK), F32)
        for hh in range(HEADS_PER_TILE):
            a = acc[tt * HEADS_PER_TILE + hh]
            out = jnp.where(_in_head(feat, hh), a / a[ones_row[hh]:ones_row[hh] + 1, :], out)
        o_ref[0, :, tile[tt]] = (out.T * ga_ref[0, :, tile[tt]]).astype(o_ref.dtype)


def _moba_prompt(aqt, ak, avt, keep, bias, ga, *, group=4, tps=2):
    b, d_att, t = aqt.shape
    tiles = d_att // LANES
    n_blocks = t // MOBA_BLOCK
    n_bias = bias.shape[1]
    group = min(group, n_blocks)
    assert n_blocks % group == 0 and group & (group - 1) == 0 and tiles % tps == 0
    width = tps * LANES
    n_heads = tps * HEADS_PER_TILE
    once = dict(pipeline_mode=pl.Buffered(1))
    return pl.pallas_call(
        functools.partial(_moba_prompt_body, n_bias=n_bias, n_blocks=n_blocks, group=group, tps=tps),
        grid=(b, tiles // tps, n_blocks),
        in_specs=[pl.BlockSpec((1, width, MOBA_BLOCK), lambda ib, p, i: (ib, p, i)),
                  pl.BlockSpec((1, t, width), lambda ib, p, i: (ib, 0, p), **once),
                  pl.BlockSpec((1, width, t), lambda ib, p, i: (ib, p, 0), **once),
                  pl.BlockSpec((1, tps, keep.shape[2], MOBA_BLOCK), lambda ib, p, i: (ib, p, 0, i)),
                  pl.BlockSpec((n_heads, n_bias, MOBA_BLOCK, MOBA_BLOCK), lambda ib, p, i: (p, 0, 0, 0), **once),
                  pl.BlockSpec((1, MOBA_BLOCK, width), lambda ib, p, i: (ib, i, p))],
        out_specs=pl.BlockSpec((1, MOBA_BLOCK, width), lambda ib, p, i: (ib, i, p)),
        out_shape=jax.ShapeDtypeStruct((b, t, d_att), BF16),
        scratch_shapes=[pltpu.VMEM((n_heads, t, 2 * LANES), BF16),
                        pltpu.VMEM((n_heads, LANES, t), BF16),
                        pltpu.VMEM((n_heads, t, MOBA_BLOCK), F32)],
        compiler_params=pltpu.CompilerParams(
            dimension_semantics=("arbitrary", "arbitrary", "arbitrary"), vmem_limit_bytes=VMEM_LIMIT),
        name="moba_prompt",
    )(aqt, ak, avt, keep, bias, ga.reshape(b, t, d_att))


def _moba_sample_body(pt_ref, q_ref, kn_ref, vn_ref, bias_ref, ck_hbm, cv_hbm, o_ref,
                      kbuf, vbuf, sems, s_scr, *, n_pages, heads, rows):
    b = pl.program_id(0)
    n_slots = kbuf.shape[0]
    ahead = n_slots - 1
    slot = lax.rem(b, n_slots)

    def page_copies(seq, sl):
        out = []
        for p in range(n_pages):
            phys = pt_ref[seq, p]
            out.append(pltpu.make_async_copy(ck_hbm.at[phys], kbuf.at[sl, p], sems.at[0, sl]))
            out.append(pltpu.make_async_copy(cv_hbm.at[phys], vbuf.at[sl, p], sems.at[1, sl]))
        return out

    @pl.when(b == 0)
    def _():
        for seq in range(ahead):
            for cp in page_copies(seq, seq):
                cp.start()

    @pl.when(b + ahead < pl.num_programs(0))
    def _():
        for cp in page_copies(b + ahead, lax.rem(b + ahead, n_slots)):
            cp.start()

    for cp in page_copies(b, slot):
        cp.wait()
    k_pages = [kbuf.at[slot, p] for p in range(n_pages)]
    v_pages = [vbuf.at[slot, p] for p in range(n_pages)]
    d_att = heads * ATT_HEAD_DIM
    page = kbuf.shape[-1]
    pages_per_block = MOBA_BLOCK // page
    n_blocks = n_pages // pages_per_block
    n_q = heads * rows
    feat = lax.broadcasted_iota(jnp.int32, (1, d_att), 1)
    lane = lax.broadcasted_iota(jnp.int32, (n_q, LANES), 1)

    q = q_ref[0] * (ATT_HEAD_DIM ** -0.5)
    q_bd = jnp.concatenate([jnp.where(_in_head(feat, h), q, 0.0) for h in range(heads)], axis=0).astype(BF16)

    gate = jnp.zeros((n_q, LANES), F32)
    for n in range(n_blocks):
        tot = jnp.zeros((n_q, page), F32)
        for p in range(n * pages_per_block, (n + 1) * pages_per_block):
            kt = k_pages[p][...].astype(BF16)
            s = jnp.dot(q_bd, kt, preferred_element_type=F32)
            s_scr[:, p * page:(p + 1) * page] = s
            tot = tot + s
        gate = jnp.where(lane == n, jnp.sum(tot, axis=-1, keepdims=True) * (1.0 / MOBA_BLOCK), gate)
    keep = _topk_keep(gate, lane, jnp.full(gate.shape, n_blocks, jnp.int32), axis=1, n=n_blocks)

    def new_rows(ref):
        return jnp.concatenate([ref[0], jnp.zeros((LANES - rows, d_att), F32)], axis=0).astype(BF16)

    own = n_pages * page
    s_own = lax.dot_general(q_bd, new_rows(kn_ref), NT_DIMS, preferred_element_type=F32)
    s_own = s_own + bias_ref[:, own:own + LANES]
    s_scr[:, own:own + LANES] = s_own
    m_wide = s_own
    for p in range(n_pages):
        n = p // pages_per_block
        s = s_scr[:, p * page:(p + 1) * page] + bias_ref[:, p * page:(p + 1) * page] + keep[:, n:n + 1]
        s_scr[:, p * page:(p + 1) * page] = s
        m_wide = jnp.maximum(m_wide, s)
    m = jnp.max(m_wide, axis=-1, keepdims=True)

    pr = jnp.exp(s_scr[:, own:own + LANES] - m)
    l_wide = pr
    acc = jnp.dot(pr.astype(BF16), new_rows(vn_ref), preferred_element_type=F32)
    for p in range(n_pages):
        pr = jnp.exp(s_scr[:, p * page:(p + 1) * page] - m)
        l_wide = l_wide + pr
        vt = v_pages[p][...].astype(BF16)
        acc = acc + lax.dot_general(pr.astype(BF16), vt, NT_DIMS, preferred_element_type=F32)
    acc = acc / jnp.sum(l_wide, axis=-1, keepdims=True)
    out = jnp.zeros((rows, d_att), F32)
    for h in range(heads):
        out = jnp.where(_in_head(feat, h), acc[h * rows:(h + 1) * rows, :], out)
    o_ref[0] = out


def _moba_sample(aq, ak, av, cache_kt, cache_vt, page_table, bias, *, heads):
    nb, rows, d_att = aq.shape
    n_pages = page_table.shape[1]
    page = cache_kt.shape[-1]
    assert cache_kt.shape[1:] == (d_att, page) and MOBA_BLOCK % page == 0 and page == LANES
    tok = pl.BlockSpec((1, rows, d_att), lambda b, pt: (b, 0, 0))
    in_hbm = pl.BlockSpec(memory_space=pl.ANY)
    n_slots = 3
    assert nb >= n_slots
    page_slots = pltpu.VMEM((n_slots, n_pages, d_att, page), F32)
    grid_spec = pltpu.PrefetchScalarGridSpec(
        num_scalar_prefetch=1,
        grid=(nb,),
        in_specs=[tok, tok, tok, pl.BlockSpec(bias.shape, lambda b, pt: (0, 0)), in_hbm, in_hbm],
        out_specs=tok,
        scratch_shapes=[page_slots, page_slots, pltpu.SemaphoreType.DMA((2, n_slots)),
                        pltpu.VMEM(bias.shape, F32)],
    )
    return pl.pallas_call(
        functools.partial(_moba_sample_body, n_pages=n_pages, heads=heads, rows=rows),
        grid_spec=grid_spec,
        out_shape=jax.ShapeDtypeStruct((nb, rows, d_att), F32),
        compiler_params=pltpu.CompilerParams(
            dimension_semantics=("arbitrary",), vmem_limit_bytes=VMEM_LIMIT),
        name="moba_sample",
    )(page_table, aq, ak, av, bias, cache_kt, cache_vt)


def _out_proj_body(*refs, d_lin, gated):
    if gated:
        ol_ref, oa_ref, x_ref, w_ref, gpost_ref, y_ref = refs
        oa = oa_ref[...]
    else:
        ol_ref, oa_ref, ga_ref, x_ref, w_ref, gpost_ref, y_ref = refs
        oa = oa_ref[...] * ga_ref[...]
    o = jnp.dot(ol_ref[...].astype(BF16), w_ref[0:d_lin, :], preferred_element_type=F32)
    o = o + jnp.dot(oa.astype(BF16), w_ref[d_lin:, :], preferred_element_type=F32)
    y_ref[...] = x_ref[...] + _rms(o, gpost_ref[...])


def _out_proj(o_lin, o_att, ga, x, w_bf, gpost, *, tm=256):
    rows, d_model = x.shape
    tm = min(tm, rows)
    assert rows % tm == 0
    d_lin = o_lin.shape[1]
    row_spec = lambda w: pl.BlockSpec((tm, w), lambda i: (i, 0))
    acts = [o_lin, o_att] + ([] if ga is None else [ga])
    return pl.pallas_call(
        functools.partial(_out_proj_body, d_lin=d_lin, gated=ga is None),
        grid=(rows // tm,),
        in_specs=[row_spec(a.shape[1]) for a in acts] + [
            row_spec(d_model),
            pl.BlockSpec(w_bf.shape, lambda i: (0, 0)),
            pl.BlockSpec((1, d_model), lambda i: (0, 0)),
        ],
        out_specs=row_spec(d_model),
        out_shape=jax.ShapeDtypeStruct((rows, d_model), F32),
        compiler_params=pltpu.CompilerParams(dimension_semantics=("arbitrary",)),
        name="out_proj",
    )(*acts, x, w_bf, gpost)


def kernel(x_prompt, x_sample, cache_k, cache_v, state_hgrn, page_table, w_in, w_out,
           norm_pre, norm_post, norm_lin_out, lin_lower_bound, rel_bias):
    depth = w_in.shape[0]
    assert depth == 1 and lin_lower_bound.shape[0] == depth + 1
    b, t, d_model = x_prompt.shape
    nb, ts, _ = x_sample.shape
    d_lin = lin_lower_bound.shape[1]
    d_att = w_out.shape[1] - d_lin
    lin_heads = d_lin // LIN_HEAD_DIM
    att_heads = rel_bias.shape[1]
    n_pages = page_table.shape[1]
    page = cache_k.shape[2]
    past_len = n_pages * page
    assert d_att == att_heads * ATT_HEAD_DIM and t % MOBA_BLOCK == 0
    assert past_len % MOBA_BLOCK == 0 and ts <= SUBLANES

    w_in_bf = w_in[0].astype(BF16)
    w_out_bf = w_out[0].astype(BF16)
    gpre, gpost, glin = norm_pre, norm_post, norm_lin_out
    proj = functools.partial(_in_proj, gpre=gpre, w_bf=w_in_bf, llb=lin_lower_bound,
                             d_lin=d_lin, d_att=d_att)

    xp = x_prompt.reshape(b * t, d_model)
    q, g, kk, v, gl, ga, ak, aqt, akt, avt = proj(xp, seq_len=t, tm=512)
    q, g, kk, v, gl, ak = [a.reshape(b, t, -1) for a in (q, g, kk, v, gl, ak)]
    s0 = jnp.zeros((b, lin_heads, LIN_HEAD_DIM, LIN_HEAD_DIM), F32)
    o_lin, s_prompt = _hgrn_tiles(q, g, kk, v, gl, s0, glin, chunk=32, t_tile=math.gcd(t, 1024))
    keep = _moba_gate(aqt, ak)
    n_bias = 6
    assert (n_bias - 1) * MOBA_BLOCK - (MOBA_BLOCK - 1) >= MAX_DISTANCE
    bias = _prompt_bias(rel_bias, n_bias)
    o_att = _moba_prompt(aqt, ak, avt, keep, bias, ga)
    y_prompt = _out_proj(o_lin.reshape(b * t, d_lin), o_att.reshape(b * t, d_att),
                         None, xp, w_out_bf, gpost, tm=1024).reshape(b, t, d_model)

    rows = SUBLANES
    xs = jnp.pad(x_sample, ((0, 0), (0, rows - ts), (0, 0))).reshape(nb * rows, d_model)
    qs, gs, kks, vs, gls, gas, aqs, aks, avs = [a.reshape(nb, rows, -1) for a in proj(xs)]
    o_lin_s, s_sample = _hgrn_step(qs, gs, kks, vs, gls, state_hgrn[0], glin, valid=ts,
                                   n_seq=math.gcd(nb, 32))
    bias_s = _sample_bias(rel_bias, past_len=past_len, n_valid=ts, rows=rows)
    cache_kt = cache_k[0].transpose(0, 2, 3, 1).reshape(-1, d_att, page)
    cache_vt = cache_v[0].transpose(0, 2, 3, 1).reshape(-1, d_att, page)
    o_att_s = _moba_sample(aqs, aks, avs, cache_kt, cache_vt, page_table, bias_s, heads=att_heads)
    y_s = _out_proj(o_lin_s.reshape(nb * rows, d_lin), o_att_s.reshape(nb * rows, d_att),
                    gas.reshape(nb * rows, d_att), xs, w_out_bf, gpost)
    y_sample = y_s.reshape(nb, rows, d_model)[:, :ts]

    def prompt_kv(a):
        return a.reshape(1, b, att_heads, ATT_HEAD_DIM, t).transpose(0, 1, 4, 2, 3)

    kvs_shape = (1, nb, ts, att_heads, ATT_HEAD_DIM)
    return (y_prompt, y_sample, prompt_kv(akt), prompt_kv(avt), s_prompt[None],
            aks[:, :ts].reshape(kvs_shape), avs[:, :ts].reshape(kvs_shape), s_sample[None])
```

```python
import functools
import math

import jax
import jax.numpy as jnp
import numpy as np
from jax import lax
from jax.experimental import pallas as pl
from jax.experimental.pallas import tpu as pltpu

F32 = jnp.float32
BF16 = jnp.bfloat16

EPS = 1e-6
LIN_HEAD_DIM = 128
ATT_HEAD_DIM = 64
MOBA_BLOCK = 256
MOBA_TOPK = 3
N_BUCKETS = 32
MAX_DISTANCE = 1024
NEG = -2e30
M_INIT = -1e30
LOG2E = 1.4426950408889634
LANES = 128
SUBLANES = 8
VMEM_LIMIT = 56 * 1024 * 1024
HEADS_PER_TILE = LANES // ATT_HEAD_DIM

NT_DIMS = (((1,), (1,)), ((), ()))
TN_DIMS = (((0,), (0,)), ((), ()))


def _sigmoid(x):
    return 1.0 / (1.0 + jnp.exp(-x))


def _silu(x):
    return x * _sigmoid(x)


def _rms(x, g):
    return x * lax.rsqrt(jnp.mean(x * x, axis=-1, keepdims=True) + EPS) * g


def _in_head(idx, head):
    return (idx >= head * ATT_HEAD_DIM) & (idx < (head + 1) * ATT_HEAD_DIM)


def _in_proj_body(x_ref, gpre_ref, w_ref, llb_ref, *rest, d_lin, d_att, feat_major):
    if feat_major:
        wt_ref, q_o, g_o, kk_o, v_o, gl_o, ga_o, ak_o, aqt_o, akt_o, avt_o = rest
    else:
        q_o, g_o, kk_o, v_o, gl_o, ga_o, aq_o, ak_o, av_o = rest
    h = _rms(x_ref[...], gpre_ref[...]).astype(BF16)

    def proj(col, width):
        return jnp.dot(h, w_ref[:, col:col + width], preferred_element_type=F32)

    def proj_t(row, width):
        return lax.dot_general(wt_ref[row:row + width, :], h, NT_DIMS, preferred_element_type=F32)

    llb = llb_ref[...]
    e = jnp.exp(llb - jnp.max(llb, axis=0, keepdims=True))
    lb = e[0:1, :] / jnp.sum(e, axis=0, keepdims=True)

    q_o[...] = _silu(proj(0, d_lin))
    f = lb + (1.0 - lb) * _sigmoid(proj(d_lin, d_lin))
    g_o[...] = jnp.log(f)
    kk_o[...] = 1.0 - f
    v_o[...] = proj(2 * d_lin, d_lin)
    gl_o[...] = _silu(proj(3 * d_lin, d_lin))
    a0 = 4 * d_lin
    ga_o[...] = _silu(proj(a0 + 3 * d_att, d_att))
    if feat_major:
        aqt_o[0] = proj_t(0, d_att)
        akt = proj_t(d_att, d_att)
        akt_o[0] = akt
        ak_o[...] = akt.T
        avt_o[0] = proj_t(2 * d_att, d_att)
    else:
        aq_o[...] = proj(a0, d_att)
        ak_o[...] = proj(a0 + d_att, d_att)
        av_o[...] = proj(a0 + 2 * d_att, d_att)


def _in_proj(x, gpre, w_bf, llb, *, d_lin, d_att, seq_len=None, tm=256):
    rows, d_model = x.shape
    tm = min(tm, rows)
    assert rows % tm == 0
    d_in = w_bf.shape[1]
    feat_major = seq_len is not None
    row_spec = lambda w: pl.BlockSpec((tm, w), lambda i: (i, 0))
    in_specs = [
        pl.BlockSpec((tm, d_model), lambda i: (i, 0)),
        pl.BlockSpec((1, d_model), lambda i: (0, 0)),
        pl.BlockSpec((d_model, d_in), lambda i: (0, 0), pipeline_mode=pl.Buffered(1)),
        pl.BlockSpec(llb.shape, lambda i: (0, 0)),
    ]
    args = [x, gpre, w_bf, llb]
    out_specs = [row_spec(d_lin)] * 5 + [row_spec(d_att)]
    out_shape = [jax.ShapeDtypeStruct((rows, d_lin), F32)] * 5 + [jax.ShapeDtypeStruct((rows, d_att), F32)]
    if feat_major:
        assert seq_len % tm == 0
        tiles = seq_len // tm
        a0 = 4 * d_lin
        wt_bf = w_bf[:, a0:a0 + 3 * d_att].T
        in_specs.append(pl.BlockSpec(wt_bf.shape, lambda i: (0, 0), pipeline_mode=pl.Buffered(1)))
        args.append(wt_bf)
        t_spec = pl.BlockSpec((1, d_att, tm), lambda i: (i // tiles, 0, i % tiles))
        t_shape = jax.ShapeDtypeStruct((rows // seq_len, d_att, seq_len), F32)
        out_specs += [row_spec(d_att)] + [t_spec] * 3
        out_shape += [jax.ShapeDtypeStruct((rows, d_att), F32)] + [t_shape] * 3
    else:
        out_specs += [row_spec(d_att)] * 3
        out_shape += [jax.ShapeDtypeStruct((rows, d_att), F32)] * 3
    return pl.pallas_call(
        functools.partial(_in_proj_body, d_lin=d_lin, d_att=d_att, feat_major=feat_major),
        grid=(rows // tm,),
        in_specs=in_specs,
        out_specs=out_specs,
        out_shape=out_shape,
        compiler_params=pltpu.CompilerParams(
            dimension_semantics=("arbitrary",), vmem_limit_bytes=VMEM_LIMIT),
        name="in_proj",
    )(*args)


def _hgrn_step_body(q_ref, g_ref, kk_ref, v_ref, gl_ref, s0_ref, glin_ref, o_ref, sout_ref, b_scr,
                    *, rows, valid, n_seq):
    row = lax.broadcasted_iota(jnp.int32, (rows, 1), 0)
    for s in range(n_seq):
        acc = jnp.zeros((1, LIN_HEAD_DIM), F32)
        for r in range(rows):
            if r < valid:
                acc = acc + g_ref[s, r:r + 1, :]
            b_scr[s, r:r + 1, :] = acc
    seqs = range(n_seq)
    b = [b_scr[s] for s in seqs]
    b_last = [b_scr[s, rows - 1:rows, :] for s in seqs]
    q = [q_ref[s] for s in seqs]
    kk = [jnp.where(row < valid, kk_ref[s], 0.0) for s in seqs]

    o = [jnp.dot((q[s] * jnp.exp(b[s])).astype(BF16), s0_ref[s, 0].astype(BF16), preferred_element_type=F32)
         for s in seqs]
    for s in seqs:
        for j in range(valid):
            w = q[s] * kk_ref[s, j:j + 1, :] * jnp.exp(jnp.minimum(b[s] - b_scr[s, j:j + 1, :], 0.0))
            a = jnp.sum(w, axis=-1, keepdims=True)
            o[s] = o[s] + jnp.where(row >= j, a, 0.0) * v_ref[s, j:j + 1, :]
    u = [lax.dot_general((kk[s] * jnp.exp(b_last[s] - b[s])).astype(BF16), v_ref[s].astype(BF16), TN_DIMS,
                         preferred_element_type=F32) for s in seqs]
    ones = jnp.ones((rows, LIN_HEAD_DIM), BF16)
    decay = []
    for s in seqs:
        rest = jnp.exp(b_last[s])
        pieces = jnp.zeros((rows, LIN_HEAD_DIM), F32)
        for pi in range(3):
            piece = rest.astype(BF16).astype(F32)
            rest = rest - piece
            pieces = jnp.where(row == pi, piece, pieces)
        decay.append(lax.dot_general(pieces.astype(BF16), ones, TN_DIMS, preferred_element_type=F32))
    for s in seqs:
        sout_ref[s, 0] = s0_ref[s, 0] * decay[s] + u[s]
        o_ref[s] = _rms(o[s], glin_ref[...]) * gl_ref[s]


def _hgrn_step(q, g, kk, v, gl, s0, glin, *, valid, n_seq):
    nb, rows, d_lin = q.shape
    heads = d_lin // LIN_HEAD_DIM
    assert rows >= 3 and nb % n_seq == 0
    seq_spec = pl.BlockSpec((n_seq, rows, LIN_HEAD_DIM), lambda b, h: (b, 0, h))
    st_spec = pl.BlockSpec((n_seq, 1, LIN_HEAD_DIM, LIN_HEAD_DIM), lambda b, h: (b, h, 0, 0))
    return pl.pallas_call(
        functools.partial(_hgrn_step_body, rows=rows, valid=valid, n_seq=n_seq),
        grid=(nb // n_seq, heads),
        in_specs=[seq_spec] * 5 + [st_spec, pl.BlockSpec((1, LIN_HEAD_DIM), lambda b, h: (0, 0))],
        out_specs=[seq_spec, st_spec],
        out_shape=[jax.ShapeDtypeStruct((nb, rows, d_lin), F32),
                   jax.ShapeDtypeStruct(s0.shape, F32)],
        scratch_shapes=[pltpu.VMEM((n_seq, rows, LIN_HEAD_DIM), F32)],
        compiler_params=pltpu.CompilerParams(dimension_semantics=("arbitrary", "arbitrary")),
        name="hgrn_step",
    )(q, g, kk, v, gl, s0, glin)


SAFE_DECAY = 60.0


def _hgrn_tile_body(q_ref, g_ref, kk_ref, v_ref, gl_ref, s0_ref, glin_ref, o_ref, sout_ref,
                    st_scr, b_scr, oi_scr, u_scr, kkp_scr, bp_scr, vp_scr, *, chunk, t_tile, heads):
    it = pl.program_id(1)
    hs = [slice(h * LIN_HEAD_DIM, (h + 1) * LIN_HEAD_DIM) for h in range(heads)]

    @pl.when(it == 0)
    def _():
        for h in range(heads):
            st_scr[h] = s0_ref[0, h].T

    lg = int(math.log2(chunk))
    n_groups = t_tile // LANES
    n_chunks = t_tile // chunk
    r = lax.broadcasted_iota(jnp.int32, (LANES, LANES), 0)
    c = lax.broadcasted_iota(jnp.int32, (LANES, LANES), 1)
    causal = ((r >> lg) == (c >> lg)) & (c <= r)
    tri = jnp.where(causal, 1.0, 0.0).astype(BF16)
    n_pieces = 3
    for h in range(heads):
        pieces = []
        for gi in range(n_groups):
            rest = g_ref[0, gi * LANES:(gi + 1) * LANES, hs[h]]
            for _ in range(n_pieces):
                piece = rest.astype(BF16)
                rest = rest - piece.astype(F32)
                pieces.append(piece)
        sums = jnp.dot(tri, jnp.concatenate(pieces, axis=1), preferred_element_type=F32)
        for gi in range(n_groups):
            b = jnp.zeros((LANES, LIN_HEAD_DIM), F32)
            for pi in range(n_pieces):
                col = (gi * n_pieces + pi) * LIN_HEAD_DIM
                b = b + sums[:, col:col + LIN_HEAD_DIM]
            b_scr[h, gi * LANES:(gi + 1) * LANES, :] = b

    def intra_mxu():
        atts = {}
        for h in range(heads):
            for gi in range(n_groups):
                rows = slice(gi * LANES, (gi + 1) * LANES)
                b = b_scr[h, rows, :]
                qe = (q_ref[0, rows, hs[h]] * jnp.exp(b)).astype(BF16)
                ke = (kk_ref[0, rows, hs[h]] * jnp.exp(-b)).astype(BF16)
                att = lax.dot_general(qe, ke, NT_DIMS, preferred_element_type=F32)
                atts[h, gi] = jnp.where(causal, att, 0.0).astype(BF16)
        for h in range(heads):
            for gi in range(n_groups):
                rows = slice(gi * LANES, (gi + 1) * LANES)
                oi_scr[h, rows, :] = jnp.dot(atts[h, gi], v_ref[0, rows, hs[h]].astype(BF16),
                                             preferred_element_type=F32)

    def intra_pairs():
        zeros = jnp.zeros((chunk, LIN_HEAD_DIM), F32)
        tmod = lax.broadcasted_iota(jnp.int32, (t_tile, 1), 0) & (chunk - 1)
        for h in range(heads):
            for dst, src in ((kkp_scr, kk_ref[0, :, hs[h]]), (bp_scr, b_scr[h]), (vp_scr, v_ref[0, :, hs[h]])):
                dst[0:chunk, :] = zeros
                dst[chunk:, :] = src
            oi_scr[h] = jnp.zeros((t_tile, LIN_HEAD_DIM), F32)
            q = q_ref[0, :, hs[h]]
            b = b_scr[h]

            def lag(d, carry, h=h, q=q, b=b):
                start = chunk - d
                w = q * kkp_scr[pl.ds(start, t_tile), :] * jnp.exp(
                    jnp.minimum(b - bp_scr[pl.ds(start, t_tile), :], 0.0))
                a = jnp.sum(w, axis=-1, keepdims=True)
                oi_scr[h] += jnp.where(tmod >= d, a, 0.0) * vp_scr[pl.ds(start, t_tile), :]
                return carry

            lax.fori_loop(0, chunk, lag, 0)

    lax.cond(jnp.min(b_scr[...]) >= -SAFE_DECAY, intra_mxu, intra_pairs)

    for h in range(heads):
        for n in range(n_chunks):
            rows = slice(n * chunk, (n + 1) * chunk)
            b_last = b_scr[h, (n + 1) * chunk - 1:(n + 1) * chunk, :]
            kt = (kk_ref[0, rows, hs[h]] * jnp.exp(b_last - b_scr[h, rows, :])).astype(BF16)
            u_scr[h, n] = lax.dot_general(v_ref[0, rows, hs[h]].astype(BF16), kt, TN_DIMS,
                                          preferred_element_type=F32)

    st = [st_scr[h] for h in range(heads)]
    for n in range(n_chunks):
        rows = slice(n * chunk, (n + 1) * chunk)
        for h in range(heads):
            b_last = b_scr[h, (n + 1) * chunk - 1:(n + 1) * chunk, :]
            o = lax.dot_general((q_ref[0, rows, hs[h]] * jnp.exp(b_scr[h, rows, :])).astype(BF16),
                                st[h].astype(BF16), NT_DIMS, preferred_element_type=F32) + oi_scr[h, rows, :]
            st[h] = st[h] * jnp.exp(b_last) + u_scr[h, n]
            o_ref[0, rows, hs[h]] = (_rms(o, glin_ref[...]) * gl_ref[0, rows, hs[h]]).astype(o_ref.dtype)
    for h in range(heads):
        st_scr[h] = st[h]

    @pl.when(it == pl.num_programs(1) - 1)
    def _():
        for h in range(heads):
            sout_ref[0, h] = st[h].T


def _hgrn_tiles(q, g, kk, v, gl, s0, glin, *, chunk, t_tile):
    nb, t, d_lin = q.shape
    heads = d_lin // LIN_HEAD_DIM
    assert LANES % chunk == 0 and t_tile % LANES == 0 and t % t_tile == 0
    seq_spec = pl.BlockSpec((1, t_tile, d_lin), lambda b, i: (b, i, 0))
    st_spec = pl.BlockSpec((1, heads, LIN_HEAD_DIM, LIN_HEAD_DIM), lambda b, i: (b, 0, 0, 0))
    tile = pltpu.VMEM((heads, t_tile, LIN_HEAD_DIM), F32)
    halo = pltpu.VMEM((chunk + t_tile, LIN_HEAD_DIM), F32)
    return pl.pallas_call(
        functools.partial(_hgrn_tile_body, chunk=chunk, t_tile=t_tile, heads=heads),
        grid=(nb, t // t_tile),
        in_specs=[seq_spec] * 5 + [st_spec, pl.BlockSpec((1, LIN_HEAD_DIM), lambda b, i: (0, 0))],
        out_specs=[seq_spec, st_spec],
        out_shape=[jax.ShapeDtypeStruct((nb, t, d_lin), BF16),
                   jax.ShapeDtypeStruct(s0.shape, F32)],
        scratch_shapes=[pltpu.VMEM((heads, LIN_HEAD_DIM, LIN_HEAD_DIM), F32), tile, tile,
                        pltpu.VMEM((heads, t_tile // chunk, LIN_HEAD_DIM, LIN_HEAD_DIM), F32),
                        halo, halo, halo],
        compiler_params=pltpu.CompilerParams(
            dimension_semantics=("arbitrary", "arbitrary"), vmem_limit_bytes=VMEM_LIMIT),
        name="hgrn_tiles",
    )(q, g, kk, v, gl, s0, glin)


def _t5_first_distances():
    max_exact = N_BUCKETS // 2
    n = np.arange(0, MAX_DISTANCE + 1, dtype=np.int32)
    ratio = np.maximum(n, 1).astype(np.float32) / np.float32(max_exact)
    large = max_exact + (np.log(ratio) / np.float32(math.log(MAX_DISTANCE / max_exact))
                         * np.float32(N_BUCKETS - max_exact)).astype(np.int32)
    bucket = np.where(n < max_exact, n, np.minimum(large, N_BUCKETS - 1))
    assert np.all(np.diff(bucket) >= 0) and bucket[-1] == N_BUCKETS - 1
    return [int(np.argmax(bucket >= bk)) for bk in range(N_BUCKETS)]


_T5_FIRST_DISTANCE = _t5_first_distances()


def _t5_bias(dist, rb_ref, head):
    n = jnp.maximum(dist, 0)
    out = jnp.full(dist.shape, rb_ref[0, head], F32)
    for bk in range(1, N_BUCKETS):
        out = jnp.where(n >= _T5_FIRST_DISTANCE[bk], rb_ref[bk, head], out)
    return out


def _prompt_bias_body(rb_ref, o_ref, *, n_tiles):
    h = pl.program_id(0)
    x = lax.broadcasted_iota(jnp.int32, (SUBLANES, 2 * MOBA_BLOCK), 1)
    for d in range(n_tiles):
        dist = (d - 1) * MOBA_BLOCK + x
        by_dist = jnp.where(dist >= 0, _t5_bias(dist, rb_ref, h) * LOG2E, NEG)
        wide = jnp.broadcast_to(by_dist[0:1, :], (MOBA_BLOCK, 2 * MOBA_BLOCK))
        o_ref[0, d] = pltpu.roll(wide, 0, 1, stride=1, stride_axis=0)[:, MOBA_BLOCK:]


def _prompt_bias(rel_bias, n_tiles):
    heads = rel_bias.shape[1]
    return pl.pallas_call(
        functools.partial(_prompt_bias_body, n_tiles=n_tiles),
        grid=(heads,),
        in_specs=[pl.BlockSpec(memory_space=pltpu.SMEM)],
        out_specs=pl.BlockSpec((1, n_tiles, MOBA_BLOCK, MOBA_BLOCK), lambda h: (h, 0, 0, 0)),
        out_shape=jax.ShapeDtypeStruct((heads, n_tiles, MOBA_BLOCK, MOBA_BLOCK), F32),
        name="prompt_bias",
    )(rel_bias)


def _sample_bias_body(rb_ref, o_ref, *, past_len, n_valid, rows):
    h = pl.program_id(0)
    width = past_len + LANES
    r = lax.broadcasted_iota(jnp.int32, (rows, width), 0)
    kpos = lax.broadcasted_iota(jnp.int32, (rows, width), 1)
    dist = past_len + r - kpos
    ok = (dist >= 0) & (kpos < past_len + n_valid)
    o_ref[...] = jnp.where(ok, _t5_bias(dist, rb_ref, h), NEG)


def _sample_bias(rel_bias, *, past_len, n_valid, rows):
    heads = rel_bias.shape[1]
    width = past_len + LANES
    return pl.pallas_call(
        functools.partial(_sample_bias_body, past_len=past_len, n_valid=n_valid, rows=rows),
        grid=(heads,),
        in_specs=[pl.BlockSpec(memory_space=pltpu.SMEM)],
        out_specs=pl.BlockSpec((rows, width), lambda h: (h, 0)),
        out_shape=jax.ShapeDtypeStruct((heads * rows, width), F32),
        name="sample_bias",
    )(rel_bias)


def _topk_keep(gate, idx, own, axis, n):
    past = idx < own
    gm = jnp.where(past, gate, -jnp.inf)
    rank = jnp.zeros(gate.shape, jnp.int32)
    for jp in range(n):
        gj = lax.slice_in_dim(gm, jp, jp + 1, axis=axis)
        beats = (gj > gm) | ((gj == gm) & (jp < idx))
        rank = rank + beats.astype(jnp.int32)
    keep = ((rank < MOBA_TOPK) & past) | (idx == own)
    return jnp.where(keep, 0.0, NEG)


def _topk_keep_rows(gate, idx, own):
    past = idx < own
    gm = jnp.where(past, gate, -jnp.inf)
    idx_f = idx.astype(F32)
    chosen = jnp.zeros(gate.shape, jnp.bool_)
    for _ in range(MOBA_TOPK):
        top = jnp.max(gm, axis=0, keepdims=True)
        first = jnp.min(jnp.where(gm == top, idx_f, float(gate.shape[0])), axis=0, keepdims=True)
        pick = idx_f == first
        chosen = chosen | pick
        gm = jnp.where(pick, -jnp.inf, gm)
    keep = (chosen & past) | (idx == own)
    return jnp.where(keep, 0.0, NEG)


def _moba_gate_body(qt_ref, k_ref, a_ref, km_scr, *, n_blocks):
    t = n_blocks * MOBA_BLOCK
    qt = qt_ref[0]
    lane = lax.broadcasted_iota(jnp.int32, (1, LANES), 1)
    for n in range(n_blocks):
        km_scr[pl.ds(n, 1), :] = jnp.sum(
            k_ref[0, n * MOBA_BLOCK:(n + 1) * MOBA_BLOCK, :], axis=0, keepdims=True) * (1.0 / MOBA_BLOCK)
    km = km_scr[...]
    blk = lax.broadcasted_iota(jnp.int32, (n_blocks, t), 0)
    own = lax.broadcasted_iota(jnp.int32, (n_blocks, t), 1) >> int(math.log2(MOBA_BLOCK))
    for hh in range(HEADS_PER_TILE):
        kmh = jnp.where(_in_head(lane, hh), km, 0.0)
        gate_t = jnp.dot(kmh, qt, precision=lax.Precision.HIGHEST, preferred_element_type=F32)
        a_ref[0, 0, hh * n_blocks:(hh + 1) * n_blocks, :] = _topk_keep_rows(gate_t, blk, own)


def _moba_gate(aqt, ak):
    b, d_att, t = aqt.shape
    tiles = d_att // LANES
    n_blocks = t // MOBA_BLOCK
    return pl.pallas_call(
        functools.partial(_moba_gate_body, n_blocks=n_blocks),
        grid=(b, tiles),
        in_specs=[pl.BlockSpec((1, LANES, t), lambda i, p: (i, p, 0)),
                  pl.BlockSpec((1, t, LANES), lambda i, p: (i, 0, p))],
        out_specs=pl.BlockSpec((1, 1, HEADS_PER_TILE * n_blocks, t), lambda i, p: (i, p, 0, 0)),
        out_shape=jax.ShapeDtypeStruct((b, tiles, HEADS_PER_TILE * n_blocks, t), F32),
        scratch_shapes=[pltpu.VMEM((n_blocks, LANES), F32)],
        compiler_params=pltpu.CompilerParams(
            dimension_semantics=("arbitrary", "arbitrary"), vmem_limit_bytes=VMEM_LIMIT),
        name="moba_gate",
    )(aqt, ak)


def _moba_prompt_body(qt_ref, k_ref, vt_ref, keep_ref, bias_ref, ga_ref, o_ref, kaug_scr, vtaug_scr, s_scr,
                      *, n_bias, n_blocks, group, tps):
    i = pl.program_id(2)
    rows = group * MOBA_BLOCK
    t = n_blocks * MOBA_BLOCK
    lg_block = int(math.log2(MOBA_BLOCK))
    feat = lax.broadcasted_iota(jnp.int32, (LANES, 1), 0)
    heads = [(tt, hh) for tt in range(tps) for hh in range(HEADS_PER_TILE)]
    tile = [slice(tt * LANES, (tt + 1) * LANES) for tt in range(tps)]
    ones_row = [((hh + 1) % HEADS_PER_TILE) * ATT_HEAD_DIM for hh in range(HEADS_PER_TILE)]

    @pl.when(i == 0)
    def _():
        blk = lax.broadcasted_iota(jnp.int32, (rows, LANES), 0) >> lg_block
        lane = lax.broadcasted_iota(jnp.int32, (rows, LANES), 1)
        for c in range(t // rows):
            sl = slice(c * rows, (c + 1) * rows)
            for hd, (tt, hh) in enumerate(heads):
                onehot = jnp.where(lane == hh * n_blocks + c * group + blk, 1.0, 0.0)
                kaug_scr[hd, sl, :] = jnp.concatenate([k_ref[0, sl, tile[tt]], onehot], axis=1).astype(BF16)
                vtaug_scr[hd, :, sl] = jnp.where(feat == ones_row[hh], 1.0, vt_ref[0, tile[tt], sl]).astype(BF16)

    q_aug = []
    for tt, hh in heads:
        qt = qt_ref[0, tile[tt], :] * (ATT_HEAD_DIM ** -0.5 * LOG2E)
        keep = keep_ref[0, tt]
        pad = jnp.zeros((LANES - keep.shape[0], MOBA_BLOCK), F32)
        q_aug.append(jnp.concatenate([jnp.where(_in_head(feat, hh), qt, 0.0), keep, pad], axis=0).astype(BF16))
    n_groups = (i >> int(math.log2(group))) + 1
    n_heads = len(heads)

    def scores(gi, m):
        r0 = pl.multiple_of(gi * rows, rows)
        m_new = []
        for hd in range(n_heads):
            s = jnp.dot(kaug_scr[hd, pl.ds(r0, rows), :], q_aug[hd], preferred_element_type=F32)
            mh = m[hd]
            for u in range(group):
                d = jnp.clip(i - (gi * group + u), 0, n_bias - 1)
                su = s[u * MOBA_BLOCK:(u + 1) * MOBA_BLOCK] + bias_ref[hd, d]
                s_scr[hd, pl.ds(r0 + u * MOBA_BLOCK, MOBA_BLOCK), :] = su
                mh = jnp.maximum(mh, jnp.max(su, axis=0, keepdims=True))
            m_new.append(mh)
        return tuple(m_new)

    def values(gi, m, m_before, acc):
        r0 = pl.multiple_of(gi * rows, rows)
        out = []
        for hd in range(n_heads):
            p = jnp.exp2(s_scr[hd, pl.ds(r0, rows), :] - m[hd]).astype(BF16)
            out.append(acc[hd] * jnp.exp2(m_before[hd] - m[hd])
                       + jnp.dot(vtaug_scr[hd, :, pl.ds(r0, rows)], p, preferred_element_type=F32))
        return tuple(out)

    m_init = tuple(jnp.full((1, MOBA_BLOCK), M_INIT, F32) for _ in range(n_heads))
    acc0 = tuple(jnp.zeros((LANES, MOBA_BLOCK), F32) for _ in range(n_heads))

    def stage(gi, carry):
        m, m_before, acc = carry
        acc = values(gi - 1, m, m_before, acc)
        return scores(gi, m), m, acc

    m, m_before, acc = lax.fori_loop(1, n_groups, stage, (scores(0, m_init), m_init, acc0))
    acc = values(n_groups - 1, m, m_before, acc)
    for tt in range(tps):
        out = jnp.zeros((LANES, MOBA_BLOCK), F32)
        for hh in range(HEADS_PER_TILE):
            a = acc[tt * HEADS_PER_TILE + hh]
            out = jnp.where(_in_head(feat, hh), a / a[ones_row[hh]:ones_row[hh] + 1, :], out)
        o_ref[0, :, tile[tt]] = (out.T * ga_ref[0, :, tile[tt]]).astype(o_ref.dtype)


def _moba_prompt(aqt, ak, avt, keep, bias, ga, *, group=4, tps=2):
    b, d_att, t = aqt.shape
    tiles = d_att // LANES
    n_blocks = t // MOBA_BLOCK
    n_bias = bias.shape[1]
    group = min(group, n_blocks)
    assert n_blocks % group == 0 and group & (group - 1) == 0 and tiles % tps == 0
    width = tps * LANES
    n_heads = tps * HEADS_PER_TILE
    once = dict(pipeline_mode=pl.Buffered(1))
    return pl.pallas_call(
        functools.partial(_moba_prompt_body, n_bias=n_bias, n_blocks=n_blocks, group=group, tps=tps),
        grid=(b, tiles // tps, n_blocks),
        in_specs=[pl.BlockSpec((1, width, MOBA_BLOCK), lambda ib, p, i: (ib, p, i)),
                  pl.BlockSpec((1, t, width), lambda ib, p, i: (ib, 0, p), **once),
                  pl.BlockSpec((1, width, t), lambda ib, p, i: (ib, p, 0), **once),
                  pl.BlockSpec((1, tps, keep.shape[2], MOBA_BLOCK), lambda ib, p, i: (ib, p, 0, i)),
                  pl.BlockSpec((n_heads, n_bias, MOBA_BLOCK, MOBA_BLOCK), lambda ib, p, i: (p, 0, 0, 0), **once),
                  pl.BlockSpec((1, MOBA_BLOCK, width), lambda ib, p, i: (ib, i, p))],
        out_specs=pl.BlockSpec((1, MOBA_BLOCK, width), lambda ib, p, i: (ib, i, p)),
        out_shape=jax.ShapeDtypeStruct((b, t, d_att), BF16),
        scratch_shapes=[pltpu.VMEM((n_heads, t, 2 * LANES), BF16),
                        pltpu.VMEM((n_heads, LANES, t), BF16),
                        pltpu.VMEM((n_heads, t, MOBA_BLOCK), F32)],
        compiler_params=pltpu.CompilerParams(
            dimension_semantics=("arbitrary", "arbitrary", "arbitrary"), vmem_limit_bytes=VMEM_LIMIT),
        name="moba_prompt",
    )(aqt, ak, avt, keep, bias, ga.reshape(b, t, d_att))


def _moba_sample_body(pt_ref, q_ref, kn_ref, vn_ref, bias_ref, ck_hbm, cv_hbm, o_ref,
                      kbuf, vbuf, sems, s_scr, *, n_pages, heads, rows):
    b = pl.program_id(0)
    n_slots = kbuf.shape[0]
    ahead = n_slots - 1
    slot = lax.rem(b, n_slots)

    def page_copies(seq, sl):
        out = []
        for p in range(n_pages):
            phys = pt_ref[seq, p]
            out.append(pltpu.make_async_copy(ck_hbm.at[phys], kbuf.at[sl, p], sems.at[0, sl]))
            out.append(pltpu.make_async_copy(cv_hbm.at[phys], vbuf.at[sl, p], sems.at[1, sl]))
        return out

    @pl.when(b == 0)
    def _():
        for seq in range(ahead):
            for cp in page_copies(seq, seq):
                cp.start()

    @pl.when(b + ahead < pl.num_programs(0))
    def _():
        for cp in page_copies(b + ahead, lax.rem(b + ahead, n_slots)):
            cp.start()

    for cp in page_copies(b, slot):
        cp.wait()
    k_pages = [kbuf.at[slot, p] for p in range(n_pages)]
    v_pages = [vbuf.at[slot, p] for p in range(n_pages)]
    d_att = heads * ATT_HEAD_DIM
    page = kbuf.shape[-1]
    pages_per_block = MOBA_BLOCK // page
    n_blocks = n_pages // pages_per_block
    n_q = heads * rows
    feat = lax.broadcasted_iota(jnp.int32, (1, d_att), 1)
    lane = lax.broadcasted_iota(jnp.int32, (n_q, LANES), 1)

    q = q_ref[0] * (ATT_HEAD_DIM ** -0.5)
    q_bd = jnp.concatenate([jnp.where(_in_head(feat, h), q, 0.0) for h in range(heads)], axis=0).astype(BF16)

    gate = jnp.zeros((n_q, LANES), F32)
    for n in range(n_blocks):
        tot = jnp.zeros((n_q, page), F32)
        for p in range(n * pages_per_block, (n + 1) * pages_per_block):
            kt = k_pages[p][...].astype(BF16)
            s = jnp.dot(q_bd, kt, preferred_element_type=F32)
            s_scr[:, p * page:(p + 1) * page] = s
            tot = tot + s
        gate = jnp.where(lane == n, jnp.sum(tot, axis=-1, keepdims=True) * (1.0 / MOBA_BLOCK), gate)
    keep = _topk_keep(gate, lane, jnp.full(gate.shape, n_blocks, jnp.int32), axis=1, n=n_blocks)

    def new_rows(ref):
        return jnp.concatenate([ref[0], jnp.zeros((LANES - rows, d_att), F32)], axis=0).astype(BF16)

    own = n_pages * page
    s_own = lax.dot_general(q_bd, new_rows(kn_ref), NT_DIMS, preferred_element_type=F32)
    s_own = s_own + bias_ref[:, own:own + LANES]
    s_scr[:, own:own + LANES] = s_own
    m_wide = s_own
    for p in range(n_pages):
        n = p // pages_per_block
        s = s_scr[:, p * page:(p + 1) * page] + bias_ref[:, p * page:(p + 1) * page] + keep[:, n:n + 1]
        s_scr[:, p * page:(p + 1) * page] = s
        m_wide = jnp.maximum(m_wide, s)
    m = jnp.max(m_wide, axis=-1, keepdims=True)

    pr = jnp.exp(s_scr[:, own:own + LANES] - m)
    l_wide = pr
    acc = jnp.dot(pr.astype(BF16), new_rows(vn_ref), preferred_element_type=F32)
    for p in range(n_pages):
        pr = jnp.exp(s_scr[:, p * page:(p + 1) * page] - m)
        l_wide = l_wide + pr
        vt = v_pages[p][...].astype(BF16)
        acc = acc + lax.dot_general(pr.astype(BF16), vt, NT_DIMS, preferred_element_type=F32)
    acc = acc / jnp.sum(l_wide, axis=-1, keepdims=True)
    out = jnp.zeros((rows, d_att), F32)
    for h in range(heads):
        out = jnp.where(_in_head(feat, h), acc[h * rows:(h + 1) * rows, :], out)
    o_ref[0] = out


def _moba_sample(aq, ak, av, cache_kt, cache_vt, page_table, bias, *, heads):
    nb, rows, d_att = aq.shape
    n_pages = page_table.shape[1]
    page = cache_kt.shape[-1]
    assert cache_kt.shape[1:] == (d_att, page) and MOBA_BLOCK % page == 0 and page == LANES
    tok = pl.BlockSpec((1, rows, d_att), lambda b, pt: (b, 0, 0))
    in_hbm = pl.BlockSpec(memory_space=pl.ANY)
    n_slots = 3
    assert nb >= n_slots
    page_slots = pltpu.VMEM((n_slots, n_pages, d_att, page), F32)
    grid_spec = pltpu.PrefetchScalarGridSpec(
        num_scalar_prefetch=1,
        grid=(nb,),
        in_specs=[tok, tok, tok, pl.BlockSpec(bias.shape, lambda b, pt: (0, 0)), in_hbm, in_hbm],
        out_specs=tok,
        scratch_shapes=[page_slots, page_slots, pltpu.SemaphoreType.DMA((2, n_slots)),
                        pltpu.VMEM(bias.shape, F32)],
    )
    return pl.pallas_call(
        functools.partial(_moba_sample_body, n_pages=n_pages, heads=heads, rows=rows),
        grid_spec=grid_spec,
        out_shape=jax.ShapeDtypeStruct((nb, rows, d_att), F32),
        compiler_params=pltpu.CompilerParams(
            dimension_semantics=("arbitrary",), vmem_limit_bytes=VMEM_LIMIT),
        name="moba_sample",
    )(page_table, aq, ak, av, bias, cache_kt, cache_vt)


def _out_proj_body(*refs, d_lin, gated):
    if gated:
        ol_ref, oa_ref, x_ref, w_ref, gpost_ref, y_ref = refs
        oa = oa_ref[...]
    else:
        ol_ref, oa_ref, ga_ref, x_ref, w_ref, gpost_ref, y_ref = refs
        oa = oa_ref[...] * ga_ref[...]
    o = jnp.dot(ol_ref[...].astype(BF16), w_ref[0:d_lin, :], preferred_element_type=F32)
    o = o + jnp.dot(oa.astype(BF16), w_ref[d_lin:, :], preferred_element_type=F32)
    y_ref[...] = x_ref[...] + _rms(o, gpost_ref[...])


def _out_proj(o_lin, o_att, ga, x, w_bf, gpost, *, tm=256):
    rows, d_model = x.shape
    tm = min(tm, rows)
    assert rows % tm == 0
    d_lin = o_lin.shape[1]
    row_spec = lambda w: pl.BlockSpec((tm, w), lambda i: (i, 0))
    acts = [o_lin, o_att] + ([] if ga is None else [ga])
    return pl.pallas_call(
        functools.partial(_out_proj_body, d_lin=d_lin, gated=ga is None),
        grid=(rows // tm,),
        in_specs=[row_spec(a.shape[1]) for a in acts] + [
            row_spec(d_model),
            pl.BlockSpec(w_bf.shape, lambda i: (0, 0)),
            pl.BlockSpec((1, d_model), lambda i: (0, 0)),
        ],
        out_specs=row_spec(d_model),
        out_shape=jax.ShapeDtypeStruct((rows, d_model), F32),
        compiler_params=pltpu.CompilerParams(dimension_semantics=("arbitrary",)),
        name="out_proj",
    )(*acts, x, w_bf, gpost)


def kernel(x_prompt, x_sample, cache_k, cache_v, state_hgrn, page_table, w_in, w_out,
           norm_pre, norm_post, norm_lin_out, lin_lower_bound, rel_bias):
    depth = w_in.shape[0]
    assert depth == 1 and lin_lower_bound.shape[0] == depth + 1
    b, t, d_model = x_prompt.shape
    nb, ts, _ = x_sample.shape
    d_lin = lin_lower_bound.shape[1]
    d_att = w_out.shape[1] - d_lin
    lin_heads = d_lin // LIN_HEAD_DIM
    att_heads = rel_bias.shape[1]
    n_pages = page_table.shape[1]
    page = cache_k.shape[2]
    past_len = n_pages * page
    assert d_att == att_heads * ATT_HEAD_DIM and t % MOBA_BLOCK == 0
    assert past_len % MOBA_BLOCK == 0 and ts <= SUBLANES

    w_in_bf = w_in[0].astype(BF16)
    w_out_bf = w_out[0].astype(BF16)
    gpre, gpost, glin = norm_pre, norm_post, norm_lin_out
    proj = functools.partial(_in_proj, gpre=gpre, w_bf=w_in_bf, llb=lin_lower_bound,
                             d_lin=d_lin, d_att=d_att)

    xp = x_prompt.reshape(b * t, d_model)
    q, g, kk, v, gl, ga, ak, aqt, akt, avt = proj(xp, seq_len=t, tm=512)
    q, g, kk, v, gl, ak = [a.reshape(b, t, -1) for a in (q, g, kk, v, gl, ak)]
    s0 = jnp.zeros((b, lin_heads, LIN_HEAD_DIM, LIN_HEAD_DIM), F32)
    o_lin, s_prompt = _hgrn_tiles(q, g, kk, v, gl, s0, glin, chunk=32, t_tile=math.gcd(t, 1024))
    keep = _moba_gate(aqt, ak)
    n_bias = 6
    assert (n_bias - 1) * MOBA_BLOCK - (MOBA_BLOCK - 1) >= MAX_DISTANCE
    bias = _prompt_bias(rel_bias, n_bias)
    o_att = _moba_prompt(aqt, ak, avt, keep, bias, ga)
    y_prompt = _out_proj(o_lin.reshape(b * t, d_lin), o_att.reshape(b * t, d_att),
                         None, xp, w_out_bf, gpost, tm=2048).reshape(b, t, d_model)

    rows = SUBLANES
    xs = jnp.pad(x_sample, ((0, 0), (0, rows - ts), (0, 0))).reshape(nb * rows, d_model)
    qs, gs, kks, vs, gls, gas, aqs, aks, avs = [a.reshape(nb, rows, -1) for a in proj(xs)]
    o_lin_s, s_sample = _hgrn_step(qs, gs, kks, vs, gls, state_hgrn[0], glin, valid=ts,
                                   n_seq=math.gcd(nb, 32))
    bias_s = _sample_bias(rel_bias, past_len=past_len, n_valid=ts, rows=rows)
    cache_kt = cache_k[0].transpose(0, 2, 3, 1).reshape(-1, d_att, page)
    cache_vt = cache_v[0].transpose(0, 2, 3, 1).reshape(-1, d_att, page)
    o_att_s = _moba_sample(aqs, aks, avs, cache_kt, cache_vt, page_table, bias_s, heads=att_heads)
    y_s = _out_proj(o_lin_s.reshape(nb * rows, d_lin), o_att_s.reshape(nb * rows, d_att),
                    gas.reshape(nb * rows, d_att), xs, w_out_bf, gpost)
    y_sample = y_s.reshape(nb, rows, d_model)[:, :ts]

    def prompt_kv(a):
        return a.reshape(1, b, att_heads, ATT_HEAD_DIM, t).transpose(0, 1, 4, 2, 3)

    kvs_shape = (1, nb, ts, att_heads, ATT_HEAD_DIM)
    return (y_prompt, y_sample, prompt_kv(akt), prompt_kv(avt), s_prompt[None],
            aks[:, :ts].reshape(kvs_shape), avs[:, :ts].reshape(kvs_shape), s_sample[None])
```

```python
import functools
import math

import jax
import jax.numpy as jnp
import numpy as np
from jax import lax
from jax.experimental import pallas as pl
from jax.experimental.pallas import tpu as pltpu

F32 = jnp.float32
BF16 = jnp.bfloat16

EPS = 1e-6
LIN_HEAD_DIM = 128
ATT_HEAD_DIM = 64
MOBA_BLOCK = 256
MOBA_TOPK = 3
N_BUCKETS = 32
MAX_DISTANCE = 1024
NEG = -2e30
M_INIT = -1e30
LOG2E = 1.4426950408889634
LANES = 128
SUBLANES = 8
VMEM_LIMIT = 56 * 1024 * 1024
HEADS_PER_TILE = LANES // ATT_HEAD_DIM

NT_DIMS = (((1,), (1,)), ((), ()))
TN_DIMS = (((0,), (0,)), ((), ()))


def _sigmoid(x):
    return 1.0 / (1.0 + jnp.exp(-x))


def _silu(x):
    return x * _sigmoid(x)


def _rms(x, g):
    return x * lax.rsqrt(jnp.mean(x * x, axis=-1, keepdims=True) + EPS) * g


def _in_head(idx, head):
    return (idx >= head * ATT_HEAD_DIM) & (idx < (head + 1) * ATT_HEAD_DIM)


def _in_proj_body(x_ref, gpre_ref, w_ref, llb_ref, *rest, d_lin, d_att, feat_major):
    if feat_major:
        wt_ref, q_o, g_o, kk_o, v_o, gl_o, ga_o, ak_o, aqt_o, akt_o, avt_o = rest
    else:
        q_o, g_o, kk_o, v_o, gl_o, ga_o, aq_o, ak_o, av_o = rest
    h = _rms(x_ref[...], gpre_ref[...]).astype(BF16)

    def proj(col, width):
        return jnp.dot(h, w_ref[:, col:col + width], preferred_element_type=F32)

    def proj_t(row, width):
        return lax.dot_general(wt_ref[row:row + width, :], h, NT_DIMS, preferred_element_type=F32)

    llb = llb_ref[...]
    e = jnp.exp(llb - jnp.max(llb, axis=0, keepdims=True))
    lb = e[0:1, :] / jnp.sum(e, axis=0, keepdims=True)

    q_o[...] = _silu(proj(0, d_lin))
    f = lb + (1.0 - lb) * _sigmoid(proj(d_lin, d_lin))
    g_o[...] = jnp.log(f)
    kk_o[...] = 1.0 - f
    v_o[...] = proj(2 * d_lin, d_lin)
    gl_o[...] = _silu(proj(3 * d_lin, d_lin))
    a0 = 4 * d_lin
    ga_o[...] = _silu(proj(a0 + 3 * d_att, d_att))
    if feat_major:
        aqt_o[0] = proj_t(0, d_att)
        akt = proj_t(d_att, d_att)
        akt_o[0] = akt
        ak_o[...] = akt.T
        avt_o[0] = proj_t(2 * d_att, d_att)
    else:
        aq_o[...] = proj(a0, d_att)
        ak_o[...] = proj(a0 + d_att, d_att)
        av_o[...] = proj(a0 + 2 * d_att, d_att)


def _in_proj(x, gpre, w_bf, llb, *, d_lin, d_att, seq_len=None, tm=256):
    rows, d_model = x.shape
    tm = min(tm, rows)
    assert rows % tm == 0
    d_in = w_bf.shape[1]
    feat_major = seq_len is not None
    row_spec = lambda w: pl.BlockSpec((tm, w), lambda i: (i, 0))
    in_specs = [
        pl.BlockSpec((tm, d_model), lambda i: (i, 0)),
        pl.BlockSpec((1, d_model), lambda i: (0, 0)),
        pl.BlockSpec((d_model, d_in), lambda i: (0, 0), pipeline_mode=pl.Buffered(1)),
        pl.BlockSpec(llb.shape, lambda i: (0, 0)),
    ]
    args = [x, gpre, w_bf, llb]
    out_specs = [row_spec(d_lin)] * 5 + [row_spec(d_att)]
    out_shape = [jax.ShapeDtypeStruct((rows, d_lin), F32)] * 5 + [jax.ShapeDtypeStruct((rows, d_att), F32)]
    if feat_major:
        assert seq_len % tm == 0
        tiles = seq_len // tm
        a0 = 4 * d_lin
        wt_bf = w_bf[:, a0:a0 + 3 * d_att].T
        in_specs.append(pl.BlockSpec(wt_bf.shape, lambda i: (0, 0), pipeline_mode=pl.Buffered(1)))
        args.append(wt_bf)
        t_spec = pl.BlockSpec((1, d_att, tm), lambda i: (i // tiles, 0, i % tiles))
        t_shape = jax.ShapeDtypeStruct((rows // seq_len, d_att, seq_len), F32)
        out_specs += [row_spec(d_att)] + [t_spec] * 3
        out_shape += [jax.ShapeDtypeStruct((rows, d_att), F32)] + [t_shape] * 3
    else:
        out_specs += [row_spec(d_att)] * 3
        out_shape += [jax.ShapeDtypeStruct((rows, d_att), F32)] * 3
    return pl.pallas_call(
        functools.partial(_in_proj_body, d_lin=d_lin, d_att=d_att, feat_major=feat_major),
        grid=(rows // tm,),
        in_specs=in_specs,
        out_specs=out_specs,
        out_shape=out_shape,
        compiler_params=pltpu.CompilerParams(
            dimension_semantics=("arbitrary",), vmem_limit_bytes=VMEM_LIMIT),
        name="in_proj",
    )(*args)


def _hgrn_step_body(q_ref, g_ref, kk_ref, v_ref, gl_ref, s0_ref, glin_ref, o_ref, sout_ref, b_scr,
                    *, rows, valid, n_seq):
    row = lax.broadcasted_iota(jnp.int32, (rows, 1), 0)
    for s in range(n_seq):
        acc = jnp.zeros((1, LIN_HEAD_DIM), F32)
        for r in range(rows):
            if r < valid:
                acc = acc + g_ref[s, r:r + 1, :]
            b_scr[s, r:r + 1, :] = acc
    seqs = range(n_seq)
    b = [b_scr[s] for s in seqs]
    b_last = [b_scr[s, rows - 1:rows, :] for s in seqs]
    q = [q_ref[s] for s in seqs]
    kk = [jnp.where(row < valid, kk_ref[s], 0.0) for s in seqs]

    o = [jnp.dot((q[s] * jnp.exp(b[s])).astype(BF16), s0_ref[s, 0].astype(BF16), preferred_element_type=F32)
         for s in seqs]
    for s in seqs:
        for j in range(valid):
            w = q[s] * kk_ref[s, j:j + 1, :] * jnp.exp(jnp.minimum(b[s] - b_scr[s, j:j + 1, :], 0.0))
            a = jnp.sum(w, axis=-1, keepdims=True)
            o[s] = o[s] + jnp.where(row >= j, a, 0.0) * v_ref[s, j:j + 1, :]
    u = [lax.dot_general((kk[s] * jnp.exp(b_last[s] - b[s])).astype(BF16), v_ref[s].astype(BF16), TN_DIMS,
                         preferred_element_type=F32) for s in seqs]
    ones = jnp.ones((rows, LIN_HEAD_DIM), BF16)
    decay = []
    for s in seqs:
        rest = jnp.exp(b_last[s])
        pieces = jnp.zeros((rows, LIN_HEAD_DIM), F32)
        for pi in range(3):
            piece = rest.astype(BF16).astype(F32)
            rest = rest - piece
            pieces = jnp.where(row == pi, piece, pieces)
        decay.append(lax.dot_general(pieces.astype(BF16), ones, TN_DIMS, preferred_element_type=F32))
    for s in seqs:
        sout_ref[s, 0] = s0_ref[s, 0] * decay[s] + u[s]
        o_ref[s] = _rms(o[s], glin_ref[...]) * gl_ref[s]


def _hgrn_step(q, g, kk, v, gl, s0, glin, *, valid, n_seq):
    nb, rows, d_lin = q.shape
    heads = d_lin // LIN_HEAD_DIM
    assert rows >= 3 and nb % n_seq == 0
    seq_spec = pl.BlockSpec((n_seq, rows, LIN_HEAD_DIM), lambda b, h: (b, 0, h))
    st_spec = pl.BlockSpec((n_seq, 1, LIN_HEAD_DIM, LIN_HEAD_DIM), lambda b, h: (b, h, 0, 0))
    return pl.pallas_call(
        functools.partial(_hgrn_step_body, rows=rows, valid=valid, n_seq=n_seq),
        grid=(nb // n_seq, heads),
        in_specs=[seq_spec] * 5 + [st_spec, pl.BlockSpec((1, LIN_HEAD_DIM), lambda b, h: (0, 0))],
        out_specs=[seq_spec, st_spec],
        out_shape=[jax.ShapeDtypeStruct((nb, rows, d_lin), F32),
                   jax.ShapeDtypeStruct(s0.shape, F32)],
        scratch_shapes=[pltpu.VMEM((n_seq, rows, LIN_HEAD_DIM), F32)],
        compiler_params=pltpu.CompilerParams(dimension_semantics=("arbitrary", "arbitrary")),
        name="hgrn_step",
    )(q, g, kk, v, gl, s0, glin)


SAFE_DECAY = 60.0


def _hgrn_tile_body(q_ref, g_ref, kk_ref, v_ref, gl_ref, s0_ref, glin_ref, o_ref, sout_ref,
                    st_scr, b_scr, oi_scr, u_scr, kkp_scr, bp_scr, vp_scr, *, chunk, t_tile, heads):
    it = pl.program_id(1)
    hs = [slice(h * LIN_HEAD_DIM, (h + 1) * LIN_HEAD_DIM) for h in range(heads)]

    @pl.when(it == 0)
    def _():
        for h in range(heads):
            st_scr[h] = s0_ref[0, h].T

    lg = int(math.log2(chunk))
    n_groups = t_tile // LANES
    n_chunks = t_tile // chunk
    r = lax.broadcasted_iota(jnp.int32, (LANES, LANES), 0)
    c = lax.broadcasted_iota(jnp.int32, (LANES, LANES), 1)
    causal = ((r >> lg) == (c >> lg)) & (c <= r)
    tri = jnp.where(causal, 1.0, 0.0).astype(BF16)
    n_pieces = 3
    for h in range(heads):
        pieces = []
        for gi in range(n_groups):
            rest = g_ref[0, gi * LANES:(gi + 1) * LANES, hs[h]]
            for _ in range(n_pieces):
                piece = rest.astype(BF16)
                rest = rest - piece.astype(F32)
                pieces.append(piece)
        sums = jnp.dot(tri, jnp.concatenate(pieces, axis=1), preferred_element_type=F32)
        for gi in range(n_groups):
            b = jnp.zeros((LANES, LIN_HEAD_DIM), F32)
            for pi in range(n_pieces):
                col = (gi * n_pieces + pi) * LIN_HEAD_DIM
                b = b + sums[:, col:col + LIN_HEAD_DIM]
            b_scr[h, gi * LANES:(gi + 1) * LANES, :] = b

    def intra_mxu():
        atts = {}
        for h in range(heads):
            for gi in range(n_groups):
                rows = slice(gi * LANES, (gi + 1) * LANES)
                b = b_scr[h, rows, :]
                qe = (q_ref[0, rows, hs[h]] * jnp.exp(b)).astype(BF16)
                ke = (kk_ref[0, rows, hs[h]] * jnp.exp(-b)).astype(BF16)
                att = lax.dot_general(qe, ke, NT_DIMS, preferred_element_type=F32)
                atts[h, gi] = jnp.where(causal, att, 0.0).astype(BF16)
        for h in range(heads):
            for gi in range(n_groups):
                rows = slice(gi * LANES, (gi + 1) * LANES)
                oi_scr[h, rows, :] = jnp.dot(atts[h, gi], v_ref[0, rows, hs[h]].astype(BF16),
                                             preferred_element_type=F32)

    def intra_pairs():
        zeros = jnp.zeros((chunk, LIN_HEAD_DIM), F32)
        tmod = lax.broadcasted_iota(jnp.int32, (t_tile, 1), 0) & (chunk - 1)
        for h in range(heads):
            for dst, src in ((kkp_scr, kk_ref[0, :, hs[h]]), (bp_scr, b_scr[h]), (vp_scr, v_ref[0, :, hs[h]])):
                dst[0:chunk, :] = zeros
                dst[chunk:, :] = src
            oi_scr[h] = jnp.zeros((t_tile, LIN_HEAD_DIM), F32)
            q = q_ref[0, :, hs[h]]
            b = b_scr[h]

            def lag(d, carry, h=h, q=q, b=b):
                start = chunk - d
                w = q * kkp_scr[pl.ds(start, t_tile), :] * jnp.exp(
                    jnp.minimum(b - bp_scr[pl.ds(start, t_tile), :], 0.0))
                a = jnp.sum(w, axis=-1, keepdims=True)
                oi_scr[h] += jnp.where(tmod >= d, a, 0.0) * vp_scr[pl.ds(start, t_tile), :]
                return carry

            lax.fori_loop(0, chunk, lag, 0)

    lax.cond(jnp.min(b_scr[...]) >= -SAFE_DECAY, intra_mxu, intra_pairs)

    for h in range(heads):
        for n in range(n_chunks):
            rows = slice(n * chunk, (n + 1) * chunk)
            b_last = b_scr[h, (n + 1) * chunk - 1:(n + 1) * chunk, :]
            kt = (kk_ref[0, rows, hs[h]] * jnp.exp(b_last - b_scr[h, rows, :])).astype(BF16)
            u_scr[h, n] = lax.dot_general(v_ref[0, rows, hs[h]].astype(BF16), kt, TN_DIMS,
                                          preferred_element_type=F32)

    st = [st_scr[h] for h in range(heads)]
    for n in range(n_chunks):
        rows = slice(n * chunk, (n + 1) * chunk)
        for h in range(heads):
            b_last = b_scr[h, (n + 1) * chunk - 1:(n + 1) * chunk, :]
            o = lax.dot_general((q_ref[0, rows, hs[h]] * jnp.exp(b_scr[h, rows, :])).astype(BF16),
                                st[h].astype(BF16), NT_DIMS, preferred_element_type=F32) + oi_scr[h, rows, :]
            st[h] = st[h] * jnp.exp(b_last) + u_scr[h, n]
            o_ref[0, rows, hs[h]] = (_rms(o, glin_ref[...]) * gl_ref[0, rows, hs[h]]).astype(o_ref.dtype)
    for h in range(heads):
        st_scr[h] = st[h]

    @pl.when(it == pl.num_programs(1) - 1)
    def _():
        for h in range(heads):
            sout_ref[0, h] = st[h].T


def _hgrn_tiles(q, g, kk, v, gl, s0, glin, *, chunk, t_tile):
    nb, t, d_lin = q.shape
    heads = d_lin // LIN_HEAD_DIM
    assert LANES % chunk == 0 and t_tile % LANES == 0 and t % t_tile == 0
    seq_spec = pl.BlockSpec((1, t_tile, d_lin), lambda b, i: (b, i, 0))
    st_spec = pl.BlockSpec((1, heads, LIN_HEAD_DIM, LIN_HEAD_DIM), lambda b, i: (b, 0, 0, 0))
    tile = pltpu.VMEM((heads, t_tile, LIN_HEAD_DIM), F32)
    halo = pltpu.VMEM((chunk + t_tile, LIN_HEAD_DIM), F32)
    return pl.pallas_call(
        functools.partial(_hgrn_tile_body, chunk=chunk, t_tile=t_tile, heads=heads),
        grid=(nb, t // t_tile),
        in_specs=[seq_spec] * 5 + [st_spec, pl.BlockSpec((1, LIN_HEAD_DIM), lambda b, i: (0, 0))],
        out_specs=[seq_spec, st_spec],
        out_shape=[jax.ShapeDtypeStruct((nb, t, d_lin), BF16),
                   jax.ShapeDtypeStruct(s0.shape, F32)],
        scratch_shapes=[pltpu.VMEM((heads, LIN_HEAD_DIM, LIN_HEAD_DIM), F32), tile, tile,
                        pltpu.VMEM((heads, t_tile // chunk, LIN_HEAD_DIM, LIN_HEAD_DIM), F32),
                        halo, halo, halo],
        compiler_params=pltpu.CompilerParams(
            dimension_semantics=("arbitrary", "arbitrary"), vmem_limit_bytes=VMEM_LIMIT),
        name="hgrn_tiles",
    )(q, g, kk, v, gl, s0, glin)


def _t5_first_distances():
    max_exact = N_BUCKETS // 2
    n = np.arange(0, MAX_DISTANCE + 1, dtype=np.int32)
    ratio = np.maximum(n, 1).astype(np.float32) / np.float32(max_exact)
    large = max_exact + (np.log(ratio) / np.float32(math.log(MAX_DISTANCE / max_exact))
                         * np.float32(N_BUCKETS - max_exact)).astype(np.int32)
    bucket = np.where(n < max_exact, n, np.minimum(large, N_BUCKETS - 1))
    assert np.all(np.diff(bucket) >= 0) and bucket[-1] == N_BUCKETS - 1
    return [int(np.argmax(bucket >= bk)) for bk in range(N_BUCKETS)]


_T5_FIRST_DISTANCE = _t5_first_distances()


def _t5_bias(dist, rb_ref, head):
    n = jnp.maximum(dist, 0)
    out = jnp.full(dist.shape, rb_ref[0, head], F32)
    for bk in range(1, N_BUCKETS):
        out = jnp.where(n >= _T5_FIRST_DISTANCE[bk], rb_ref[bk, head], out)
    return out


def _prompt_bias_body(rb_ref, o_ref, *, n_tiles):
    h = pl.program_id(0)
    x = lax.broadcasted_iota(jnp.int32, (SUBLANES, 2 * MOBA_BLOCK), 1)
    for d in range(n_tiles):
        dist = (d - 1) * MOBA_BLOCK + x
        by_dist = jnp.where(dist >= 0, _t5_bias(dist, rb_ref, h) * LOG2E, NEG)
        wide = jnp.broadcast_to(by_dist[0:1, :], (MOBA_BLOCK, 2 * MOBA_BLOCK))
        o_ref[0, d] = pltpu.roll(wide, 0, 1, stride=1, stride_axis=0)[:, MOBA_BLOCK:]


def _prompt_bias(rel_bias, n_tiles):
    heads = rel_bias.shape[1]
    return pl.pallas_call(
        functools.partial(_prompt_bias_body, n_tiles=n_tiles),
        grid=(heads,),
        in_specs=[pl.BlockSpec(memory_space=pltpu.SMEM)],
        out_specs=pl.BlockSpec((1, n_tiles, MOBA_BLOCK, MOBA_BLOCK), lambda h: (h, 0, 0, 0)),
        out_shape=jax.ShapeDtypeStruct((heads, n_tiles, MOBA_BLOCK, MOBA_BLOCK), F32),
        name="prompt_bias",
    )(rel_bias)


def _sample_bias_body(rb_ref, o_ref, *, past_len, n_valid, rows):
    h = pl.program_id(0)
    width = past_len + LANES
    r = lax.broadcasted_iota(jnp.int32, (rows, width), 0)
    kpos = lax.broadcasted_iota(jnp.int32, (rows, width), 1)
    dist = past_len + r - kpos
    ok = (dist >= 0) & (kpos < past_len + n_valid)
    o_ref[...] = jnp.where(ok, _t5_bias(dist, rb_ref, h), NEG)


def _sample_bias(rel_bias, *, past_len, n_valid, rows):
    heads = rel_bias.shape[1]
    width = past_len + LANES
    return pl.pallas_call(
        functools.partial(_sample_bias_body, past_len=past_len, n_valid=n_valid, rows=rows),
        grid=(heads,),
        in_specs=[pl.BlockSpec(memory_space=pltpu.SMEM)],
        out_specs=pl.BlockSpec((rows, width), lambda h: (h, 0)),
        out_shape=jax.ShapeDtypeStruct((heads * rows, width), F32),
        name="sample_bias",
    )(rel_bias)


def _topk_keep(gate, idx, own, axis, n):
    past = idx < own
    gm = jnp.where(past, gate, -jnp.inf)
    rank = jnp.zeros(gate.shape, jnp.int32)
    for jp in range(n):
        gj = lax.slice_in_dim(gm, jp, jp + 1, axis=axis)
        beats = (gj > gm) | ((gj == gm) & (jp < idx))
        rank = rank + beats.astype(jnp.int32)
    keep = ((rank < MOBA_TOPK) & past) | (idx == own)
    return jnp.where(keep, 0.0, NEG)


def _topk_keep_rows(gate, idx, own):
    past = idx < own
    gm = jnp.where(past, gate, -jnp.inf)
    idx_f = idx.astype(F32)
    chosen = jnp.zeros(gate.shape, jnp.bool_)
    for _ in range(MOBA_TOPK):
        top = jnp.max(gm, axis=0, keepdims=True)
        first = jnp.min(jnp.where(gm == top, idx_f, float(gate.shape[0])), axis=0, keepdims=True)
        pick = idx_f == first
        chosen = chosen | pick
        gm = jnp.where(pick, -jnp.inf, gm)
    keep = (chosen & past) | (idx == own)
    return jnp.where(keep, 0.0, NEG)


def _moba_gate_body(qt_ref, k_ref, a_ref, km_scr, *, n_blocks):
    t = n_blocks * MOBA_BLOCK
    qt = qt_ref[0]
    lane = lax.broadcasted_iota(jnp.int32, (1, LANES), 1)
    for n in range(n_blocks):
        km_scr[pl.ds(n, 1), :] = jnp.sum(
            k_ref[0, n * MOBA_BLOCK:(n + 1) * MOBA_BLOCK, :], axis=0, keepdims=True) * (1.0 / MOBA_BLOCK)
    km = km_scr[...]
    blk = lax.broadcasted_iota(jnp.int32, (n_blocks, t), 0)
    own = lax.broadcasted_iota(jnp.int32, (n_blocks, t), 1) >> int(math.log2(MOBA_BLOCK))
    for hh in range(HEADS_PER_TILE):
        kmh = jnp.where(_in_head(lane, hh), km, 0.0)
        gate_t = jnp.dot(kmh, qt, precision=lax.Precision.HIGHEST, preferred_element_type=F32)
        a_ref[0, 0, hh * n_blocks:(hh + 1) * n_blocks, :] = _topk_keep_rows(gate_t, blk, own)


def _moba_gate(aqt, ak):
    b, d_att, t = aqt.shape
    tiles = d_att // LANES
    n_blocks = t // MOBA_BLOCK
    return pl.pallas_call(
        functools.partial(_moba_gate_body, n_blocks=n_blocks),
        grid=(b, tiles),
        in_specs=[pl.BlockSpec((1, LANES, t), lambda i, p: (i, p, 0)),
                  pl.BlockSpec((1, t, LANES), lambda i, p: (i, 0, p))],
        out_specs=pl.BlockSpec((1, 1, HEADS_PER_TILE * n_blocks, t), lambda i, p: (i, p, 0, 0)),
        out_shape=jax.ShapeDtypeStruct((b, tiles, HEADS_PER_TILE * n_blocks, t), F32),
        scratch_shapes=[pltpu.VMEM((n_blocks, LANES), F32)],
        compiler_params=pltpu.CompilerParams(
            dimension_semantics=("arbitrary", "arbitrary"), vmem_limit_bytes=VMEM_LIMIT),
        name="moba_gate",
    )(aqt, ak)


def _moba_prompt_body(qt_ref, k_ref, vt_ref, bias_ref, ga_ref, o_ref, kaug_scr, vtaug_scr, s_scr,
                      keep_scr, km_scr, *, n_bias, n_blocks, group, tps):
    i = pl.program_id(2)
    rows = group * MOBA_BLOCK
    t = n_blocks * MOBA_BLOCK
    lg_block = int(math.log2(MOBA_BLOCK))
    feat = lax.broadcasted_iota(jnp.int32, (LANES, 1), 0)
    heads = [(tt, hh) for tt in range(tps) for hh in range(HEADS_PER_TILE)]
    tile = [slice(tt * LANES, (tt + 1) * LANES) for tt in range(tps)]
    ones_row = [((hh + 1) % HEADS_PER_TILE) * ATT_HEAD_DIM for hh in range(HEADS_PER_TILE)]

    @pl.when(i == 0)
    def _():
        blk = lax.broadcasted_iota(jnp.int32, (rows, LANES), 0) >> lg_block
        lane = lax.broadcasted_iota(jnp.int32, (rows, LANES), 1)
        for c in range(t // rows):
            sl = slice(c * rows, (c + 1) * rows)
            for hd, (tt, hh) in enumerate(heads):
                onehot = jnp.where(lane == hh * n_blocks + c * group + blk, 1.0, 0.0)
                kaug_scr[hd, sl, :] = jnp.concatenate([k_ref[0, sl, tile[tt]], onehot], axis=1).astype(BF16)
                vtaug_scr[hd, :, sl] = jnp.where(feat == ones_row[hh], 1.0, vt_ref[0, tile[tt], sl]).astype(BF16)
        lane1 = lax.broadcasted_iota(jnp.int32, (1, LANES), 1)
        blk_t = lax.broadcasted_iota(jnp.int32, (n_blocks, t), 0)
        own_t = lax.broadcasted_iota(jnp.int32, (n_blocks, t), 1) >> lg_block
        for tt in range(tps):
            for n in range(n_blocks):
                km_scr[pl.ds(n, 1), :] = jnp.sum(k_ref[0, n * MOBA_BLOCK:(n + 1) * MOBA_BLOCK, tile[tt]],
                                                 axis=0, keepdims=True) * (1.0 / MOBA_BLOCK)
            km = km_scr[...]
            for hh in range(HEADS_PER_TILE):
                kmh = jnp.where(_in_head(lane1, hh), km, 0.0)
                gate_t = jnp.dot(kmh, qt_ref[0, tile[tt], :], precision=lax.Precision.HIGHEST,
                                 preferred_element_type=F32)
                keep_scr[tt, hh * n_blocks:(hh + 1) * n_blocks, :] = _topk_keep_rows(gate_t, blk_t, own_t)

    q0 = pl.multiple_of(i * MOBA_BLOCK, MOBA_BLOCK)
    q_aug = []
    for tt, hh in heads:
        qt = qt_ref[0, tile[tt], pl.ds(q0, MOBA_BLOCK)] * (ATT_HEAD_DIM ** -0.5 * LOG2E)
        keep = keep_scr[tt, :, pl.ds(q0, MOBA_BLOCK)]
        pad = jnp.zeros((LANES - keep.shape[0], MOBA_BLOCK), F32)
        q_aug.append(jnp.concatenate([jnp.where(_in_head(feat, hh), qt, 0.0), keep, pad], axis=0).astype(BF16))
    n_groups = (i >> int(math.log2(group))) + 1
    n_heads = len(heads)

    def scores(gi, m):
        r0 = pl.multiple_of(gi * rows, rows)
        m_new = []
        for hd in range(n_heads):
            s = jnp.dot(kaug_scr[hd, pl.ds(r0, rows), :], q_aug[hd], preferred_element_type=F32)
            mh = m[hd]
            for u in range(group):
                d = jnp.clip(i - (gi * group + u), 0, n_bias - 1)
                su = s[u * MOBA_BLOCK:(u + 1) * MOBA_BLOCK] + bias_ref[hd, d]
                s_scr[hd, pl.ds(r0 + u * MOBA_BLOCK, MOBA_BLOCK), :] = su
                mh = jnp.maximum(mh, jnp.max(su, axis=0, keepdims=True))
            m_new.append(mh)
        return tuple(m_new)

    def values(gi, m, m_before, acc):
        r0 = pl.multiple_of(gi * rows, rows)
        out = []
        for hd in range(n_heads):
            p = jnp.exp2(s_scr[hd, pl.ds(r0, rows), :] - m[hd]).astype(BF16)
            out.append(acc[hd] * jnp.exp2(m_before[hd] - m[hd])
                       + jnp.dot(vtaug_scr[hd, :, pl.ds(r0, rows)], p, preferred_element_type=F32))
        return tuple(out)

    m_init = tuple(jnp.full((1, MOBA_BLOCK), M_INIT, F32) for _ in range(n_heads))
    acc0 = tuple(jnp.zeros((LANES, MOBA_BLOCK), F32) for _ in range(n_heads))

    def stage(gi, carry):
        m, m_before, acc = carry
        acc = values(gi - 1, m, m_before, acc)
        return scores(gi, m), m, acc

    m, m_before, acc = lax.fori_loop(1, n_groups, stage, (scores(0, m_init), m_init, acc0))
    acc = values(n_groups - 1, m, m_before, acc)
    for tt in range(tps):
        out = jnp.zeros((LANES, MOBA_BLOCK), F32)
        for hh in range(HEADS_PER_TILE):
            a = acc[tt * HEADS_PER_TILE + hh]
            out = jnp.where(_in_head(feat, hh), a / a[ones_row[hh]:ones_row[hh] + 1, :], out)
        o_ref[0, :, tile[tt]] = (out.T * ga_ref[0, :, tile[tt]]).astype(o_ref.dtype)


def _moba_prompt(aqt, ak, avt, bias, ga, *, group=4, tps=2):
    b, d_att, t = aqt.shape
    tiles = d_att // LANES
    n_blocks = t // MOBA_BLOCK
    n_bias = bias.shape[1]
    group = min(group, n_blocks)
    assert n_blocks % group == 0 and group & (group - 1) == 0 and tiles % tps == 0
    width = tps * LANES
    n_heads = tps * HEADS_PER_TILE
    once = dict(pipeline_mode=pl.Buffered(1))
    return pl.pallas_call(
        functools.partial(_moba_prompt_body, n_bias=n_bias, n_blocks=n_blocks, group=group, tps=tps),
        grid=(b, tiles // tps, n_blocks),
        in_specs=[pl.BlockSpec((1, width, t), lambda ib, p, i: (ib, p, 0), **once),
                  pl.BlockSpec((1, t, width), lambda ib, p, i: (ib, 0, p), **once),
                  pl.BlockSpec((1, width, t), lambda ib, p, i: (ib, p, 0), **once),
                  pl.BlockSpec((n_heads, n_bias, MOBA_BLOCK, MOBA_BLOCK), lambda ib, p, i: (p, 0, 0, 0), **once),
                  pl.BlockSpec((1, MOBA_BLOCK, width), lambda ib, p, i: (ib, i, p))],
        out_specs=pl.BlockSpec((1, MOBA_BLOCK, width), lambda ib, p, i: (ib, i, p)),
        out_shape=jax.ShapeDtypeStruct((b, t, d_att), BF16),
        scratch_shapes=[pltpu.VMEM((n_heads, t, 2 * LANES), BF16),
                        pltpu.VMEM((n_heads, LANES, t), BF16),
                        pltpu.VMEM((n_heads, t, MOBA_BLOCK), F32),
                        pltpu.VMEM((tps, HEADS_PER_TILE * n_blocks, t), F32),
                        pltpu.VMEM((n_blocks, LANES), F32)],
        compiler_params=pltpu.CompilerParams(
            dimension_semantics=("arbitrary", "arbitrary", "arbitrary"), vmem_limit_bytes=VMEM_LIMIT),
        name="moba_prompt",
    )(aqt, ak, avt, bias, ga.reshape(b, t, d_att))


def _moba_sample_body(pt_ref, q_ref, kn_ref, vn_ref, bias_ref, ck_hbm, cv_hbm, o_ref,
                      kbuf, vbuf, sems, s_scr, *, n_pages, heads, rows):
    b = pl.program_id(0)
    n_slots = kbuf.shape[0]
    ahead = n_slots - 1
    slot = lax.rem(b, n_slots)

    def page_copies(seq, sl):
        out = []
        for p in range(n_pages):
            phys = pt_ref[seq, p]
            out.append(pltpu.make_async_copy(ck_hbm.at[phys], kbuf.at[sl, p], sems.at[0, sl]))
            out.append(pltpu.make_async_copy(cv_hbm.at[phys], vbuf.at[sl, p], sems.at[1, sl]))
        return out

    @pl.when(b == 0)
    def _():
        for seq in range(ahead):
            for cp in page_copies(seq, seq):
                cp.start()

    @pl.when(b + ahead < pl.num_programs(0))
    def _():
        for cp in page_copies(b + ahead, lax.rem(b + ahead, n_slots)):
            cp.start()

    for cp in page_copies(b, slot):
        cp.wait()
    k_pages = [kbuf.at[slot, p] for p in range(n_pages)]
    v_pages = [vbuf.at[slot, p] for p in range(n_pages)]
    d_att = heads * ATT_HEAD_DIM
    page = kbuf.shape[-1]
    pages_per_block = MOBA_BLOCK // page
    n_blocks = n_pages // pages_per_block
    n_q = heads * rows
    feat = lax.broadcasted_iota(jnp.int32, (1, d_att), 1)
    lane = lax.broadcasted_iota(jnp.int32, (n_q, LANES), 1)

    q = q_ref[0] * (ATT_HEAD_DIM ** -0.5)
    q_bd = jnp.concatenate([jnp.where(_in_head(feat, h), q, 0.0) for h in range(heads)], axis=0).astype(BF16)

    gate = jnp.zeros((n_q, LANES), F32)
    for n in range(n_blocks):
        tot = jnp.zeros((n_q, page), F32)
        for p in range(n * pages_per_block, (n + 1) * pages_per_block):
            kt = k_pages[p][...].astype(BF16)
            s = jnp.dot(q_bd, kt, preferred_element_type=F32)
            s_scr[:, p * page:(p + 1) * page] = s
            tot = tot + s
        gate = jnp.where(lane == n, jnp.sum(tot, axis=-1, keepdims=True) * (1.0 / MOBA_BLOCK), gate)
    keep = _topk_keep(gate, lane, jnp.full(gate.shape, n_blocks, jnp.int32), axis=1, n=n_blocks)

    def new_rows(ref):
        return jnp.concatenate([ref[0], jnp.zeros((LANES - rows, d_att), F32)], axis=0).astype(BF16)

    own = n_pages * page
    s_own = lax.dot_general(q_bd, new_rows(kn_ref), NT_DIMS, preferred_element_type=F32)
    s_own = s_own + bias_ref[:, own:own + LANES]
    s_scr[:, own:own + LANES] = s_own
    m_wide = s_own
    for p in range(n_pages):
        n = p // pages_per_block
        s = s_scr[:, p * page:(p + 1) * page] + bias_ref[:, p * page:(p + 1) * page] + keep[:, n:n + 1]
        s_scr[:, p * page:(p + 1) * page] = s
        m_wide = jnp.maximum(m_wide, s)
    m = jnp.max(m_wide, axis=-1, keepdims=True)

    pr = jnp.exp(s_scr[:, own:own + LANES] - m)
    l_wide = pr
    acc = jnp.dot(pr.astype(BF16), new_rows(vn_ref), preferred_element_type=F32)
    for p in range(n_pages):
        pr = jnp.exp(s_scr[:, p * page:(p + 1) * page] - m)
        l_wide = l_wide + pr
        vt = v_pages[p][...].astype(BF16)
        acc = acc + lax.dot_general(pr.astype(BF16), vt, NT_DIMS, preferred_element_type=F32)
    acc = acc / jnp.sum(l_wide, axis=-1, keepdims=True)
    out = jnp.zeros((rows, d_att), F32)
    for h in range(heads):
        out = jnp.where(_in_head(feat, h), acc[h * rows:(h + 1) * rows, :], out)
    o_ref[0] = out


def _moba_sample(aq, ak, av, cache_kt, cache_vt, page_table, bias, *, heads):
    nb, rows, d_att = aq.shape
    n_pages = page_table.shape[1]
    page = cache_kt.shape[-1]
    assert cache_kt.shape[1:] == (d_att, page) and MOBA_BLOCK % page == 0 and page == LANES
    tok = pl.BlockSpec((1, rows, d_att), lambda b, pt: (b, 0, 0))
    in_hbm = pl.BlockSpec(memory_space=pl.ANY)
    n_slots = 3
    assert nb >= n_slots
    page_slots = pltpu.VMEM((n_slots, n_pages, d_att, page), F32)
    grid_spec = pltpu.PrefetchScalarGridSpec(
        num_scalar_prefetch=1,
        grid=(nb,),
        in_specs=[tok, tok, tok, pl.BlockSpec(bias.shape, lambda b, pt: (0, 0)), in_hbm, in_hbm],
        out_specs=tok,
        scratch_shapes=[page_slots, page_slots, pltpu.SemaphoreType.DMA((2, n_slots)),
                        pltpu.VMEM(bias.shape, F32)],
    )
    return pl.pallas_call(
        functools.partial(_moba_sample_body, n_pages=n_pages, heads=heads, rows=rows),
        grid_spec=grid_spec,
        out_shape=jax.ShapeDtypeStruct((nb, rows, d_att), F32),
        compiler_params=pltpu.CompilerParams(
            dimension_semantics=("arbitrary",), vmem_limit_bytes=VMEM_LIMIT),
        name="moba_sample",
    )(page_table, aq, ak, av, bias, cache_kt, cache_vt)


def _out_proj_body(*refs, d_lin, gated):
    if gated:
        ol_ref, oa_ref, x_ref, w_ref, gpost_ref, y_ref = refs
        oa = oa_ref[...]
    else:
        ol_ref, oa_ref, ga_ref, x_ref, w_ref, gpost_ref, y_ref = refs
        oa = oa_ref[...] * ga_ref[...]
    o = jnp.dot(ol_ref[...].astype(BF16), w_ref[0:d_lin, :], preferred_element_type=F32)
    o = o + jnp.dot(oa.astype(BF16), w_ref[d_lin:, :], preferred_element_type=F32)
    y_ref[...] = x_ref[...] + _rms(o, gpost_ref[...])


def _out_proj(o_lin, o_att, ga, x, w_bf, gpost, *, tm=256):
    rows, d_model = x.shape
    tm = min(tm, rows)
    assert rows % tm == 0
    d_lin = o_lin.shape[1]
    row_spec = lambda w: pl.BlockSpec((tm, w), lambda i: (i, 0))
    acts = [o_lin, o_att] + ([] if ga is None else [ga])
    return pl.pallas_call(
        functools.partial(_out_proj_body, d_lin=d_lin, gated=ga is None),
        grid=(rows // tm,),
        in_specs=[row_spec(a.shape[1]) for a in acts] + [
            row_spec(d_model),
            pl.BlockSpec(w_bf.shape, lambda i: (0, 0)),
            pl.BlockSpec((1, d_model), lambda i: (0, 0)),
        ],
        out_specs=row_spec(d_model),
        out_shape=jax.ShapeDtypeStruct((rows, d_model), F32),
        compiler_params=pltpu.CompilerParams(dimension_semantics=("arbitrary",)),
        name="out_proj",
    )(*acts, x, w_bf, gpost)


def kernel(x_prompt, x_sample, cache_k, cache_v, state_hgrn, page_table, w_in, w_out,
           norm_pre, norm_post, norm_lin_out, lin_lower_bound, rel_bias):
    depth = w_in.shape[0]
    assert depth == 1 and lin_lower_bound.shape[0] == depth + 1
    b, t, d_model = x_prompt.shape
    nb, ts, _ = x_sample.shape
    d_lin = lin_lower_bound.shape[1]
    d_att = w_out.shape[1] - d_lin
    lin_heads = d_lin // LIN_HEAD_DIM
    att_heads = rel_bias.shape[1]
    n_pages = page_table.shape[1]
    page = cache_k.shape[2]
    past_len = n_pages * page
    assert d_att == att_heads * ATT_HEAD_DIM and t % MOBA_BLOCK == 0
    assert past_len % MOBA_BLOCK == 0 and ts <= SUBLANES

    w_in_bf = w_in[0].astype(BF16)
    w_out_bf = w_out[0].astype(BF16)
    gpre, gpost, glin = norm_pre, norm_post, norm_lin_out
    proj = functools.partial(_in_proj, gpre=gpre, w_bf=w_in_bf, llb=lin_lower_bound,
                             d_lin=d_lin, d_att=d_att)

    xp = x_prompt.reshape(b * t, d_model)
    q, g, kk, v, gl, ga, ak, aqt, akt, avt = proj(xp, seq_len=t, tm=512)
    q, g, kk, v, gl, ak = [a.reshape(b, t, -1) for a in (q, g, kk, v, gl, ak)]
    s0 = jnp.zeros((b, lin_heads, LIN_HEAD_DIM, LIN_HEAD_DIM), F32)
    o_lin, s_prompt = _hgrn_tiles(q, g, kk, v, gl, s0, glin, chunk=32, t_tile=math.gcd(t, 1024))
    n_bias = 6
    assert (n_bias - 1) * MOBA_BLOCK - (MOBA_BLOCK - 1) >= MAX_DISTANCE
    bias = _prompt_bias(rel_bias, n_bias)
    o_att = _moba_prompt(aqt, ak, avt, bias, ga)
    y_prompt = _out_proj(o_lin.reshape(b * t, d_lin), o_att.reshape(b * t, d_att),
                         None, xp, w_out_bf, gpost, tm=2048).reshape(b, t, d_model)

    rows = SUBLANES
    xs = jnp.pad(x_sample, ((0, 0), (0, rows - ts), (0, 0))).reshape(nb * rows, d_model)
    qs, gs, kks, vs, gls, gas, aqs, aks, avs = [a.reshape(nb, rows, -1) for a in proj(xs)]
    o_lin_s, s_sample = _hgrn_step(qs, gs, kks, vs, gls, state_hgrn[0], glin, valid=ts,
                                   n_seq=math.gcd(nb, 32))
    bias_s = _sample_bias(rel_bias, past_len=past_len, n_valid=ts, rows=rows)
    cache_kt = cache_k[0].transpose(0, 2, 3, 1).reshape(-1, d_att, page)
    cache_vt = cache_v[0].transpose(0, 2, 3, 1).reshape(-1, d_att, page)
    o_att_s = _moba_sample(aqs, aks, avs, cache_kt, cache_vt, page_table, bias_s, heads=att_heads)
    y_s = _out_proj(o_lin_s.reshape(nb * rows, d_lin), o_att_s.reshape(nb * rows, d_att),
                    gas.reshape(nb * rows, d_att), xs, w_out_bf, gpost)
    y_sample = y_s.reshape(nb, rows, d_model)[:, :ts]

    def prompt_kv(a):
        return a.reshape(1, b, att_heads, ATT_HEAD_DIM, t).transpose(0, 1, 4, 2, 3)

    kvs_shape = (1, nb, ts, att_heads, ATT_HEAD_DIM)
    return (y_prompt, y_sample, prompt_kv(akt), prompt_kv(avt), s_prompt[None],
            aks[:, :ts].reshape(kvs_shape), avs[:, :ts].reshape(kvs_shape), s_sample[None])
```
